```python
import jax
import jax.numpy as jnp
from jax import lax
import numpy as np

D_MODEL = 1024
BATCH = 2
SEQ = 16384
DEPTH = 2

N_A_LAYERS = DEPTH // 2
N_B_LAYERS = DEPTH - N_A_LAYERS
EPS = 1e-6

N_HEADS_A = 4
DV_A = D_MODEL // N_HEADS_A
DQK_A = DV_A // 2
CHUNK_A = 64
GATE_CAP = 15.0
QK_A = N_HEADS_A * DQK_A
A_SPLITS = (QK_A, 2 * QK_A, 2 * QK_A + D_MODEL, 2 * QK_A + 2 * D_MODEL)
A_IN = 2 * QK_A + 2 * D_MODEL + 2 * N_HEADS_A

DH_B = 64
N_HEADS_B = D_MODEL // DH_B
QBLOCK = 128
B_IN = 2 * D_MODEL
KV_OUT = 2 * D_MODEL + N_HEADS_B

N_GROUPS = 4
EXP_PER_GROUP = 4
N_EXPERTS = N_GROUPS * EXP_PER_GROUP
TOP_K = 2
D_EXPERT = D_MODEL // 2

kernel_name = 'yoco_mlstm_fox_hier_moe'


def rmsnorm(x, g):
    xf = x.astype(jnp.float32)
    y = xf * lax.rsqrt(jnp.mean(xf * xf, axis=-1, keepdims=True) + EPS)
    return y.astype(x.dtype) * g


def softcap(z, cap):
    return cap * jnp.tanh(z / cap)


def mlstm_chunkwise(q, k, v, log_i, log_f):
    bsz, nh, seq, dk = q.shape
    dv = v.shape[-1]
    nc = seq // CHUNK_A

    def to_chunks(t):
        return jnp.moveaxis(t.reshape((bsz, nh, nc, CHUNK_A) + t.shape[3:]), 2, 0)

    causal = jnp.tril(jnp.ones((CHUNK_A, CHUNK_A), dtype=bool))

    def step(carry, inp):
        c_mat, n_vec, m_run = carry
        qb, kb, vb, ib, fb = inp
        b = jnp.cumsum(fb, axis=-1)
        log_d = jnp.where(causal, b[..., :, None] - b[..., None, :] + ib[..., None, :], -jnp.inf)
        log_inter = b + m_run[..., None]
        m_t = jnp.maximum(log_inter, jnp.max(log_d, axis=-1))
        d_mat = jnp.exp(log_d - m_t[..., None])
        w_inter = jnp.exp(log_inter - m_t)
        s = jnp.einsum('bhtd,bhsd->bhts', qb, kb) * d_mat
        num = jnp.einsum('bhts,bhsv->bhtv', s, vb) + w_inter[..., None] * jnp.einsum('bhtd,bhdv->bhtv', qb, c_mat)
        den = jnp.sum(s, axis=-1) + w_inter * jnp.einsum('bhtd,bhd->bht', qb, n_vec)
        h = num / jnp.maximum(jnp.abs(den), jnp.exp(-m_t))[..., None]
        b_end = b[..., -1]
        log_w = b_end[..., None] - b + ib
        m_new = jnp.maximum(b_end + m_run, jnp.max(log_w, axis=-1))
        w = jnp.exp(log_w - m_new[..., None])
        decay = jnp.exp(b_end + m_run - m_new)
        kw = kb * w[..., None]
        c_new = decay[..., None, None] * c_mat + jnp.einsum('bhsd,bhsv->bhdv', kw, vb)
        n_new = decay[..., None] * n_vec + jnp.sum(kw, axis=2)
        return (c_new, n_new, m_new), h

    init = (jnp.zeros((bsz, nh, dk, dv), jnp.float32),
            jnp.zeros((bsz, nh, dk), jnp.float32),
            jnp.zeros((bsz, nh), jnp.float32))
    _, hs = lax.scan(step, init, (to_chunks(q), to_chunks(k), to_chunks(v), to_chunks(log_i), to_chunks(log_f)))
    return jnp.moveaxis(hs, 0, 2).reshape(bsz, nh, seq, dv)


def mlstm_mixer(xn, w_in, b_gate, head_gain, w_out):
    bsz, seq, _ = xn.shape
    q, k, v, o, gates = jnp.split(xn @ w_in, A_SPLITS, axis=-1)

    def heads(t, d):
        return t.reshape(bsz, seq, N_HEADS_A, d).transpose(0, 2, 1, 3).astype(jnp.float32)

    qh = heads(q, DQK_A)
    kh = heads(k, DQK_A) * (DQK_A ** -0.5)
    vh = heads(v, DV_A)
    g = softcap((gates + b_gate).astype(jnp.float32), GATE_CAP).transpose(0, 2, 1)
    log_i = g[:, :N_HEADS_A]
    log_f = jax.nn.log_sigmoid(g[:, N_HEADS_A:])
    h = mlstm_chunkwise(qh, kh, vh, log_i, log_f).transpose(0, 2, 1, 3)
    h = h * lax.rsqrt(jnp.mean(h * h, axis=-1, keepdims=True) + EPS)
    h = h.reshape(bsz, seq, D_MODEL).astype(xn.dtype) * head_gain * jax.nn.sigmoid(o)
    return h @ w_out


def shared_kv(hn, kv_w, kv_b_f, k_gain):
    bsz, seq, _ = hn.shape
    k, v, fg = jnp.split(hn @ kv_w, (D_MODEL, 2 * D_MODEL), axis=-1)
    k = rmsnorm(k.reshape(bsz, seq, N_HEADS_B, DH_B), k_gain).transpose(0, 2, 1, 3)
    v = v.reshape(bsz, seq, N_HEADS_B, DH_B).transpose(0, 2, 1, 3)
    log_f = jax.nn.log_sigmoid((fg + kv_b_f).astype(jnp.float32))
    c = jnp.cumsum(log_f, axis=1).transpose(0, 2, 1)
    return k, v, c


def fox_mixer(xn, k, v, c, w_in, q_gain, w_out):
    bsz, seq, _ = xn.shape
    q, og = jnp.split(xn @ w_in, (D_MODEL,), axis=-1)
    q = rmsnorm(q.reshape(bsz, seq, N_HEADS_B, DH_B), q_gain).transpose(0, 2, 1, 3) * (DH_B ** -0.5)
    outs = []
    for qb in range(seq // QBLOCK):
        lo = qb * QBLOCK
        hi = lo + QBLOCK
        logits = jnp.einsum('bhtd,bhsd->bhts', q[:, :, lo:hi], k[:, :, :hi]).astype(jnp.float32)
        logits = logits + c[:, :, lo:hi, None] - c[:, :, None, :hi]
        mask = jnp.arange(hi)[None, :] <= (lo + jnp.arange(QBLOCK))[:, None]
        p = jax.nn.softmax(jnp.where(mask, logits, -jnp.inf), axis=-1)
        outs.append(jnp.einsum('bhts,bhsd->bhtd', p.astype(v.dtype), v[:, :, :hi]))
    o = jnp.concatenate(outs, axis=2).transpose(0, 2, 1, 3).reshape(bsz, seq, D_MODEL)
    return (o * jax.nn.sigmoid(og)) @ w_out


def hier_moe(xn, w_grp, b_grp, w_exp, b_exp, w_gate, w_up, w_down):
    bsz, seq, d = xn.shape
    t = xn.reshape(-1, d)
    g_prob = jax.nn.softmax((t @ w_grp + b_grp).astype(jnp.float32), axis=-1)
    g_p, g_idx = lax.top_k(g_prob, 1)
    e_logits = (t @ w_exp + b_exp).astype(jnp.float32).reshape(-1, N_GROUPS, EXP_PER_GROUP)
    e_sel = jnp.einsum('tge,tg->te', e_logits, jax.nn.one_hot(g_idx[:, 0], N_GROUPS, dtype=jnp.float32))
    e_p, e_idx = lax.top_k(jax.nn.softmax(e_sel, axis=-1), TOP_K)
    e_p = e_p / jnp.sum(e_p, axis=-1, keepdims=True)
    weights = g_p * e_p
    expert_id = g_idx * EXP_PER_GROUP + e_idx
    combine = jnp.einsum('tk,tke->te', weights, jax.nn.one_hot(expert_id, N_EXPERTS, dtype=jnp.float32)).astype(t.dtype)
    y = jnp.zeros_like(t)
    for e in range(N_EXPERTS):
        hdn = jax.nn.silu(t @ w_gate[e]) * (t @ w_up[e])
        y = y + combine[:, e:e + 1] * (hdn @ w_down[e])
    return y.reshape(bsz, seq, d)


def setup_inputs(seed: int = 0) -> dict:
    key = jax.random.key(seed)
    ks = jax.random.split(key, 24)
    f32 = jnp.float32

    def nrm(k, shape, scale):
        return jax.random.normal(k, shape, f32) * scale

    def gain(k, shape):
        return 1.0 + 0.05 * jax.random.normal(k, shape, f32)

    x = nrm(ks[0], (BATCH, SEQ, D_MODEL), 1.0)
    norm_mix = gain(ks[1], (DEPTH, D_MODEL))
    norm_ffn = gain(ks[2], (DEPTH, D_MODEL))
    a_w_in = nrm(ks[3], (N_A_LAYERS, D_MODEL, A_IN), D_MODEL ** -0.5)
    a_b_gate = jnp.concatenate([nrm(ks[4], (N_A_LAYERS, N_HEADS_A), 0.1),
                                3.0 + nrm(ks[5], (N_A_LAYERS, N_HEADS_A), 0.5)], axis=-1)
    a_head_gain = gain(ks[6], (N_A_LAYERS, D_MODEL))
    a_w_out = nrm(ks[7], (N_A_LAYERS, D_MODEL, D_MODEL), D_MODEL ** -0.5)
    kv_norm = gain(ks[8], (D_MODEL,))
    kv_w = nrm(ks[9], (D_MODEL, KV_OUT), D_MODEL ** -0.5)
    kv_b_f = 2.0 + nrm(ks[10], (N_HEADS_B,), 0.5)
    kv_k_gain = gain(ks[11], (DH_B,))
    b_w_in = nrm(ks[12], (N_B_LAYERS, D_MODEL, B_IN), D_MODEL ** -0.5)
    b_q_gain = gain(ks[13], (N_B_LAYERS, DH_B))
    b_w_out = nrm(ks[14], (N_B_LAYERS, D_MODEL, D_MODEL), D_MODEL ** -0.5)
    moe_w_grp = nrm(ks[15], (DEPTH, D_MODEL, N_GROUPS), D_MODEL ** -0.5)
    moe_b_grp = nrm(ks[16], (DEPTH, N_GROUPS), 0.01)
    moe_w_exp = nrm(ks[17], (DEPTH, D_MODEL, N_EXPERTS), D_MODEL ** -0.5)
    moe_b_exp = nrm(ks[18], (DEPTH, N_EXPERTS), 0.01)
    moe_w_gate = nrm(ks[19], (DEPTH, N_EXPERTS, D_MODEL, D_EXPERT), D_MODEL ** -0.5)
    moe_w_up = nrm(ks[20], (DEPTH, N_EXPERTS, D_MODEL, D_EXPERT), D_MODEL ** -0.5)
    moe_w_down = nrm(ks[21], (DEPTH, N_EXPERTS, D_EXPERT, D_MODEL), D_EXPERT ** -0.5)
    norm_final = gain(ks[22], (D_MODEL,))
    return {'x': x, 'norm_mix': norm_mix, 'norm_ffn': norm_ffn,
            'a_w_in': a_w_in, 'a_b_gate': a_b_gate, 'a_head_gain': a_head_gain, 'a_w_out': a_w_out,
            'kv_norm': kv_norm, 'kv_w': kv_w, 'kv_b_f': kv_b_f, 'kv_k_gain': kv_k_gain,
            'b_w_in': b_w_in, 'b_q_gain': b_q_gain, 'b_w_out': b_w_out,
            'moe_w_grp': moe_w_grp, 'moe_b_grp': moe_b_grp, 'moe_w_exp': moe_w_exp, 'moe_b_exp': moe_b_exp,
            'moe_w_gate': moe_w_gate, 'moe_w_up': moe_w_up, 'moe_w_down': moe_w_down,
            'norm_final': norm_final}


def reference(x, norm_mix, norm_ffn, a_w_in, a_b_gate, a_head_gain, a_w_out, kv_norm, kv_w, kv_b_f, kv_k_gain,
              b_w_in, b_q_gain, b_w_out, moe_w_grp, moe_b_grp, moe_w_exp, moe_b_exp, moe_w_gate, moe_w_up,
              moe_w_down, norm_final):
    h = x
    k_sh = None
    v_sh = None
    c_sh = None
    for layer in range(DEPTH):
        xn = rmsnorm(h, norm_mix[layer])
        if layer < N_A_LAYERS:
            h = h + mlstm_mixer(xn, a_w_in[layer], a_b_gate[layer], a_head_gain[layer], a_w_out[layer])
        else:
            j = layer - N_A_LAYERS
            h = h + fox_mixer(xn, k_sh, v_sh, c_sh, b_w_in[j], b_q_gain[j], b_w_out[j])
        h = h + hier_moe(rmsnorm(h, norm_ffn[layer]), moe_w_grp[layer], moe_b_grp[layer], moe_w_exp[layer],
                         moe_b_exp[layer], moe_w_gate[layer], moe_w_up[layer], moe_w_down[layer])
        if layer == N_A_LAYERS - 1:
            k_sh, v_sh, c_sh = shared_kv(rmsnorm(h, kv_norm), kv_w, kv_b_f, kv_k_gain)
    return rmsnorm(h, norm_final)
```

```python
import functools
import math

import jax
import jax.numpy as jnp
from jax import lax
from jax.experimental import pallas as pl
from jax.experimental.pallas import tpu as pltpu

F32 = jnp.float32
BF16 = jnp.bfloat16

D_MODEL = 1024
EPS = 1e-6

N_HEADS_A = 4
DV_A = D_MODEL // N_HEADS_A
DQK_A = DV_A // 2
QK_A = N_HEADS_A * DQK_A
GATE_CAP = 15.0
CHUNK = 128
CHUNK_SHIFT = CHUNK.bit_length() - 1

DH_B = 64
N_HEADS_B = D_MODEL // DH_B
LOG2E = 1.4426950408889634
BIAS_LANE = DH_B
NEG_BIG = -1e30

N_GROUPS = 4
EXP_PER_GROUP = 4
N_EXPERTS = N_GROUPS * EXP_PER_GROUP
D_EXPERT = D_MODEL // 2
ROUTE_LANE0 = N_GROUPS

LANES = 128
VMEM_LIMIT = 56 * 1024 * 1024


def _params(sem, vmem=VMEM_LIMIT):
    return pltpu.CompilerParams(dimension_semantics=sem, vmem_limit_bytes=vmem)


def _dot(a, b):
    return jnp.dot(a, b, preferred_element_type=F32)


def _dot_nt(a, b):
    return lax.dot_general(a, b, (((1,), (1,)), ((), ())), preferred_element_type=F32)


def _dot_tn(a, b):
    return lax.dot_general(a, b, (((0,), (0,)), ((), ())), preferred_element_type=F32)


def _split2(x):
    hi = x.astype(BF16)
    lo = (x - hi.astype(F32)).astype(BF16)
    return hi, lo


def _split3(x):
    hi = x.astype(BF16)
    r = x - hi.astype(F32)
    mid = r.astype(BF16)
    lo = (r - mid.astype(F32)).astype(BF16)
    return hi, mid, lo


def _dot_x3(x_hi, x_lo, w_ref):
    return _dot(x_hi, w_ref[0]) + _dot(x_lo, w_ref[0]) + _dot(x_hi, w_ref[1])


def _dot_nt_x3(w_ref, x_hi, x_lo):
    return _dot_nt(w_ref[0], x_hi) + _dot_nt(w_ref[0], x_lo) + _dot_nt(w_ref[1], x_hi)


def _dot_exact_rhs(a, b_exact):
    a0, a1, a2 = _split3(a)
    return _dot(a0, b_exact) + _dot(a1, b_exact) + _dot(a2, b_exact)


def _dot_exact_lhs(a_exact, b):
    b0, b1, b2 = _split3(b)
    return _dot(a_exact, b0) + _dot(a_exact, b1) + _dot(a_exact, b2)


def _rms_scale(x):
    return lax.rsqrt(jnp.mean(x * x, axis=-1, keepdims=True) + EPS)


def _log_sigmoid(z):
    return jnp.minimum(z, 0.0) - jnp.log(1.0 + jnp.exp(-jnp.abs(z)))


def _sigmoid(z):
    return 1.0 / (1.0 + jnp.exp(-z))


def _softcap(z):
    return GATE_CAP * jnp.tanh(z / GATE_CAP)


def _proj_a_kernel(x_ref, g_ref, wq_ref, wk_ref, wv_ref, wo_ref, wgc_ref, wgr_ref, bgc_ref, bgr_ref,
                   q_ref, k_ref, v_ref, o_ref, gc_ref, gr_ref, *, tm):
    x = x_ref[...]
    xn = x * _rms_scale(x) * g_ref[...]
    xh, xl = _split2(xn)
    q_ref[...] = _dot(xh, wq_ref[...]).astype(BF16)
    k_ref[...] = (_dot(xh, wk_ref[...]) * (DQK_A ** -0.5)).astype(BF16)
    v_ref[...] = _dot(xh, wv_ref[...]).astype(BF16)
    o_ref[...] = _dot(xh, wo_ref[...])

    zc = _softcap(_dot_x3(xh, xl, wgc_ref) + bgc_ref[...])
    zr = _softcap(_dot_nt_x3(wgr_ref, xh, xl) + bgr_ref[...])
    lane = lax.broadcasted_iota(jnp.int32, zc.shape, 1)
    sub = lax.broadcasted_iota(jnp.int32, zr.shape, 0)
    vc = jnp.where(lane < N_HEADS_A, zc, _log_sigmoid(zc))
    vr = jnp.where(sub < N_HEADS_A, zr, _log_sigmoid(zr))
    ti = lax.broadcasted_iota(jnp.int32, (tm, tm), 0)
    tj = lax.broadcasted_iota(jnp.int32, (tm, tm), 1)
    same = (ti >> CHUNK_SHIFT) == (tj >> CHUNK_SHIFT)
    tril = jnp.where(same & (tj <= ti), 1.0, 0.0).astype(BF16)
    triu = jnp.where(same & (ti <= tj), 1.0, 0.0).astype(BF16)
    cc = _dot_exact_lhs(tril, vc)
    cr = _dot_exact_rhs(vr, triu)
    gc_ref[...] = jnp.where(lane < N_HEADS_A, vc, cc)
    gr_ref[...] = jnp.where(sub < N_HEADS_A, vr, cr)


def _proj_a(x2d, g, wq, wk, wv, wo, wgc, wgr, bgc, bgr, tm):
    t = x2d.shape[0]
    row = lambda i: (i, 0)
    const2 = lambda i: (0, 0)
    const3 = lambda i: (0, 0, 0)
    return pl.pallas_call(
        functools.partial(_proj_a_kernel, tm=tm),
        grid=(t // tm,),
        in_specs=[
            pl.BlockSpec((tm, D_MODEL), row),
            pl.BlockSpec((1, D_MODEL), const2),
            pl.BlockSpec((D_MODEL, QK_A), const2),
            pl.BlockSpec((D_MODEL, QK_A), const2),
            pl.BlockSpec((D_MODEL, D_MODEL), const2),
            pl.BlockSpec((D_MODEL, D_MODEL), const2),
            pl.BlockSpec((2, D_MODEL, LANES), const3),
            pl.BlockSpec((2, 8, D_MODEL), const3),
            pl.BlockSpec((1, LANES), const2),
            pl.BlockSpec((8, 1), const2),
        ],
        out_specs=[
            pl.BlockSpec((tm, QK_A), row),
            pl.BlockSpec((tm, QK_A), row),
            pl.BlockSpec((tm, D_MODEL), row),
            pl.BlockSpec((tm, D_MODEL), row),
            pl.BlockSpec((tm, LANES), row),
            pl.BlockSpec((8, tm), lambda i: (0, i)),
        ],
        out_shape=[
            jax.ShapeDtypeStruct((t, QK_A), BF16),
            jax.ShapeDtypeStruct((t, QK_A), BF16),
            jax.ShapeDtypeStruct((t, D_MODEL), BF16),
            jax.ShapeDtypeStruct((t, D_MODEL), F32),
            jax.ShapeDtypeStruct((t, LANES), F32),
            jax.ShapeDtypeStruct((8, t), F32),
        ],
        compiler_params=_params(("parallel",)),
        name="proj_a",
    )(x2d, g, wq, wk, wv, wo, wgc, wgr, bgc, bgr)


def _mlstm_kernel(q_ref, k_ref, v_ref, gc_ref, gr_ref, h_ref, c_scr, n_scr, m_scr):
    @pl.when(pl.program_id(1) == 0)
    def _():
        c_scr[...] = jnp.zeros_like(c_scr)
        n_scr[...] = jnp.zeros_like(n_scr)
        m_scr[...] = jnp.zeros_like(m_scr)

    ti = lax.broadcasted_iota(jnp.int32, (CHUNK, CHUNK), 0)
    si = lax.broadcasted_iota(jnp.int32, (CHUNK, CHUNK), 1)
    causal = si <= ti
    gc = gc_ref[...]
    gr = gr_ref[...]
    for h in range(N_HEADS_A):
        li_c = gc[:, h:h + 1]
        b_c = gc[:, N_HEADS_A + h:N_HEADS_A + h + 1]
        li_r = gr[h:h + 1, :]
        b_r = gr[N_HEADS_A + h:N_HEADS_A + h + 1, :]
        a_r = li_r - b_r
        a_c = li_c - b_c
        b_end = b_c[CHUNK - 1:CHUNK, :]
        m_run = m_scr[h][:, 0:1]

        log_inter = b_c + m_run
        log_d = b_c + a_r
        m_intra = jnp.max(jnp.where(causal, log_d, -jnp.inf), axis=-1, keepdims=True)
        m_t = jnp.maximum(log_inter, m_intra)
        d_mat = jnp.where(causal, jnp.exp(log_d - m_t), 0.0)
        w_inter = jnp.exp(log_inter - m_t)

        qh = q_ref[:, h * DQK_A:(h + 1) * DQK_A]
        kh = k_ref[:, h * DQK_A:(h + 1) * DQK_A]
        vh = v_ref[:, h * DV_A:(h + 1) * DV_A]
        c_mat = c_scr[h]
        n_vec = n_scr[h]
        s = _dot_nt(qh, kh) * d_mat
        num = _dot(s.astype(BF16), vh) + w_inter * _dot(qh, c_mat.astype(BF16))
        qn = jnp.sum(qh.astype(F32) * n_vec, axis=-1, keepdims=True)
        den = jnp.sum(s, axis=-1, keepdims=True) + w_inter * qn
        hh = num / jnp.maximum(jnp.abs(den), jnp.exp(-m_t))
        h_ref[:, h * DV_A:(h + 1) * DV_A] = hh * _rms_scale(hh)

        log_w_r = b_end + a_r
        m_new = jnp.maximum(b_end + m_run, jnp.max(log_w_r, axis=-1, keepdims=True))
        w_c = jnp.exp(b_end + a_c - m_new)
        decay = jnp.exp(b_end + m_run - m_new)
        kw = kh.astype(F32) * w_c
        c_scr[h] = decay * c_mat + _dot_tn(kw.astype(BF16), vh)
        n_scr[h] = decay * n_vec + jnp.sum(kw, axis=0, keepdims=True)
        m_scr[h] = jnp.broadcast_to(m_new, (1, LANES))


def _mlstm(q, k, v, gc, gr, bsz, seq):
    nc = seq // CHUNK
    row = lambda b, c: (b * nc + c, 0)
    return pl.pallas_call(
        _mlstm_kernel,
        grid=(bsz, nc),
        in_specs=[
            pl.BlockSpec((CHUNK, QK_A), row),
            pl.BlockSpec((CHUNK, QK_A), row),
            pl.BlockSpec((CHUNK, D_MODEL), row),
            pl.BlockSpec((CHUNK, LANES), row),
            pl.BlockSpec((8, CHUNK), lambda b, c: (0, b * nc + c)),
        ],
        out_specs=pl.BlockSpec((CHUNK, D_MODEL), row),
        out_shape=jax.ShapeDtypeStruct((bsz * seq, D_MODEL), F32),
        scratch_shapes=[
            pltpu.VMEM((N_HEADS_A, DQK_A, DV_A), F32),
            pltpu.VMEM((N_HEADS_A, 1, DQK_A), F32),
            pltpu.VMEM((N_HEADS_A, 1, LANES), F32),
        ],
        compiler_params=_params(("arbitrary", "arbitrary")),
        name="mlstm",
    )(q, k, v, gc, gr)


def _route(logits):
    lane = lax.broadcasted_iota(jnp.int32, logits.shape, 1)
    big = jnp.int32(10 ** 6)

    def top(mask):
        mx = jnp.max(jnp.where(mask, logits, -jnp.inf), axis=-1, keepdims=True)
        idx = jnp.min(jnp.where(mask & (logits == mx), lane, big), axis=-1, keepdims=True)
        return mx, idx

    gmask = lane < N_GROUPS
    gmax, gidx = top(gmask)
    g_p = 1.0 / jnp.sum(jnp.where(gmask, jnp.exp(logits - gmax), 0.0), axis=-1, keepdims=True)
    lo = ROUTE_LANE0 + EXP_PER_GROUP * gidx
    emask = (lane >= lo) & (lane < lo + EXP_PER_GROUP)
    e1, i1 = top(emask)
    e2, i2 = top(emask & (lane != i1))
    esum = jnp.sum(jnp.where(emask, jnp.exp(logits - e1), 0.0), axis=-1, keepdims=True)
    p1 = 1.0 / esum
    p2 = jnp.exp(e2 - e1) / esum
    w1 = g_p * (p1 / (p1 + p2))
    w2 = g_p * (p2 / (p1 + p2))
    return jnp.where(lane == i1, w1, 0.0) + jnp.where(lane == i2, w2, 0.0)


def _ffn_norm_and_route(h, gn_ref, wr_ref, br_ref, xn_ref, comb_ref):
    xn = h * _rms_scale(h) * gn_ref[...]
    xh, xl = _split2(xn)
    xn_ref[...] = xh
    comb_ref[...] = _route(_dot_x3(xh, xl, wr_ref) + br_ref[...])


def _out_a_kernel(hn_ref, o_ref, x_ref, hg_ref, wout_ref, gn_ref, wr_ref, br_ref,
                  h1_ref, xn_ref, comb_ref):
    z = (hn_ref[...] * hg_ref[...] * _sigmoid(o_ref[...])).astype(BF16)
    h1 = x_ref[...] + _dot(z, wout_ref[...])
    h1_ref[...] = h1
    _ffn_norm_and_route(h1, gn_ref, wr_ref, br_ref, xn_ref, comb_ref)


def _out_a(hn, o, x2d, hg, wout, gn, wr, br, tm):
    t = x2d.shape[0]
    row = lambda i: (i, 0)
    const2 = lambda i: (0, 0)
    const3 = lambda i: (0, 0, 0)
    return pl.pallas_call(
        _out_a_kernel,
        grid=(t // tm,),
        in_specs=[
            pl.BlockSpec((tm, D_MODEL), row),
            pl.BlockSpec((tm, D_MODEL), row),
            pl.BlockSpec((tm, D_MODEL), row),
            pl.BlockSpec((1, D_MODEL), const2),
            pl.BlockSpec((D_MODEL, D_MODEL), const2),
            pl.BlockSpec((1, D_MODEL), const2),
            pl.BlockSpec((2, D_MODEL, LANES), const3),
            pl.BlockSpec((1, LANES), const2),
        ],
        out_specs=[
            pl.BlockSpec((tm, D_MODEL), row),
            pl.BlockSpec((tm, D_MODEL), row),
            pl.BlockSpec((tm, LANES), row),
        ],
        out_shape=[
            jax.ShapeDtypeStruct((t, D_MODEL), F32),
            jax.ShapeDtypeStruct((t, D_MODEL), BF16),
            jax.ShapeDtypeStruct((t, LANES), F32),
        ],
        compiler_params=_params(("parallel",)),
        name="out_a",
    )(hn, o, x2d, hg, wout, gn, wr, br)


def _moe_kernel(xn_ref, h_ref, comb_ref, wg_ref, wu_ref, wd_ref, gf_ref, out_ref, acc_ref, *, final_norm):
    e = pl.program_id(1)

    @pl.when(e == 0)
    def _():
        acc_ref[...] = h_ref[...]

    x = xn_ref[...]
    comb = comb_ref[...]
    lane = lax.broadcasted_iota(jnp.int32, comb.shape, 1)
    w_e = jnp.sum(jnp.where(lane == ROUTE_LANE0 + e, comb, 0.0), axis=-1, keepdims=True)
    g = _dot(x, wg_ref[0])
    u = _dot(x, wu_ref[0])
    hdn = (g * _sigmoid(g) * u * w_e).astype(BF16)
    acc_ref[...] += _dot(hdn, wd_ref[0])

    @pl.when(e == N_EXPERTS - 1)
    def _():
        y = acc_ref[...]
        if final_norm:
            y = y * _rms_scale(y) * gf_ref[...]
        out_ref[...] = y


def _moe(xn, h, comb, wg, wu, wd, gf, tm, final_norm):
    t = xn.shape[0]
    row = lambda i, e: (i, 0)
    wsel = lambda i, e: (e, 0, 0)
    return pl.pallas_call(
        functools.partial(_moe_kernel, final_norm=final_norm),
        grid=(t // tm, N_EXPERTS),
        in_specs=[
            pl.BlockSpec((tm, D_MODEL), row),
            pl.BlockSpec((tm, D_MODEL), row),
            pl.BlockSpec((tm, LANES), row),
            pl.BlockSpec((1, D_MODEL, D_EXPERT), wsel),
            pl.BlockSpec((1, D_MODEL, D_EXPERT), wsel),
            pl.BlockSpec((1, D_EXPERT, D_MODEL), wsel),
            pl.BlockSpec((1, D_MODEL), lambda i, e: (0, 0)),
        ],
        out_specs=pl.BlockSpec((tm, D_MODEL), row),
        out_shape=jax.ShapeDtypeStruct((t, D_MODEL), F32),
        scratch_shapes=[pltpu.VMEM((tm, D_MODEL), F32)],
        compiler_params=_params(("parallel", "arbitrary")),
        name="moe_final" if final_norm else "moe",
    )(xn, h, comb, wg, wu, wd, gf)


def _head_rms(x, gsum_ref, gexp_ref):
    s_hi, s_lo = _split2(x * x)
    ms = _dot(s_hi, gsum_ref[...]) + _dot(s_lo, gsum_ref[...])
    r_hi, r_lo = _split2(lax.rsqrt(ms + EPS))
    return _dot(r_hi, gexp_ref[...]) + _dot(r_lo, gexp_ref[...])


def _proj_kvq_kernel(h_ref, gkv_ref, gq_ref, wk_ref, wvt_ref, wf_ref, bf_ref, wq_ref, wogt_ref,
                     kgain_ref, qgain_ref, gsum_ref, gexp_ref, sel_ref,
                     kaug_ref, vt_ref, qaug_ref, sgt_ref, carry_ref, *, tm):
    @pl.when(pl.program_id(1) == 0)
    def _():
        carry_ref[...] = jnp.zeros_like(carry_ref)

    hres = h_ref[...]
    y = hres * _rms_scale(hres)
    a = y * gkv_ref[...]
    ah, al = _split2(a)
    bh = (y * gq_ref[...]).astype(BF16)

    logf = _log_sigmoid(_dot_x3(ah, al, wf_ref) + bf_ref[...])
    ti = lax.broadcasted_iota(jnp.int32, (tm, tm), 0)
    tj = lax.broadcasted_iota(jnp.int32, (tm, tm), 1)
    tril = jnp.where(tj <= ti, 1.0, 0.0).astype(BF16)
    c = _dot_exact_lhs(tril, logf) + carry_ref[...]
    carry_ref[...] = c[tm - 1:tm, :]

    n0, n1, n2 = _split3(c * (-LOG2E))
    lane = lax.broadcasted_iota(jnp.int32, (tm, LANES), 1)
    packed = jnp.where(lane < N_HEADS_B, n0.astype(F32),
                       jnp.where(lane < 2 * N_HEADS_B, pltpu.roll(n1.astype(F32), N_HEADS_B, 1),
                                 pltpu.roll(n2.astype(F32), 2 * N_HEADS_B, 1)))
    packed = jnp.where(lane < 3 * N_HEADS_B, packed, 0.0).astype(BF16)
    extras = _dot(packed, sel_ref[...])

    k = _dot(ah, wk_ref[...])
    kn = k * _head_rms(k, gsum_ref, gexp_ref) * kgain_ref[...]
    q = _dot(bh, wq_ref[...])
    qn = q * _head_rms(q, gsum_ref, gexp_ref) * qgain_ref[...]
    ones = jnp.where((lane >= BIAS_LANE) & (lane < BIAS_LANE + 3), 1.0, 0.0)
    for j in range(N_HEADS_B // 2):
        kj = kn[:, j * LANES:(j + 1) * LANES]
        qj = qn[:, j * LANES:(j + 1) * LANES]
        for half, (kk, qq) in enumerate(((kj, qj), (pltpu.roll(kj, DH_B, 1), pltpu.roll(qj, DH_B, 1)))):
            hd = 2 * j + half
            ex = extras[:, hd * LANES:(hd + 1) * LANES]
            kaug_ref[0, hd] = jnp.where(lane < DH_B, kk, ex).astype(BF16)
            qaug_ref[0, hd] = jnp.where(lane < DH_B, qq, ones).astype(BF16)

    vt = _dot_nt(wvt_ref[...], ah)
    sgt = _sigmoid(_dot_nt(wogt_ref[...], bh))
    for hd in range(N_HEADS_B):
        vt_ref[0, hd] = vt[hd * DH_B:(hd + 1) * DH_B, :].astype(BF16)
    sgt_ref[0] = sgt


def _proj_kvq(h2, gkv, gq, wk, wvt, wf, bf, wq, wogt, kgain, qgain, gsum, gexp, sel, bsz, seq, tm):
    ns = seq // tm
    row = lambda b, s: (b * ns + s, 0)
    const2 = lambda b, s: (0, 0)
    const3 = lambda b, s: (0, 0, 0)
    return pl.pallas_call(
        functools.partial(_proj_kvq_kernel, tm=tm),
        grid=(bsz, ns),
        in_specs=[
            pl.BlockSpec((tm, D_MODEL), row),
            pl.BlockSpec((1, D_MODEL), const2),
            pl.BlockSpec((1, D_MODEL), const2),
            pl.BlockSpec((D_MODEL, D_MODEL), const2),
            pl.BlockSpec((D_MODEL, D_MODEL), const2),
            pl.BlockSpec((2, D_MODEL, LANES), const3),
            pl.BlockSpec((1, LANES), const2),
            pl.BlockSpec((D_MODEL, D_MODEL), const2),
            pl.BlockSpec((D_MODEL, D_MODEL), const2),
            pl.BlockSpec((1, D_MODEL), const2),
            pl.BlockSpec((1, D_MODEL), const2),
            pl.BlockSpec((D_MODEL, LANES), const2),
            pl.BlockSpec((LANES, D_MODEL), const2),
            pl.BlockSpec((LANES, N_HEADS_B * LANES), const2),
        ],
        out_specs=[
            pl.BlockSpec((1, N_HEADS_B, tm, LANES), lambda b, s: (b, 0, s, 0)),
            pl.BlockSpec((1, N_HEADS_B, DH_B, tm), lambda b, s: (b, 0, 0, s)),
            pl.BlockSpec((1, N_HEADS_B, tm, LANES), lambda b, s: (b, 0, s, 0)),
            pl.BlockSpec((1, D_MODEL, tm), lambda b, s: (b, 0, s)),
        ],
        out_shape=[
            jax.ShapeDtypeStruct((bsz, N_HEADS_B, seq, LANES), BF16),
            jax.ShapeDtypeStruct((bsz, N_HEADS_B, DH_B, seq), BF16),
            jax.ShapeDtypeStruct((bsz, N_HEADS_B, seq, LANES), BF16),
            jax.ShapeDtypeStruct((bsz, D_MODEL, seq), F32),
        ],
        scratch_shapes=[pltpu.VMEM((1, LANES), F32)],
        compiler_params=_params(("arbitrary", "arbitrary")),
        name="proj_kvq",
    )(h2, gkv, gq, wk, wvt, wf, bf, wq, wogt, kgain, qgain, gsum, gexp, sel)


def _attn_kernel(q_ref, k_ref, vt_ref, o_ref, *, tq, tk):
    qi = pl.program_id(2)
    q = q_ref[0, 0]

    def step(j, carry, masked):
        m, l, acc = carry
        off = pl.multiple_of(j * tk, tk)
        kb = k_ref[0, 0, pl.ds(off, tk), :]
        s = _dot_nt(kb, q)
        if masked:
            ki = lax.broadcasted_iota(jnp.int32, (tk, tq), 0)
            qj = lax.broadcasted_iota(jnp.int32, (tk, tq), 1)
            s = jnp.where(ki + (j * tk - qi * tq) <= qj, s, NEG_BIG)
        m_new = jnp.maximum(m, jnp.max(s, axis=0, keepdims=True))
        p = jnp.exp2(s - m_new)
        alpha = jnp.exp2(m - m_new)
        l = alpha * l + jnp.sum(p, axis=0, keepdims=True)
        vb = vt_ref[0, 0, :, pl.ds(off, tk)]
        acc = alpha * acc + _dot(vb, p.astype(BF16))
        return m_new, l, acc

    init = (jnp.full((1, tq), NEG_BIG, F32), jnp.zeros((1, tq), F32), jnp.zeros((DH_B, tq), F32))
    nfull = qi * (tq // tk)
    carry = lax.fori_loop(0, nfull, lambda j, c: step(j, c, False), init)
    for d in range(tq // tk):
        carry = step(nfull + d, carry, True)
    _, l, acc = carry
    o_ref[0, 0] = acc / l


def _attn(qaug, kaug, vt, tq, tk):
    bsz, nh, seq, _ = qaug.shape
    return pl.pallas_call(
        functools.partial(_attn_kernel, tq=tq, tk=tk),
        grid=(bsz, nh, seq // tq),
        in_specs=[
            pl.BlockSpec((1, 1, tq, LANES), lambda b, h, i: (b, h, i, 0)),
            pl.BlockSpec((1, 1, seq, LANES), lambda b, h, i: (b, h, 0, 0)),
            pl.BlockSpec((1, 1, DH_B, seq), lambda b, h, i: (b, h, 0, 0)),
        ],
        out_specs=pl.BlockSpec((1, 1, DH_B, tq), lambda b, h, i: (b, h, 0, i)),
        out_shape=jax.ShapeDtypeStruct((bsz, nh, DH_B, seq), F32),
        compiler_params=_params(("parallel", "parallel", "arbitrary")),
        name="attn",
    )(qaug, kaug, vt)


def _out_b_kernel(ot_ref, sgt_ref, h_ref, wout_ref, gn_ref, wr_ref, br_ref, h3_ref, xn_ref, comb_ref):
    zt = (ot_ref[0] * sgt_ref[0]).astype(BF16)
    h3 = h_ref[...] + _dot_tn(zt, wout_ref[...])
    h3_ref[...] = h3
    _ffn_norm_and_route(h3, gn_ref, wr_ref, br_ref, xn_ref, comb_ref)


def _out_b(ot, sgt, h2, wout, gn, wr, br, bsz, seq, tm):
    ns = seq // tm
    row = lambda b, s: (b * ns + s, 0)
    const2 = lambda b, s: (0, 0)
    const3 = lambda b, s: (0, 0, 0)
    t = bsz * seq
    return pl.pallas_call(
        _out_b_kernel,
        grid=(bsz, ns),
        in_specs=[
            pl.BlockSpec((1, D_MODEL, tm), lambda b, s: (b, 0, s)),
            pl.BlockSpec((1, D_MODEL, tm), lambda b, s: (b, 0, s)),
            pl.BlockSpec((tm, D_MODEL), row),
            pl.BlockSpec((D_MODEL, D_MODEL), const2),
            pl.BlockSpec((1, D_MODEL), const2),
            pl.BlockSpec((2, D_MODEL, LANES), const3),
            pl.BlockSpec((1, LANES), const2),
        ],
        out_specs=[
            pl.BlockSpec((tm, D_MODEL), row),
            pl.BlockSpec((tm, D_MODEL), row),
            pl.BlockSpec((tm, LANES), row),
        ],
        out_shape=[
            jax.ShapeDtypeStruct((t, D_MODEL), F32),
            jax.ShapeDtypeStruct((t, D_MODEL), BF16),
            jax.ShapeDtypeStruct((t, LANES), F32),
        ],
        compiler_params=_params(("parallel", "parallel")),
        name="out_b",
    )(ot, sgt, h2, wout, gn, wr, br)


def _hi_lo(w):
    hi = w.astype(BF16)
    lo = (w - hi.astype(F32)).astype(BF16)
    return jnp.stack([hi, lo])


def _pad_lanes(w, width=LANES):
    return jnp.pad(w, ((0, 0),) * (w.ndim - 1) + ((0, width - w.shape[-1]),))


def _router_params(w_grp, b_grp, w_exp, b_exp):
    w = _pad_lanes(jnp.concatenate([w_grp, w_exp], axis=-1))
    b = _pad_lanes(jnp.concatenate([b_grp, b_exp], axis=-1)[None, :])
    return _hi_lo(w), b


def _tile(seq, pref):
    t = pref
    while seq % t:
        t //= 2
    return t


def kernel(x, norm_mix, norm_ffn, a_w_in, a_b_gate, a_head_gain, a_w_out, kv_norm, kv_w, kv_b_f, kv_k_gain,
           b_w_in, b_q_gain, b_w_out, moe_w_grp, moe_b_grp, moe_w_exp, moe_b_exp, moe_w_gate, moe_w_up,
           moe_w_down, norm_final):
    bsz, seq, _ = x.shape
    t = bsz * seq
    assert seq % CHUNK == 0
    tm = _tile(seq, 512)
    tq = _tile(seq, 256)
    x2d = x.reshape(t, D_MODEL)

    w_in = a_w_in[0]
    wq = w_in[:, :QK_A].astype(BF16)
    wk = w_in[:, QK_A:2 * QK_A].astype(BF16)
    wv = w_in[:, 2 * QK_A:2 * QK_A + D_MODEL].astype(BF16)
    wo = w_in[:, 2 * QK_A + D_MODEL:2 * QK_A + 2 * D_MODEL].astype(BF16)
    wgate = w_in[:, 2 * QK_A + 2 * D_MODEL:]
    wgc = _hi_lo(_pad_lanes(wgate))
    wgr = _hi_lo(wgate.T)
    bgc = _pad_lanes(a_b_gate[0][None, :])
    bgr = a_b_gate[0][:, None]
    q, k, v, o, gc, gr = _proj_a(x2d, norm_mix[0][None, :], wq, wk, wv, wo, wgc, wgr, bgc, bgr, tm)
    hn = _mlstm(q, k, v, gc, gr, bsz, seq)
    wr0, br0 = _router_params(moe_w_grp[0], moe_b_grp[0], moe_w_exp[0], moe_b_exp[0])
    h1, xn1, comb1 = _out_a(hn, o, x2d, a_head_gain[0][None, :], a_w_out[0].astype(BF16),
                            norm_ffn[0][None, :], wr0, br0, tm)
    gf = norm_final[None, :]
    h2 = _moe(xn1, h1, comb1, moe_w_gate[0].astype(BF16), moe_w_up[0].astype(BF16),
              moe_w_down[0].astype(BF16), gf, tm, final_norm=False)

    wkk = kv_w[:, :D_MODEL].astype(BF16)
    wvt = kv_w[:, D_MODEL:2 * D_MODEL].T.astype(BF16)
    wf = _hi_lo(_pad_lanes(kv_w[:, 2 * D_MODEL:]))
    bf = _pad_lanes(kv_b_f[None, :])
    wq1 = b_w_in[0][:, :D_MODEL].astype(BF16)
    wogt = b_w_in[0][:, D_MODEL:].T.astype(BF16)
    kgain = jnp.tile(kv_k_gain, N_HEADS_B)[None, :]
    qgain = jnp.tile(b_q_gain[0], N_HEADS_B)[None, :] * (DH_B ** -0.5 * LOG2E)
    head_of = jnp.arange(D_MODEL) // DH_B
    gsum = (head_of[:, None] == jnp.arange(LANES)[None, :]).astype(BF16) * (1.0 / DH_B)
    gexp = (jnp.arange(LANES)[:, None] == head_of[None, :]).astype(BF16)
    src = jnp.arange(LANES)[:, None]
    dst = jnp.arange(N_HEADS_B * LANES)[None, :]
    sel = ((src < 3 * N_HEADS_B) & (dst == (src % N_HEADS_B) * LANES + BIAS_LANE + src // N_HEADS_B)).astype(BF16)
    kaug, vt, qaug, sgt = _proj_kvq(h2, kv_norm[None, :], norm_mix[1][None, :], wkk, wvt, wf, bf, wq1, wogt,
                                    kgain, qgain, gsum, gexp, sel, bsz, seq, tm)
    ot = _attn(qaug, kaug, vt, tq, tq)
    wr1, br1 = _router_params(moe_w_grp[1], moe_b_grp[1], moe_w_exp[1], moe_b_exp[1])
    h3, xn3, comb3 = _out_b(ot.reshape(bsz, D_MODEL, seq), sgt, h2, b_w_out[0].astype(BF16),
                            norm_ffn[1][None, :], wr1, br1, bsz, seq, tm)
    out = _moe(xn3, h3, comb3, moe_w_gate[1].astype(BF16), moe_w_up[1].astype(BF16),
               moe_w_down[1].astype(BF16), gf, tm, final_norm=True)
    return out.reshape(bsz, seq, D_MODEL)
```

```python
import functools
import math

import jax
import jax.numpy as jnp
from jax import lax
from jax.experimental import pallas as pl
from jax.experimental.pallas import tpu as pltpu

F32 = jnp.float32
BF16 = jnp.bfloat16

D_MODEL = 1024
EPS = 1e-6

N_HEADS_A = 4
DV_A = D_MODEL // N_HEADS_A
DQK_A = DV_A // 2
QK_A = N_HEADS_A * DQK_A
GATE_CAP = 15.0
CHUNK = 128
CHUNK_SHIFT = CHUNK.bit_length() - 1

DH_B = 64
N_HEADS_B = D_MODEL // DH_B
LOG2E = 1.4426950408889634
BIAS_LANE = DH_B
NEG_BIG = -1e30
ATT_BLK = 256
ATT_CHAINS = 8
SKIP_LOG2 = 160.0

N_GROUPS = 4
EXP_PER_GROUP = 4
N_EXPERTS = N_GROUPS * EXP_PER_GROUP
D_EXPERT = D_MODEL // 2
ROUTE_LANE0 = N_GROUPS

LANES = 128
VMEM_LIMIT = 56 * 1024 * 1024


def _params(sem, vmem=VMEM_LIMIT):
    return pltpu.CompilerParams(dimension_semantics=sem, vmem_limit_bytes=vmem)


def _dot(a, b):
    return jnp.dot(a, b, preferred_element_type=F32)


def _dot_nt(a, b):
    return lax.dot_general(a, b, (((1,), (1,)), ((), ())), preferred_element_type=F32)


def _dot_tn(a, b):
    return lax.dot_general(a, b, (((0,), (0,)), ((), ())), preferred_element_type=F32)


def _split2(x):
    hi = x.astype(BF16)
    lo = (x - hi.astype(F32)).astype(BF16)
    return hi, lo


def _split3(x):
    hi = x.astype(BF16)
    r = x - hi.astype(F32)
    mid = r.astype(BF16)
    lo = (r - mid.astype(F32)).astype(BF16)
    return hi, mid, lo


def _dot_x3(x_hi, x_lo, w_ref):
    return _dot(x_hi, w_ref[0]) + _dot(x_lo, w_ref[0]) + _dot(x_hi, w_ref[1])


def _dot_nt_x3(w_ref, x_hi, x_lo):
    return _dot_nt(w_ref[0], x_hi) + _dot_nt(w_ref[0], x_lo) + _dot_nt(w_ref[1], x_hi)


def _dot_exact_rhs(a, b_exact):
    a0, a1, a2 = _split3(a)
    return _dot(a0, b_exact) + _dot(a1, b_exact) + _dot(a2, b_exact)


def _dot_exact_lhs(a_exact, b):
    b0, b1, b2 = _split3(b)
    return _dot(a_exact, b0) + _dot(a_exact, b1) + _dot(a_exact, b2)


def _rms_scale(x):
    return lax.rsqrt(jnp.mean(x * x, axis=-1, keepdims=True) + EPS)


def _log_sigmoid(z):
    return jnp.minimum(z, 0.0) - jnp.log(1.0 + jnp.exp(-jnp.abs(z)))


def _sigmoid(z):
    return 1.0 / (1.0 + jnp.exp(-z))


def _softcap(z):
    return GATE_CAP * jnp.tanh(z / GATE_CAP)


def _proj_a_kernel(x_ref, g_ref, wq_ref, wk_ref, wv_ref, wo_ref, wgc_ref, wgr_ref, bgc_ref, bgr_ref,
                   q_ref, k_ref, v_ref, o_ref, gc_ref, gr_ref, *, tm):
    x = x_ref[...]
    xn = x * _rms_scale(x) * g_ref[...]
    xh, xl = _split2(xn)
    q_ref[...] = _dot(xh, wq_ref[...]).astype(BF16)
    k_ref[...] = (_dot(xh, wk_ref[...]) * (DQK_A ** -0.5)).astype(BF16)
    v_ref[...] = _dot(xh, wv_ref[...]).astype(BF16)
    o_ref[...] = _dot(xh, wo_ref[...])

    zc = _softcap(_dot_x3(xh, xl, wgc_ref) + bgc_ref[...])
    zr = _softcap(_dot_nt_x3(wgr_ref, xh, xl) + bgr_ref[...])
    lane = lax.broadcasted_iota(jnp.int32, zc.shape, 1)
    sub = lax.broadcasted_iota(jnp.int32, zr.shape, 0)
    vc = jnp.where(lane < N_HEADS_A, zc, _log_sigmoid(zc))
    vr = jnp.where(sub < N_HEADS_A, zr, _log_sigmoid(zr))
    ti = lax.broadcasted_iota(jnp.int32, (tm, tm), 0)
    tj = lax.broadcasted_iota(jnp.int32, (tm, tm), 1)
    same = (ti >> CHUNK_SHIFT) == (tj >> CHUNK_SHIFT)
    tril = jnp.where(same & (tj <= ti), 1.0, 0.0).astype(BF16)
    triu = jnp.where(same & (ti <= tj), 1.0, 0.0).astype(BF16)
    cc = _dot_exact_lhs(tril, vc)
    cr = _dot_exact_rhs(vr, triu)
    gc_ref[...] = jnp.where(lane < N_HEADS_A, vc, cc)
    gr_ref[...] = jnp.where(sub < N_HEADS_A, vr, cr)


def _proj_a(x2d, g, wq, wk, wv, wo, wgc, wgr, bgc, bgr, tm):
    t = x2d.shape[0]
    row = lambda i: (i, 0)
    const2 = lambda i: (0, 0)
    const3 = lambda i: (0, 0, 0)
    return pl.pallas_call(
        functools.partial(_proj_a_kernel, tm=tm),
        grid=(t // tm,),
        in_specs=[
            pl.BlockSpec((tm, D_MODEL), row),
            pl.BlockSpec((1, D_MODEL), const2),
            pl.BlockSpec((D_MODEL, QK_A), const2),
            pl.BlockSpec((D_MODEL, QK_A), const2),
            pl.BlockSpec((D_MODEL, D_MODEL), const2),
            pl.BlockSpec((D_MODEL, D_MODEL), const2),
            pl.BlockSpec((2, D_MODEL, LANES), const3),
            pl.BlockSpec((2, 8, D_MODEL), const3),
            pl.BlockSpec((1, LANES), const2),
            pl.BlockSpec((8, 1), const2),
        ],
        out_specs=[
            pl.BlockSpec((tm, QK_A), row),
            pl.BlockSpec((tm, QK_A), row),
            pl.BlockSpec((tm, D_MODEL), row),
            pl.BlockSpec((tm, D_MODEL), row),
            pl.BlockSpec((tm, LANES), row),
            pl.BlockSpec((8, tm), lambda i: (0, i)),
        ],
        out_shape=[
            jax.ShapeDtypeStruct((t, QK_A), BF16),
            jax.ShapeDtypeStruct((t, QK_A), BF16),
            jax.ShapeDtypeStruct((t, D_MODEL), BF16),
            jax.ShapeDtypeStruct((t, D_MODEL), F32),
            jax.ShapeDtypeStruct((t, LANES), F32),
            jax.ShapeDtypeStruct((8, t), F32),
        ],
        compiler_params=_params(("parallel",)),
        name="proj_a",
    )(x2d, g, wq, wk, wv, wo, wgc, wgr, bgc, bgr)


def _mlstm_kernel(q_ref, k_ref, v_ref, gc_ref, gr_ref, h_ref, c_scr, n_scr, m_scr):
    @pl.when(pl.program_id(1) == 0)
    def _():
        c_scr[...] = jnp.zeros_like(c_scr)
        n_scr[...] = jnp.zeros_like(n_scr)
        m_scr[...] = jnp.zeros_like(m_scr)

    ti = lax.broadcasted_iota(jnp.int32, (CHUNK, CHUNK), 0)
    si = lax.broadcasted_iota(jnp.int32, (CHUNK, CHUNK), 1)
    causal = si <= ti
    gc = gc_ref[...]
    gr = gr_ref[...]
    for h in range(N_HEADS_A):
        li_c = gc[:, h:h + 1]
        b_c = gc[:, N_HEADS_A + h:N_HEADS_A + h + 1]
        li_r = gr[h:h + 1, :]
        b_r = gr[N_HEADS_A + h:N_HEADS_A + h + 1, :]
        a_r = li_r - b_r
        a_c = li_c - b_c
        b_end = b_c[CHUNK - 1:CHUNK, :]
        m_run = m_scr[h][:, 0:1]

        log_inter = b_c + m_run
        log_d = b_c + a_r
        m_intra = jnp.max(jnp.where(causal, log_d, -jnp.inf), axis=-1, keepdims=True)
        m_t = jnp.maximum(log_inter, m_intra)
        d_mat = jnp.where(causal, jnp.exp(log_d - m_t), 0.0)
        w_inter = jnp.exp(log_inter - m_t)

        qh = q_ref[:, h * DQK_A:(h + 1) * DQK_A]
        kh = k_ref[:, h * DQK_A:(h + 1) * DQK_A]
        vh = v_ref[:, h * DV_A:(h + 1) * DV_A]
        c_mat = c_scr[h]
        n_vec = n_scr[h]
        s = _dot_nt(qh, kh) * d_mat
        num = _dot(s.astype(BF16), vh) + w_inter * _dot(qh, c_mat.astype(BF16))
        qn = jnp.sum(qh.astype(F32) * n_vec, axis=-1, keepdims=True)
        den = jnp.sum(s, axis=-1, keepdims=True) + w_inter * qn
        hh = num / jnp.maximum(jnp.abs(den), jnp.exp(-m_t))
        h_ref[:, h * DV_A:(h + 1) * DV_A] = hh * _rms_scale(hh)

        log_w_r = b_end + a_r
        m_new = jnp.maximum(b_end + m_run, jnp.max(log_w_r, axis=-1, keepdims=True))
        w_c = jnp.exp(b_end + a_c - m_new)
        decay = jnp.exp(b_end + m_run - m_new)
        kw = kh.astype(F32) * w_c
        c_scr[h] = decay * c_mat + _dot_tn(kw.astype(BF16), vh)
        n_scr[h] = decay * n_vec + jnp.sum(kw, axis=0, keepdims=True)
        m_scr[h] = jnp.broadcast_to(m_new, (1, LANES))


def _mlstm(q, k, v, gc, gr, bsz, seq):
    nc = seq // CHUNK
    row = lambda b, c: (b * nc + c, 0)
    return pl.pallas_call(
        _mlstm_kernel,
        grid=(bsz, nc),
        in_specs=[
            pl.BlockSpec((CHUNK, QK_A), row),
            pl.BlockSpec((CHUNK, QK_A), row),
            pl.BlockSpec((CHUNK, D_MODEL), row),
            pl.BlockSpec((CHUNK, LANES), row),
            pl.BlockSpec((8, CHUNK), lambda b, c: (0, b * nc + c)),
        ],
        out_specs=pl.BlockSpec((CHUNK, D_MODEL), row),
        out_shape=jax.ShapeDtypeStruct((bsz * seq, D_MODEL), F32),
        scratch_shapes=[
            pltpu.VMEM((N_HEADS_A, DQK_A, DV_A), F32),
            pltpu.VMEM((N_HEADS_A, 1, DQK_A), F32),
            pltpu.VMEM((N_HEADS_A, 1, LANES), F32),
        ],
        compiler_params=_params(("arbitrary", "arbitrary")),
        name="mlstm",
    )(q, k, v, gc, gr)


def _route(logits):
    lane = lax.broadcasted_iota(jnp.int32, logits.shape, 1)
    big = jnp.int32(10 ** 6)

    def top(mask):
        mx = jnp.max(jnp.where(mask, logits, -jnp.inf), axis=-1, keepdims=True)
        idx = jnp.min(jnp.where(mask & (logits == mx), lane, big), axis=-1, keepdims=True)
        return mx, idx

    gmask = lane < N_GROUPS
    gmax, gidx = top(gmask)
    g_p = 1.0 / jnp.sum(jnp.where(gmask, jnp.exp(logits - gmax), 0.0), axis=-1, keepdims=True)
    lo = ROUTE_LANE0 + EXP_PER_GROUP * gidx
    emask = (lane >= lo) & (lane < lo + EXP_PER_GROUP)
    e1, i1 = top(emask)
    e2, i2 = top(emask & (lane != i1))
    esum = jnp.sum(jnp.where(emask, jnp.exp(logits - e1), 0.0), axis=-1, keepdims=True)
    p1 = 1.0 / esum
    p2 = jnp.exp(e2 - e1) / esum
    w1 = g_p * (p1 / (p1 + p2))
    w2 = g_p * (p2 / (p1 + p2))
    return jnp.where(lane == i1, w1, 0.0) + jnp.where(lane == i2, w2, 0.0)


def _ffn_norm_and_route(h, gn_ref, wr_ref, br_ref, xn_ref, comb_ref):
    xn = h * _rms_scale(h) * gn_ref[...]
    xh, xl = _split2(xn)
    xn_ref[...] = xh
    comb_ref[...] = _route(_dot_x3(xh, xl, wr_ref) + br_ref[...])


def _out_a_kernel(hn_ref, o_ref, x_ref, hg_ref, wout_ref, gn_ref, wr_ref, br_ref,
                  h1_ref, xn_ref, comb_ref):
    z = (hn_ref[...] * hg_ref[...] * _sigmoid(o_ref[...])).astype(BF16)
    h1 = x_ref[...] + _dot(z, wout_ref[...])
    h1_ref[...] = h1
    _ffn_norm_and_route(h1, gn_ref, wr_ref, br_ref, xn_ref, comb_ref)


def _out_a(hn, o, x2d, hg, wout, gn, wr, br, tm):
    t = x2d.shape[0]
    row = lambda i: (i, 0)
    const2 = lambda i: (0, 0)
    const3 = lambda i: (0, 0, 0)
    return pl.pallas_call(
        _out_a_kernel,
        grid=(t // tm,),
        in_specs=[
            pl.BlockSpec((tm, D_MODEL), row),
            pl.BlockSpec((tm, D_MODEL), row),
            pl.BlockSpec((tm, D_MODEL), row),
            pl.BlockSpec((1, D_MODEL), const2),
            pl.BlockSpec((D_MODEL, D_MODEL), const2),
            pl.BlockSpec((1, D_MODEL), const2),
            pl.BlockSpec((2, D_MODEL, LANES), const3),
            pl.BlockSpec((1, LANES), const2),
        ],
        out_specs=[
            pl.BlockSpec((tm, D_MODEL), row),
            pl.BlockSpec((tm, D_MODEL), row),
            pl.BlockSpec((tm, LANES), row),
        ],
        out_shape=[
            jax.ShapeDtypeStruct((t, D_MODEL), F32),
            jax.ShapeDtypeStruct((t, D_MODEL), BF16),
            jax.ShapeDtypeStruct((t, LANES), F32),
        ],
        compiler_params=_params(("parallel",)),
        name="out_a",
    )(hn, o, x2d, hg, wout, gn, wr, br)


def _moe_kernel(xn_ref, h_ref, comb_ref, wg_ref, wu_ref, wd_ref, gf_ref, out_ref, acc_ref, *, final_norm):
    e = pl.program_id(1)

    @pl.when(e == 0)
    def _():
        acc_ref[...] = h_ref[...]

    x = xn_ref[...]
    comb = comb_ref[...]
    lane = lax.broadcasted_iota(jnp.int32, comb.shape, 1)
    w_e = jnp.sum(jnp.where(lane == ROUTE_LANE0 + e, comb, 0.0), axis=-1, keepdims=True)
    g = _dot(x, wg_ref[0])
    u = _dot(x, wu_ref[0])
    hdn = (g * _sigmoid(g) * u * w_e).astype(BF16)
    acc_ref[...] += _dot(hdn, wd_ref[0])

    @pl.when(e == N_EXPERTS - 1)
    def _():
        y = acc_ref[...]
        if final_norm:
            y = y * _rms_scale(y) * gf_ref[...]
        out_ref[...] = y


def _moe(xn, h, comb, wg, wu, wd, gf, tm, final_norm):
    t = xn.shape[0]
    row = lambda i, e: (i, 0)
    wsel = lambda i, e: (e, 0, 0)
    return pl.pallas_call(
        functools.partial(_moe_kernel, final_norm=final_norm),
        grid=(t // tm, N_EXPERTS),
        in_specs=[
            pl.BlockSpec((tm, D_MODEL), row),
            pl.BlockSpec((tm, D_MODEL), row),
            pl.BlockSpec((tm, LANES), row),
            pl.BlockSpec((1, D_MODEL, D_EXPERT), wsel),
            pl.BlockSpec((1, D_MODEL, D_EXPERT), wsel),
            pl.BlockSpec((1, D_EXPERT, D_MODEL), wsel),
            pl.BlockSpec((1, D_MODEL), lambda i, e: (0, 0)),
        ],
        out_specs=pl.BlockSpec((tm, D_MODEL), row),
        out_shape=jax.ShapeDtypeStruct((t, D_MODEL), F32),
        scratch_shapes=[pltpu.VMEM((tm, D_MODEL), F32)],
        compiler_params=_params(("parallel", "arbitrary")),
        name="moe_final" if final_norm else "moe",
    )(xn, h, comb, wg, wu, wd, gf)


def _head_rms(x, gsum_ref, gexp_ref):
    s_hi, s_lo = _split2(x * x)
    ms = _dot(s_hi, gsum_ref[...]) + _dot(s_lo, gsum_ref[...])
    r_hi, r_lo = _split2(lax.rsqrt(ms + EPS))
    return _dot(r_hi, gexp_ref[...]) + _dot(r_lo, gexp_ref[...])


def _proj_kvq_kernel(h_ref, gkv_ref, gq_ref, wk_ref, wvt_ref, wf_ref, bf_ref, wq_ref, wogt_ref,
                     kgain_ref, qgain_ref, gsum_ref, gexp_ref, sel_ref,
                     kaug_ref, vt_ref, qaug_ref, sgt_ref, cstat_ref, carry_ref, *, tm):
    @pl.when(pl.program_id(1) == 0)
    def _():
        carry_ref[...] = jnp.zeros_like(carry_ref)

    hres = h_ref[...]
    y = hres * _rms_scale(hres)
    a = y * gkv_ref[...]
    ah, al = _split2(a)
    bh = (y * gq_ref[...]).astype(BF16)

    logf = _log_sigmoid(_dot_x3(ah, al, wf_ref) + bf_ref[...])
    ti = lax.broadcasted_iota(jnp.int32, (tm, tm), 0)
    tj = lax.broadcasted_iota(jnp.int32, (tm, tm), 1)
    tril = jnp.where(tj <= ti, 1.0, 0.0).astype(BF16)
    c = _dot_exact_lhs(tril, logf) + carry_ref[...]
    carry_ref[...] = c[tm - 1:tm, :]

    nb = c * (-LOG2E)
    for blk in range(tm // ATT_BLK):
        nb_blk = nb[blk * ATT_BLK:(blk + 1) * ATT_BLK, :]
        cstat_ref[0, blk, 0:1, :] = jnp.max(nb_blk, axis=0, keepdims=True)
        cstat_ref[0, blk, 1:2, :] = jnp.min(nb_blk, axis=0, keepdims=True)
    n0, n1, n2 = _split3(nb)
    lane = lax.broadcasted_iota(jnp.int32, (tm, LANES), 1)
    packed = jnp.where(lane < N_HEADS_B, n0.astype(F32),
                       jnp.where(lane < 2 * N_HEADS_B, pltpu.roll(n1.astype(F32), N_HEADS_B, 1),
                                 pltpu.roll(n2.astype(F32), 2 * N_HEADS_B, 1)))
    packed = jnp.where(lane < 3 * N_HEADS_B, packed, 0.0).astype(BF16)
    extras = _dot(packed, sel_ref[...])

    k = _dot(ah, wk_ref[...])
    kn = k * _head_rms(k, gsum_ref, gexp_ref) * kgain_ref[...]
    q = _dot(bh, wq_ref[...])
    qn = q * _head_rms(q, gsum_ref, gexp_ref) * qgain_ref[...]
    ones = jnp.where((lane >= BIAS_LANE) & (lane < BIAS_LANE + 3), 1.0, 0.0)
    for j in range(N_HEADS_B // 2):
        kj = kn[:, j * LANES:(j + 1) * LANES]
        qj = qn[:, j * LANES:(j + 1) * LANES]
        for half, (kk, qq) in enumerate(((kj, qj), (pltpu.roll(kj, DH_B, 1), pltpu.roll(qj, DH_B, 1)))):
            hd = 2 * j + half
            ex = extras[:, hd * LANES:(hd + 1) * LANES]
            kaug_ref[0, hd] = jnp.where(lane < DH_B, kk, ex).astype(BF16)
            qaug_ref[0, hd] = jnp.where(lane < DH_B, qq, ones).astype(BF16)

    vt = _dot_nt(wvt_ref[...], ah)
    sgt = _sigmoid(_dot_nt(wogt_ref[...], bh))
    for hd in range(N_HEADS_B):
        vt_ref[0, hd] = vt[hd * DH_B:(hd + 1) * DH_B, :].astype(BF16)
    sgt_ref[0] = sgt


def _proj_kvq(h2, gkv, gq, wk, wvt, wf, bf, wq, wogt, kgain, qgain, gsum, gexp, sel, bsz, seq, tm):
    ns = seq // tm
    row = lambda b, s: (b * ns + s, 0)
    const2 = lambda b, s: (0, 0)
    const3 = lambda b, s: (0, 0, 0)
    return pl.pallas_call(
        functools.partial(_proj_kvq_kernel, tm=tm),
        grid=(bsz, ns),
        in_specs=[
            pl.BlockSpec((tm, D_MODEL), row),
            pl.BlockSpec((1, D_MODEL), const2),
            pl.BlockSpec((1, D_MODEL), const2),
            pl.BlockSpec((D_MODEL, D_MODEL), const2),
            pl.BlockSpec((D_MODEL, D_MODEL), const2),
            pl.BlockSpec((2, D_MODEL, LANES), const3),
            pl.BlockSpec((1, LANES), const2),
            pl.BlockSpec((D_MODEL, D_MODEL), const2),
            pl.BlockSpec((D_MODEL, D_MODEL), const2),
            pl.BlockSpec((1, D_MODEL), const2),
            pl.BlockSpec((1, D_MODEL), const2),
            pl.BlockSpec((D_MODEL, LANES), const2),
            pl.BlockSpec((LANES, D_MODEL), const2),
            pl.BlockSpec((LANES, N_HEADS_B * LANES), const2),
        ],
        out_specs=[
            pl.BlockSpec((1, N_HEADS_B, tm, LANES), lambda b, s: (b, 0, s, 0)),
            pl.BlockSpec((1, N_HEADS_B, DH_B, tm), lambda b, s: (b, 0, 0, s)),
            pl.BlockSpec((1, N_HEADS_B, tm, LANES), lambda b, s: (b, 0, s, 0)),
            pl.BlockSpec((1, D_MODEL, tm), lambda b, s: (b, 0, s)),
            pl.BlockSpec((1, tm // ATT_BLK, 2, LANES), lambda b, s: (b, s, 0, 0)),
        ],
        out_shape=[
            jax.ShapeDtypeStruct((bsz, N_HEADS_B, seq, LANES), BF16),
            jax.ShapeDtypeStruct((bsz, N_HEADS_B, DH_B, seq), BF16),
            jax.ShapeDtypeStruct((bsz, N_HEADS_B, seq, LANES), BF16),
            jax.ShapeDtypeStruct((bsz, D_MODEL, seq), F32),
            jax.ShapeDtypeStruct((bsz, seq // ATT_BLK, 2, LANES), F32),
        ],
        scratch_shapes=[pltpu.VMEM((1, LANES), F32)],
        compiler_params=_params(("arbitrary", "arbitrary")),
        name="proj_kvq",
    )(h2, gkv, gq, wk, wvt, wf, bf, wq, wogt, kgain, qgain, gsum, gexp, sel)


def _attn_kernel(trips_ref, q_ref, k_ref, vt_ref, o_ref, m_scr, l_scr, acc_scr, sa_scr, sb_scr, xa_scr, xb_scr,
                 p_scr, *, chains):
    b, h, i = pl.program_id(0), pl.program_id(1), pl.program_id(2)
    trips = trips_ref[(b * pl.num_programs(1) + h) * pl.num_programs(2) + i]
    blk = ATT_BLK
    lane = lax.broadcasted_iota(jnp.int32, (blk, LANES), 1)
    k_null = jnp.where(lane == BIAS_LANE, NEG_BIG, 0.0).astype(BF16)

    def key_offset(g, t):
        kb = i * chains + g - t
        return kb, pl.multiple_of(jnp.maximum(kb, 0) * blk, blk)

    def scores(t, s_ref, x_ref, first):
        for g in range(chains):
            q = q_ref[0, 0, g * blk:(g + 1) * blk, :]
            kb, off = key_offset(g, t)
            kblk = k_ref[0, 0, pl.ds(off, blk), :]
            if not first:
                kblk = jnp.where(kb >= 0, kblk, k_null)
            s = _dot_nt(kblk, q)
            if first:
                ki = lax.broadcasted_iota(jnp.int32, (blk, blk), 0)
                qj = lax.broadcasted_iota(jnp.int32, (blk, blk), 1)
                s = jnp.where(ki <= qj, s, NEG_BIG)
            s_ref[g] = s
            x_ref[g] = jnp.max(s, axis=0, keepdims=True)

    def update(t, s_ref, x_ref):
        for g in range(chains):
            _, off = key_offset(g, t)
            vblk = vt_ref[0, 0, :, pl.ds(off, blk)]
            m_old = m_scr[g]
            m_new = jnp.maximum(m_old, x_ref[g])
            p = jnp.exp2(s_ref[g] - m_new)
            alpha = jnp.exp2(m_old - m_new)
            l_scr[g] = alpha * l_scr[g] + jnp.sum(p, axis=0, keepdims=True)
            p_scr[g] = p.astype(BF16)
            acc_scr[g] = alpha * acc_scr[g] + _dot(vblk, p_scr[g])
            m_scr[g] = m_new

    m_scr[...] = jnp.full(m_scr.shape, NEG_BIG, F32)
    l_scr[...] = jnp.zeros(l_scr.shape, F32)
    acc_scr[...] = jnp.zeros(acc_scr.shape, F32)
    scores(0, sa_scr, xa_scr, True)
    steps = trips - 1

    def body(pair, carry):
        t = 2 * pair
        scores(t + 1, sb_scr, xb_scr, False)
        update(t, sa_scr, xa_scr)
        scores(t + 2, sa_scr, xa_scr, False)
        update(t + 1, sb_scr, xb_scr)
        return carry

    lax.fori_loop(0, steps >> 1, body, 0)

    @pl.when((steps & 1) == 1)
    def _():
        scores(steps, sb_scr, xb_scr, False)
        update(steps - 1, sa_scr, xa_scr)
        update(steps, sb_scr, xb_scr)

    @pl.when((steps & 1) == 0)
    def _():
        update(steps, sa_scr, xa_scr)

    for g in range(chains):
        o_ref[0, 0, :, g * blk:(g + 1) * blk] = acc_scr[g] / l_scr[g]


def _attn_trips(cstat, kgain, qgain, chains):
    nbmax = jnp.swapaxes(cstat[:, :, 0, :N_HEADS_B], 1, 2)
    nbmin = jnp.swapaxes(cstat[:, :, 1, :N_HEADS_B], 1, 2)
    nblk = nbmax.shape[-1]
    xb = DH_B * jnp.max(jnp.abs(kgain)) * jnp.max(jnp.abs(qgain))
    slack = 2.0 * xb * 1.02 + 2.0 + SKIP_LOG2
    j = jnp.arange(nblk)
    keep = (nbmax[:, :, None, :] - nbmin[:, :, :, None] + slack >= 0.0) & (j[None, :] <= j[:, None])
    jmin = jnp.min(jnp.where(keep, j[None, :], nblk), axis=-1)
    need = j - jnp.minimum(jmin, j) + 1
    return jnp.max(need.reshape(need.shape[0], need.shape[1], nblk // chains, chains), axis=-1).astype(jnp.int32)


def _attn(qaug, kaug, vt, trips, chains):
    bsz, nh, seq, _ = qaug.shape
    tq = chains * ATT_BLK
    grid_spec = pltpu.PrefetchScalarGridSpec(
        num_scalar_prefetch=1,
        grid=(bsz, nh, seq // tq),
        in_specs=[
            pl.BlockSpec((1, 1, tq, LANES), lambda b, h, i, tr: (b, h, i, 0)),
            pl.BlockSpec((1, 1, seq, LANES), lambda b, h, i, tr: (b, h, 0, 0)),
            pl.BlockSpec((1, 1, DH_B, seq), lambda b, h, i, tr: (b, h, 0, 0)),
        ],
        out_specs=pl.BlockSpec((1, 1, DH_B, tq), lambda b, h, i, tr: (b, h, 0, i)),
        scratch_shapes=[
            pltpu.VMEM((chains, 1, ATT_BLK), F32),
            pltpu.VMEM((chains, 1, ATT_BLK), F32),
            pltpu.VMEM((chains, DH_B, ATT_BLK), F32),
            pltpu.VMEM((chains, ATT_BLK, ATT_BLK), F32),
            pltpu.VMEM((chains, ATT_BLK, ATT_BLK), F32),
            pltpu.VMEM((chains, 1, ATT_BLK), F32),
            pltpu.VMEM((chains, 1, ATT_BLK), F32),
            pltpu.VMEM((chains, ATT_BLK, ATT_BLK), BF16),
        ],
    )
    return pl.pallas_call(
        functools.partial(_attn_kernel, chains=chains),
        grid_spec=grid_spec,
        out_shape=jax.ShapeDtypeStruct((bsz, nh, DH_B, seq), F32),
        compiler_params=_params(("parallel", "parallel", "arbitrary")),
        name="attn",
    )(trips.reshape(-1), qaug, kaug, vt)


def _out_b_kernel(ot_ref, sgt_ref, h_ref, wout_ref, gn_ref, wr_ref, br_ref, h3_ref, xn_ref, comb_ref):
    zt = (ot_ref[0] * sgt_ref[0]).astype(BF16)
    h3 = h_ref[...] + _dot_tn(zt, wout_ref[...])
    h3_ref[...] = h3
    _ffn_norm_and_route(h3, gn_ref, wr_ref, br_ref, xn_ref, comb_ref)


def _out_b(ot, sgt, h2, wout, gn, wr, br, bsz, seq, tm):
    ns = seq // tm
    row = lambda b, s: (b * ns + s, 0)
    const2 = lambda b, s: (0, 0)
    const3 = lambda b, s: (0, 0, 0)
    t = bsz * seq
    return pl.pallas_call(
        _out_b_kernel,
        grid=(bsz, ns),
        in_specs=[
            pl.BlockSpec((1, D_MODEL, tm), lambda b, s: (b, 0, s)),
            pl.BlockSpec((1, D_MODEL, tm), lambda b, s: (b, 0, s)),
            pl.BlockSpec((tm, D_MODEL), row),
            pl.BlockSpec((D_MODEL, D_MODEL), const2),
            pl.BlockSpec((1, D_MODEL), const2),
            pl.BlockSpec((2, D_MODEL, LANES), const3),
            pl.BlockSpec((1, LANES), const2),
        ],
        out_specs=[
            pl.BlockSpec((tm, D_MODEL), row),
            pl.BlockSpec((tm, D_MODEL), row),
            pl.BlockSpec((tm, LANES), row),
        ],
        out_shape=[
            jax.ShapeDtypeStruct((t, D_MODEL), F32),
            jax.ShapeDtypeStruct((t, D_MODEL), BF16),
            jax.ShapeDtypeStruct((t, LANES), F32),
        ],
        compiler_params=_params(("parallel", "parallel")),
        name="out_b",
    )(ot, sgt, h2, wout, gn, wr, br)


def _hi_lo(w):
    hi = w.astype(BF16)
    lo = (w - hi.astype(F32)).astype(BF16)
    return jnp.stack([hi, lo])


def _pad_lanes(w, width=LANES):
    return jnp.pad(w, ((0, 0),) * (w.ndim - 1) + ((0, width - w.shape[-1]),))


def _router_params(w_grp, b_grp, w_exp, b_exp):
    w = _pad_lanes(jnp.concatenate([w_grp, w_exp], axis=-1))
    b = _pad_lanes(jnp.concatenate([b_grp, b_exp], axis=-1)[None, :])
    return _hi_lo(w), b


def _tile(seq, pref):
    t = pref
    while seq % t:
        t //= 2
    return t


def kernel(x, norm_mix, norm_ffn, a_w_in, a_b_gate, a_head_gain, a_w_out, kv_norm, kv_w, kv_b_f, kv_k_gain,
           b_w_in, b_q_gain, b_w_out, moe_w_grp, moe_b_grp, moe_w_exp, moe_b_exp, moe_w_gate, moe_w_up,
           moe_w_down, norm_final):
    bsz, seq, _ = x.shape
    t = bsz * seq
    assert seq % CHUNK == 0 and seq % ATT_BLK == 0
    tm = _tile(seq, 512)
    assert tm % ATT_BLK == 0
    x2d = x.reshape(t, D_MODEL)

    w_in = a_w_in[0]
    wq = w_in[:, :QK_A].astype(BF16)
    wk = w_in[:, QK_A:2 * QK_A].astype(BF16)
    wv = w_in[:, 2 * QK_A:2 * QK_A + D_MODEL].astype(BF16)
    wo = w_in[:, 2 * QK_A + D_MODEL:2 * QK_A + 2 * D_MODEL].astype(BF16)
    wgate = w_in[:, 2 * QK_A + 2 * D_MODEL:]
    wgc = _hi_lo(_pad_lanes(wgate))
    wgr = _hi_lo(wgate.T)
    bgc = _pad_lanes(a_b_gate[0][None, :])
    bgr = a_b_gate[0][:, None]
    q, k, v, o, gc, gr = _proj_a(x2d, norm_mix[0][None, :], wq, wk, wv, wo, wgc, wgr, bgc, bgr, tm)
    hn = _mlstm(q, k, v, gc, gr, bsz, seq)
    wr0, br0 = _router_params(moe_w_grp[0], moe_b_grp[0], moe_w_exp[0], moe_b_exp[0])
    h1, xn1, comb1 = _out_a(hn, o, x2d, a_head_gain[0][None, :], a_w_out[0].astype(BF16),
                            norm_ffn[0][None, :], wr0, br0, tm)
    gf = norm_final[None, :]
    h2 = _moe(xn1, h1, comb1, moe_w_gate[0].astype(BF16), moe_w_up[0].astype(BF16),
              moe_w_down[0].astype(BF16), gf, tm, final_norm=False)

    wkk = kv_w[:, :D_MODEL].astype(BF16)
    wvt = kv_w[:, D_MODEL:2 * D_MODEL].T.astype(BF16)
    wf = _hi_lo(_pad_lanes(kv_w[:, 2 * D_MODEL:]))
    bf = _pad_lanes(kv_b_f[None, :])
    wq1 = b_w_in[0][:, :D_MODEL].astype(BF16)
    wogt = b_w_in[0][:, D_MODEL:].T.astype(BF16)
    kgain = jnp.tile(kv_k_gain, N_HEADS_B)[None, :]
    qgain = jnp.tile(b_q_gain[0], N_HEADS_B)[None, :] * (DH_B ** -0.5 * LOG2E)
    head_of = jnp.arange(D_MODEL) // DH_B
    gsum = (head_of[:, None] == jnp.arange(LANES)[None, :]).astype(BF16) * (1.0 / DH_B)
    gexp = (jnp.arange(LANES)[:, None] == head_of[None, :]).astype(BF16)
    src = jnp.arange(LANES)[:, None]
    dst = jnp.arange(N_HEADS_B * LANES)[None, :]
    sel = ((src < 3 * N_HEADS_B) & (dst == (src % N_HEADS_B) * LANES + BIAS_LANE + src // N_HEADS_B)).astype(BF16)
    kaug, vt, qaug, sgt, cstat = _proj_kvq(h2, kv_norm[None, :], norm_mix[1][None, :], wkk, wvt, wf, bf, wq1,
                                           wogt, kgain, qgain, gsum, gexp, sel, bsz, seq, tm)
    chains = math.gcd(ATT_CHAINS, seq // ATT_BLK)
    ot = _attn(qaug, kaug, vt, _attn_trips(cstat, kgain, qgain, chains), chains)
    wr1, br1 = _router_params(moe_w_grp[1], moe_b_grp[1], moe_w_exp[1], moe_b_exp[1])
    h3, xn3, comb3 = _out_b(ot.reshape(bsz, D_MODEL, seq), sgt, h2, b_w_out[0].astype(BF16),
                            norm_ffn[1][None, :], wr1, br1, bsz, seq, tm)
    out = _moe(xn3, h3, comb3, moe_w_gate[1].astype(BF16), moe_w_up[1].astype(BF16),
               moe_w_down[1].astype(BF16), gf, tm, final_norm=True)
    return out.reshape(bsz, seq, D_MODEL)
```

```python
import functools
import math

import jax
import jax.numpy as jnp
from jax import lax
from jax.experimental import pallas as pl
from jax.experimental.pallas import tpu as pltpu

F32 = jnp.float32
BF16 = jnp.bfloat16

D_MODEL = 1024
EPS = 1e-6

N_HEADS_A = 4
DV_A = D_MODEL // N_HEADS_A
DQK_A = DV_A // 2
QK_A = N_HEADS_A * DQK_A
GATE_CAP = 15.0
CHUNK = 128
CHUNK_SHIFT = CHUNK.bit_length() - 1

DH_B = 64
N_HEADS_B = D_MODEL // DH_B
LOG2E = 1.4426950408889634
BIAS_LANE = DH_B
NEG_BIG = -1e30
ATT_BLK = 256
ATT_CHAINS = 8
SKIP_LOG2 = 160.0

N_GROUPS = 4
EXP_PER_GROUP = 4
N_EXPERTS = N_GROUPS * EXP_PER_GROUP
D_EXPERT = D_MODEL // 2
ROUTE_LANE0 = N_GROUPS
PAIR_LO = (0, 0, 0, 1, 1, 2)
PAIR_HI = (1, 2, 3, 2, 3, 3)
N_BINS = N_GROUPS * len(PAIR_LO)
MOE_TILE = 256
ROW_SUB = 8
ROWS_PER_STEP = 1024

LANES = 128
VMEM_LIMIT = 56 * 1024 * 1024


def _params(sem, vmem=VMEM_LIMIT):
    return pltpu.CompilerParams(dimension_semantics=sem, vmem_limit_bytes=vmem)


def _dot(a, b):
    return jnp.dot(a, b, preferred_element_type=F32)


def _dot_nt(a, b):
    return lax.dot_general(a, b, (((1,), (1,)), ((), ())), preferred_element_type=F32)


def _dot_tn(a, b):
    return lax.dot_general(a, b, (((0,), (0,)), ((), ())), preferred_element_type=F32)


def _split2(x):
    hi = x.astype(BF16)
    lo = (x - hi.astype(F32)).astype(BF16)
    return hi, lo


def _split3(x):
    hi = x.astype(BF16)
    r = x - hi.astype(F32)
    mid = r.astype(BF16)
    lo = (r - mid.astype(F32)).astype(BF16)
    return hi, mid, lo


def _dot_x3(x_hi, x_lo, w_ref):
    return _dot(x_hi, w_ref[0]) + _dot(x_lo, w_ref[0]) + _dot(x_hi, w_ref[1])


def _dot_nt_x3(w_ref, x_hi, x_lo):
    return _dot_nt(w_ref[0], x_hi) + _dot_nt(w_ref[0], x_lo) + _dot_nt(w_ref[1], x_hi)


def _dot_exact_rhs(a, b_exact):
    a0, a1, a2 = _split3(a)
    return _dot(a0, b_exact) + _dot(a1, b_exact) + _dot(a2, b_exact)


def _dot_exact_lhs(a_exact, b):
    b0, b1, b2 = _split3(b)
    return _dot(a_exact, b0) + _dot(a_exact, b1) + _dot(a_exact, b2)


def _rms_scale(x):
    return lax.rsqrt(jnp.mean(x * x, axis=-1, keepdims=True) + EPS)


def _log_sigmoid(z):
    return jnp.minimum(z, 0.0) - jnp.log(1.0 + jnp.exp(-jnp.abs(z)))


def _sigmoid(z):
    return 1.0 / (1.0 + jnp.exp(-z))


def _softcap(z):
    return GATE_CAP * jnp.tanh(z / GATE_CAP)


def _proj_a_kernel(x_ref, g_ref, wq_ref, wk_ref, wv_ref, wo_ref, wgc_ref, wgr_ref, bgc_ref, bgr_ref,
                   q_ref, k_ref, v_ref, o_ref, gc_ref, gr_ref, *, tm):
    x = x_ref[...]
    xn = x * _rms_scale(x) * g_ref[...]
    xh, xl = _split2(xn)
    q_ref[...] = _dot(xh, wq_ref[...]).astype(BF16)
    k_ref[...] = (_dot(xh, wk_ref[...]) * (DQK_A ** -0.5)).astype(BF16)
    v_ref[...] = _dot(xh, wv_ref[...]).astype(BF16)
    o_ref[...] = _dot(xh, wo_ref[...])

    zc = _softcap(_dot_x3(xh, xl, wgc_ref) + bgc_ref[...])
    zr = _softcap(_dot_nt_x3(wgr_ref, xh, xl) + bgr_ref[...])
    lane = lax.broadcasted_iota(jnp.int32, zc.shape, 1)
    sub = lax.broadcasted_iota(jnp.int32, zr.shape, 0)
    vc = jnp.where(lane < N_HEADS_A, zc, _log_sigmoid(zc))
    vr = jnp.where(sub < N_HEADS_A, zr, _log_sigmoid(zr))
    ti = lax.broadcasted_iota(jnp.int32, (tm, tm), 0)
    tj = lax.broadcasted_iota(jnp.int32, (tm, tm), 1)
    same = (ti >> CHUNK_SHIFT) == (tj >> CHUNK_SHIFT)
    tril = jnp.where(same & (tj <= ti), 1.0, 0.0).astype(BF16)
    triu = jnp.where(same & (ti <= tj), 1.0, 0.0).astype(BF16)
    cc = _dot_exact_lhs(tril, vc)
    cr = _dot_exact_rhs(vr, triu)
    gc_ref[...] = jnp.where(lane < N_HEADS_A, vc, cc)
    gr_ref[...] = jnp.where(sub < N_HEADS_A, vr, cr)


def _proj_a(x2d, g, wq, wk, wv, wo, wgc, wgr, bgc, bgr, tm):
    t = x2d.shape[0]
    row = lambda i: (i, 0)
    const2 = lambda i: (0, 0)
    const3 = lambda i: (0, 0, 0)
    return pl.pallas_call(
        functools.partial(_proj_a_kernel, tm=tm),
        grid=(t // tm,),
        in_specs=[
            pl.BlockSpec((tm, D_MODEL), row),
            pl.BlockSpec((1, D_MODEL), const2),
            pl.BlockSpec((D_MODEL, QK_A), const2),
            pl.BlockSpec((D_MODEL, QK_A), const2),
            pl.BlockSpec((D_MODEL, D_MODEL), const2),
            pl.BlockSpec((D_MODEL, D_MODEL), const2),
            pl.BlockSpec((2, D_MODEL, LANES), const3),
            pl.BlockSpec((2, 8, D_MODEL), const3),
            pl.BlockSpec((1, LANES), const2),
            pl.BlockSpec((8, 1), const2),
        ],
        out_specs=[
            pl.BlockSpec((tm, QK_A), row),
            pl.BlockSpec((tm, QK_A), row),
            pl.BlockSpec((tm, D_MODEL), row),
            pl.BlockSpec((tm, D_MODEL), row),
            pl.BlockSpec((tm, LANES), row),
            pl.BlockSpec((8, tm), lambda i: (0, i)),
        ],
        out_shape=[
            jax.ShapeDtypeStruct((t, QK_A), BF16),
            jax.ShapeDtypeStruct((t, QK_A), BF16),
            jax.ShapeDtypeStruct((t, D_MODEL), BF16),
            jax.ShapeDtypeStruct((t, D_MODEL), F32),
            jax.ShapeDtypeStruct((t, LANES), F32),
            jax.ShapeDtypeStruct((8, t), F32),
        ],
        compiler_params=_params(("parallel",)),
        name="proj_a",
    )(x2d, g, wq, wk, wv, wo, wgc, wgr, bgc, bgr)


def _mlstm_kernel(q_ref, k_ref, v_ref, gc_ref, gr_ref, h_ref, c_scr, n_scr, m_scr):
    @pl.when(pl.program_id(1) == 0)
    def _():
        c_scr[...] = jnp.zeros_like(c_scr)
        n_scr[...] = jnp.zeros_like(n_scr)
        m_scr[...] = jnp.zeros_like(m_scr)

    ti = lax.broadcasted_iota(jnp.int32, (CHUNK, CHUNK), 0)
    si = lax.broadcasted_iota(jnp.int32, (CHUNK, CHUNK), 1)
    causal = si <= ti
    gc = gc_ref[...]
    gr = gr_ref[...]
    for h in range(N_HEADS_A):
        li_c = gc[:, h:h + 1]
        b_c = gc[:, N_HEADS_A + h:N_HEADS_A + h + 1]
        li_r = gr[h:h + 1, :]
        b_r = gr[N_HEADS_A + h:N_HEADS_A + h + 1, :]
        a_r = li_r - b_r
        a_c = li_c - b_c
        b_end = b_c[CHUNK - 1:CHUNK, :]
        m_run = m_scr[h][:, 0:1]

        log_inter = b_c + m_run
        log_d = b_c + a_r
        m_intra = jnp.max(jnp.where(causal, log_d, -jnp.inf), axis=-1, keepdims=True)
        m_t = jnp.maximum(log_inter, m_intra)
        d_mat = jnp.where(causal, jnp.exp(log_d - m_t), 0.0)
        w_inter = jnp.exp(log_inter - m_t)

        qh = q_ref[:, h * DQK_A:(h + 1) * DQK_A]
        kh = k_ref[:, h * DQK_A:(h + 1) * DQK_A]
        vh = v_ref[:, h * DV_A:(h + 1) * DV_A]
        c_mat = c_scr[h]
        n_vec = n_scr[h]
        s = _dot_nt(qh, kh) * d_mat
        num = _dot(s.astype(BF16), vh) + w_inter * _dot(qh, c_mat.astype(BF16))
        qn = jnp.sum(qh.astype(F32) * n_vec, axis=-1, keepdims=True)
        den = jnp.sum(s, axis=-1, keepdims=True) + w_inter * qn
        hh = num / jnp.maximum(jnp.abs(den), jnp.exp(-m_t))
        h_ref[:, h * DV_A:(h + 1) * DV_A] = hh * _rms_scale(hh)

        log_w_r = b_end + a_r
        m_new = jnp.maximum(b_end + m_run, jnp.max(log_w_r, axis=-1, keepdims=True))
        w_c = jnp.exp(b_end + a_c - m_new)
        decay = jnp.exp(b_end + m_run - m_new)
        kw = kh.astype(F32) * w_c
        c_scr[h] = decay * c_mat + _dot_tn(kw.astype(BF16), vh)
        n_scr[h] = decay * n_vec + jnp.sum(kw, axis=0, keepdims=True)
        m_scr[h] = jnp.broadcast_to(m_new, (1, LANES))


def _mlstm(q, k, v, gc, gr, bsz, seq):
    nc = seq // CHUNK
    row = lambda b, c: (b * nc + c, 0)
    return pl.pallas_call(
        _mlstm_kernel,
        grid=(bsz, nc),
        in_specs=[
            pl.BlockSpec((CHUNK, QK_A), row),
            pl.BlockSpec((CHUNK, QK_A), row),
            pl.BlockSpec((CHUNK, D_MODEL), row),
            pl.BlockSpec((CHUNK, LANES), row),
            pl.BlockSpec((8, CHUNK), lambda b, c: (0, b * nc + c)),
        ],
        out_specs=pl.BlockSpec((CHUNK, D_MODEL), row),
        out_shape=jax.ShapeDtypeStruct((bsz * seq, D_MODEL), F32),
        scratch_shapes=[
            pltpu.VMEM((N_HEADS_A, DQK_A, DV_A), F32),
            pltpu.VMEM((N_HEADS_A, 1, DQK_A), F32),
            pltpu.VMEM((N_HEADS_A, 1, LANES), F32),
        ],
        compiler_params=_params(("arbitrary", "arbitrary")),
        name="mlstm",
    )(q, k, v, gc, gr)


def _route_bins(logits):
    lane = lax.broadcasted_iota(jnp.int32, logits.shape, 1)
    big = jnp.int32(10 ** 6)

    def top(mask):
        mx = jnp.max(jnp.where(mask, logits, -jnp.inf), axis=-1, keepdims=True)
        return jnp.min(jnp.where(mask & (logits == mx), lane, big), axis=-1, keepdims=True)

    gidx = top(lane < N_GROUPS)
    lo = ROUTE_LANE0 + EXP_PER_GROUP * gidx
    emask = (lane >= lo) & (lane < lo + EXP_PER_GROUP)
    i1 = top(emask)
    i2 = top(emask & (lane != i1))
    p_lo = jnp.minimum(i1, i2) - lo
    p_hi = jnp.maximum(i1, i2) - lo
    pair = jnp.where(p_lo == 0, p_hi - 1, jnp.where(p_lo == 1, p_hi + 1, len(PAIR_LO) - 1))
    return gidx * len(PAIR_LO) + pair


def _pair_weights(logits, grp, e_lo, e_hi):
    lane = lax.broadcasted_iota(jnp.int32, logits.shape, 1)

    def pick(idx):
        return jnp.sum(jnp.where(lane == idx, logits, 0.0), axis=-1, keepdims=True)

    def softmax_stats(mask):
        mx = jnp.max(jnp.where(mask, logits, -jnp.inf), axis=-1, keepdims=True)
        return mx, jnp.sum(jnp.where(mask, jnp.exp(logits - mx), 0.0), axis=-1, keepdims=True)

    gmax, gsum = softmax_stats(lane < N_GROUPS)
    g_p = jnp.exp(pick(grp) - gmax) / gsum
    lo = ROUTE_LANE0 + EXP_PER_GROUP * grp
    emax, esum = softmax_stats((lane >= lo) & (lane < lo + EXP_PER_GROUP))
    p_lo = jnp.exp(pick(ROUTE_LANE0 + e_lo) - emax) / esum
    p_hi = jnp.exp(pick(ROUTE_LANE0 + e_hi) - emax) / esum
    return g_p * (p_lo / (p_lo + p_hi)), g_p * (p_hi / (p_lo + p_hi))


def _store_rows(rows_ref, x, tm):
    for c in range(ROW_SUB):
        rows_ref[pl.ds(c, tm, stride=ROW_SUB), :] = x[:, c * LANES:(c + 1) * LANES]


def _load_rows(rows_ref, tm):
    return jnp.concatenate([rows_ref[pl.ds(c, tm, stride=ROW_SUB), :] for c in range(ROW_SUB)], axis=-1)


def _ffn_norm_and_route(h, gn_ref, wr_ref, br_ref, xs_ref, route_ref, counts_ref, tm, first_step):
    @pl.when(first_step)
    def _():
        counts_ref[...] = jnp.zeros_like(counts_ref)

    xn = h * _rms_scale(h) * gn_ref[...]
    _store_rows(xs_ref, xn, tm)
    xh, xl = _split2(xn)
    bins = _route_bins(_dot_x3(xh, xl, wr_ref) + br_ref[...])
    lane = lax.broadcasted_iota(jnp.int32, (tm, LANES), 1)
    onehot = jnp.where(lane == bins, 1.0, 0.0)
    ti = lax.broadcasted_iota(jnp.int32, (tm, tm), 0)
    tj = lax.broadcasted_iota(jnp.int32, (tm, tm), 1)
    earlier = jnp.where(tj < ti, 1.0, 0.0).astype(BF16)
    before = _dot(earlier, onehot.astype(BF16))
    counts = counts_ref[...]
    rank = jnp.sum(onehot * (before + counts), axis=-1, keepdims=True)
    counts_ref[...] = counts + jnp.sum(onehot, axis=0, keepdims=True)
    route_ref[...] = jnp.where(lane == 0, bins.astype(F32), jnp.where(lane == 1, rank, 0.0))


def _out_a_kernel(hn_ref, o_ref, x_ref, hg_ref, wout_ref, gn_ref, wr_ref, br_ref,
                  h1_ref, xs_ref, route_ref, counts_ref, *, tm):
    z = (hn_ref[...] * hg_ref[...] * _sigmoid(o_ref[...])).astype(BF16)
    h1 = x_ref[...] + _dot(z, wout_ref[...])
    h1_ref[...] = h1
    _ffn_norm_and_route(h1, gn_ref, wr_ref, br_ref, xs_ref, route_ref, counts_ref, tm, pl.program_id(0) == 0)


def _route_out_specs(tm, row, const2):
    return [
        pl.BlockSpec((tm, D_MODEL), row),
        pl.BlockSpec((tm * ROW_SUB, LANES), row),
        pl.BlockSpec((tm, LANES), row),
        pl.BlockSpec((1, LANES), const2),
    ]


def _route_out_shapes(t):
    return [
        jax.ShapeDtypeStruct((t, D_MODEL), F32),
        jax.ShapeDtypeStruct((t * ROW_SUB, LANES), F32),
        jax.ShapeDtypeStruct((t, LANES), F32),
        jax.ShapeDtypeStruct((1, LANES), F32),
    ]


def _out_a(hn, o, x2d, hg, wout, gn, wr, br, tm):
    t = x2d.shape[0]
    row = lambda i: (i, 0)
    const2 = lambda i: (0, 0)
    const3 = lambda i: (0, 0, 0)
    return pl.pallas_call(
        functools.partial(_out_a_kernel, tm=tm),
        grid=(t // tm,),
        in_specs=[
            pl.BlockSpec((tm, D_MODEL), row),
            pl.BlockSpec((tm, D_MODEL), row),
            pl.BlockSpec((tm, D_MODEL), row),
            pl.BlockSpec((1, D_MODEL), const2),
            pl.BlockSpec((D_MODEL, D_MODEL), const2),
            pl.BlockSpec((1, D_MODEL), const2),
            pl.BlockSpec((2, D_MODEL, LANES), const3),
            pl.BlockSpec((1, LANES), const2),
        ],
        out_specs=_route_out_specs(tm, row, const2),
        out_shape=_route_out_shapes(t),
        compiler_params=_params(("arbitrary",)),
        name="out_a",
    )(hn, o, x2d, hg, wout, gn, wr, br)


def _move_rows_kernel(starts_ref, bin_ref, rank_ref, src_ref, *rest, td, scatter):
    dst_ref, sem = rest[-2], rest[-1]
    base = pl.program_id(0) * td

    def row_copy(src_row, dst_row, n_rows):
        return pltpu.make_async_copy(src_ref.at[pl.ds(src_row * ROW_SUB, n_rows * ROW_SUB)],
                                     dst_ref.at[pl.ds(dst_row * ROW_SUB, n_rows * ROW_SUB)], sem)

    def body(j, carry):
        slot = starts_ref[bin_ref[0, 0, j]] + rank_ref[0, 0, j]
        if scatter:
            row_copy(base + j, slot, 1).start()
        else:
            row_copy(slot, base + j, 1).start()
        return carry

    lax.fori_loop(0, td, body, 0, unroll=8)
    row_copy(0, 0, td).wait()


def _move_rows(starts, bins, ranks, src, n_dst_tokens, scatter):
    t = bins.shape[0]
    td = math.gcd(ROWS_PER_STEP, t)
    smem_tok = pl.BlockSpec((1, 1, td), lambda i, st: (i, 0, 0), memory_space=pltpu.SMEM)
    any_spec = pl.BlockSpec(memory_space=pl.ANY)
    operands = [starts, bins.reshape(t // td, 1, td), ranks.reshape(t // td, 1, td), src]
    in_specs = [smem_tok, smem_tok, any_spec]
    aliases = {}
    if scatter:
        operands.append(jnp.zeros((n_dst_tokens * ROW_SUB, LANES), F32))
        in_specs.append(any_spec)
        aliases = {4: 0}
    return pl.pallas_call(
        functools.partial(_move_rows_kernel, td=td, scatter=scatter),
        grid_spec=pltpu.PrefetchScalarGridSpec(
            num_scalar_prefetch=1,
            grid=(t // td,),
            in_specs=in_specs,
            out_specs=any_spec,
            scratch_shapes=[pltpu.SemaphoreType.DMA(())],
        ),
        out_shape=jax.ShapeDtypeStruct((n_dst_tokens * ROW_SUB, LANES), F32),
        input_output_aliases=aliases,
        compiler_params=_params(("arbitrary",)),
        name="moe_dispatch" if scatter else "moe_combine",
    )(*operands)


def _moe_kernel(elo_ref, ehi_ref, grp_ref, valid_ref, xs_ref, wr_ref, br_ref,
                wg_lo_ref, wu_lo_ref, wd_lo_ref, wg_hi_ref, wu_hi_ref, wd_hi_ref, ys_ref):
    i = pl.program_id(0)

    @pl.when(valid_ref[i] == 0)
    def _():
        ys_ref[...] = jnp.zeros_like(ys_ref)

    @pl.when(valid_ref[i] != 0)
    def _():
        x = _load_rows(xs_ref, MOE_TILE).astype(BF16)
        w_lo, w_hi = _pair_weights(_dot(x, wr_ref[...]) + br_ref[...], grp_ref[i], elo_ref[i], ehi_ref[i])
        y = None
        for wg_ref, wu_ref, wd_ref, w in ((wg_lo_ref, wu_lo_ref, wd_lo_ref, w_lo),
                                           (wg_hi_ref, wu_hi_ref, wd_hi_ref, w_hi)):
            g = _dot(x, wg_ref[0])
            u = _dot(x, wu_ref[0])
            d = _dot((g * _sigmoid(g) * u * w).astype(BF16), wd_ref[0])
            y = d if y is None else y + d
        _store_rows(ys_ref, y, MOE_TILE)


def _moe_plan(route, counts, n_tiles):
    cnt = counts[0, :N_BINS].astype(jnp.int32)
    padded = (cnt + (MOE_TILE - 1)) // MOE_TILE * MOE_TILE
    ends = jnp.cumsum(padded)
    starts = jnp.pad(ends - padded, (0, LANES - N_BINS))
    last_tile = jnp.maximum(ends[-1] // MOE_TILE - 1, 0)
    tile = jnp.arange(n_tiles, dtype=jnp.int32)
    tbin = jnp.searchsorted(ends, jnp.minimum(tile, last_tile) * MOE_TILE, side="right").astype(jnp.int32)
    tbin = jnp.minimum(tbin, N_BINS - 1)
    grp = tbin // len(PAIR_LO)
    pair = tbin % len(PAIR_LO)
    elo = grp * EXP_PER_GROUP + jnp.asarray(PAIR_LO, jnp.int32)[pair]
    ehi = grp * EXP_PER_GROUP + jnp.asarray(PAIR_HI, jnp.int32)[pair]
    valid = (tile <= last_tile).astype(jnp.int32) * (ends[-1] > 0).astype(jnp.int32)
    return starts, route[:, 0].astype(jnp.int32), route[:, 1].astype(jnp.int32), elo, ehi, grp, valid


def _moe_experts(xs_sorted, elo, ehi, grp, valid, wr, br, wg, wu, wd):
    n_tiles = xs_sorted.shape[0] // (MOE_TILE * ROW_SUB)
    rows = pl.BlockSpec((MOE_TILE * ROW_SUB, LANES), lambda i, *_: (i, 0))
    lo3 = lambda i, elo, ehi, grp, valid: (elo[i], 0, 0)
    hi3 = lambda i, elo, ehi, grp, valid: (ehi[i], 0, 0)
    const2 = lambda i, *_: (0, 0)
    up_shape, down_shape = (1, D_MODEL, D_EXPERT), (1, D_EXPERT, D_MODEL)
    return pl.pallas_call(
        _moe_kernel,
        grid_spec=pltpu.PrefetchScalarGridSpec(
            num_scalar_prefetch=4,
            grid=(n_tiles,),
            in_specs=[
                rows,
                pl.BlockSpec((D_MODEL, LANES), const2),
                pl.BlockSpec((1, LANES), const2),
                pl.BlockSpec(up_shape, lo3), pl.BlockSpec(up_shape, lo3), pl.BlockSpec(down_shape, lo3),
                pl.BlockSpec(up_shape, hi3), pl.BlockSpec(up_shape, hi3), pl.BlockSpec(down_shape, hi3),
            ],
            out_specs=rows,
        ),
        out_shape=jax.ShapeDtypeStruct(xs_sorted.shape, F32),
        compiler_params=_params(("arbitrary",)),
        name="moe_experts",
    )(elo, ehi, grp, valid, xs_sorted, wr, br, wg, wu, wd, wg, wu, wd)


def _moe(xs, route, counts, wr, br, wg, wu, wd):
    t = route.shape[0]
    n_sorted = t + N_BINS * MOE_TILE
    starts, bins, ranks, elo, ehi, grp, valid = _moe_plan(route, counts, n_sorted // MOE_TILE)
    xs_sorted = _move_rows(starts, bins, ranks, xs, n_sorted, scatter=True)
    ys_sorted = _moe_experts(xs_sorted, elo, ehi, grp, valid, wr, br, wg, wu, wd)
    return _move_rows(starts, bins, ranks, ys_sorted, t, scatter=False)


def _final_kernel(h_ref, y_ref, g_ref, out_ref, *, tm):
    h = h_ref[...] + _load_rows(y_ref, tm)
    out_ref[...] = h * _rms_scale(h) * g_ref[...]


def _final(h3, y, g, tm):
    t = h3.shape[0]
    row = lambda i: (i, 0)
    return pl.pallas_call(
        functools.partial(_final_kernel, tm=tm),
        grid=(t // tm,),
        in_specs=[
            pl.BlockSpec((tm, D_MODEL), row),
            pl.BlockSpec((tm * ROW_SUB, LANES), row),
            pl.BlockSpec((1, D_MODEL), lambda i: (0, 0)),
        ],
        out_specs=pl.BlockSpec((tm, D_MODEL), row),
        out_shape=jax.ShapeDtypeStruct((t, D_MODEL), F32),
        compiler_params=_params(("parallel",)),
        name="final_norm",
    )(h3, y, g)


def _head_rms(x, gsum_ref, gexp_ref):
    s_hi, s_lo = _split2(x * x)
    ms = _dot(s_hi, gsum_ref[...]) + _dot(s_lo, gsum_ref[...])
    r_hi, r_lo = _split2(lax.rsqrt(ms + EPS))
    return _dot(r_hi, gexp_ref[...]) + _dot(r_lo, gexp_ref[...])


def _proj_kvq_kernel(h_ref, ymoe_ref, gkv_ref, gq_ref, wk_ref, wvt_ref, wf_ref, bf_ref, wq_ref, wogt_ref,
                     kgain_ref, qgain_ref, gsum_ref, gexp_ref, sel_ref,
                     h2_ref, kaug_ref, vt_ref, qaug_ref, sgt_ref, cstat_ref, carry_ref, *, tm):
    @pl.when(pl.program_id(1) == 0)
    def _():
        carry_ref[...] = jnp.zeros_like(carry_ref)

    hres = h_ref[...] + _load_rows(ymoe_ref, tm)
    h2_ref[...] = hres
    y = hres * _rms_scale(hres)
    a = y * gkv_ref[...]
    ah, al = _split2(a)
    bh = (y * gq_ref[...]).astype(BF16)

    logf = _log_sigmoid(_dot_x3(ah, al, wf_ref) + bf_ref[...])
    ti = lax.broadcasted_iota(jnp.int32, (tm, tm), 0)
    tj = lax.broadcasted_iota(jnp.int32, (tm, tm), 1)
    tril = jnp.where(tj <= ti, 1.0, 0.0).astype(BF16)
    c = _dot_exact_lhs(tril, logf) + carry_ref[...]
    carry_ref[...] = c[tm - 1:tm, :]

    nb = c * (-LOG2E)
    for blk in range(tm // ATT_BLK):
        nb_blk = nb[blk * ATT_BLK:(blk + 1) * ATT_BLK, :]
        cstat_ref[0, blk, 0:1, :] = jnp.max(nb_blk, axis=0, keepdims=True)
        cstat_ref[0, blk, 1:2, :] = jnp.min(nb_blk, axis=0, keepdims=True)
    n0, n1, n2 = _split3(nb)
    lane = lax.broadcasted_iota(jnp.int32, (tm, LANES), 1)
    packed = jnp.where(lane < N_HEADS_B, n0.astype(F32),
                       jnp.where(lane < 2 * N_HEADS_B, pltpu.roll(n1.astype(F32), N_HEADS_B, 1),
                                 pltpu.roll(n2.astype(F32), 2 * N_HEADS_B, 1)))
    packed = jnp.where(lane < 3 * N_HEADS_B, packed, 0.0).astype(BF16)
    extras = _dot(packed, sel_ref[...])

    k = _dot(ah, wk_ref[...])
    kn = k * _head_rms(k, gsum_ref, gexp_ref) * kgain_ref[...]
    q = _dot(bh, wq_ref[...])
    qn = q * _head_rms(q, gsum_ref, gexp_ref) * qgain_ref[...]
    ones = jnp.where((lane >= BIAS_LANE) & (lane < BIAS_LANE + 3), 1.0, 0.0)
    for j in range(N_HEADS_B // 2):
        kj = kn[:, j * LANES:(j + 1) * LANES]
        qj = qn[:, j * LANES:(j + 1) * LANES]
        for half, (kk, qq) in enumerate(((kj, qj), (pltpu.roll(kj, DH_B, 1), pltpu.roll(qj, DH_B, 1)))):
            hd = 2 * j + half
            ex = extras[:, hd * LANES:(hd + 1) * LANES]
            kaug_ref[0, hd] = jnp.where(lane < DH_B, kk, ex).astype(BF16)
            qaug_ref[0, hd] = jnp.where(lane < DH_B, qq, ones).astype(BF16)

    vt = _dot_nt(wvt_ref[...], ah)
    sgt = _sigmoid(_dot_nt(wogt_ref[...], bh))
    for hd in range(N_HEADS_B):
        vt_ref[0, hd] = vt[hd * DH_B:(hd + 1) * DH_B, :].astype(BF16)
    sgt_ref[0] = sgt


def _proj_kvq(h1, ymoe, gkv, gq, wk, wvt, wf, bf, wq, wogt, kgain, qgain, gsum, gexp, sel, bsz, seq, tm):
    ns = seq // tm
    row = lambda b, s: (b * ns + s, 0)
    const2 = lambda b, s: (0, 0)
    const3 = lambda b, s: (0, 0, 0)
    return pl.pallas_call(
        functools.partial(_proj_kvq_kernel, tm=tm),
        grid=(bsz, ns),
        in_specs=[
            pl.BlockSpec((tm, D_MODEL), row),
            pl.BlockSpec((tm * ROW_SUB, LANES), row),
            pl.BlockSpec((1, D_MODEL), const2),
            pl.BlockSpec((1, D_MODEL), const2),
            pl.BlockSpec((D_MODEL, D_MODEL), const2),
            pl.BlockSpec((D_MODEL, D_MODEL), const2),
            pl.BlockSpec((2, D_MODEL, LANES), const3),
            pl.BlockSpec((1, LANES), const2),
            pl.BlockSpec((D_MODEL, D_MODEL), const2),
            pl.BlockSpec((D_MODEL, D_MODEL), const2),
            pl.BlockSpec((1, D_MODEL), const2),
            pl.BlockSpec((1, D_MODEL), const2),
            pl.BlockSpec((D_MODEL, LANES), const2),
            pl.BlockSpec((LANES, D_MODEL), const2),
            pl.BlockSpec((LANES, N_HEADS_B * LANES), const2),
        ],
        out_specs=[
            pl.BlockSpec((tm, D_MODEL), row),
            pl.BlockSpec((1, N_HEADS_B, tm, LANES), lambda b, s: (b, 0, s, 0)),
            pl.BlockSpec((1, N_HEADS_B, DH_B, tm), lambda b, s: (b, 0, 0, s)),
            pl.BlockSpec((1, N_HEADS_B, tm, LANES), lambda b, s: (b, 0, s, 0)),
            pl.BlockSpec((1, D_MODEL, tm), lambda b, s: (b, 0, s)),
            pl.BlockSpec((1, tm // ATT_BLK, 2, LANES), lambda b, s: (b, s, 0, 0)),
        ],
        out_shape=[
            jax.ShapeDtypeStruct((bsz * seq, D_MODEL), F32),
            jax.ShapeDtypeStruct((bsz, N_HEADS_B, seq, LANES), BF16),
            jax.ShapeDtypeStruct((bsz, N_HEADS_B, DH_B, seq), BF16),
            jax.ShapeDtypeStruct((bsz, N_HEADS_B, seq, LANES), BF16),
            jax.ShapeDtypeStruct((bsz, D_MODEL, seq), F32),
            jax.ShapeDtypeStruct((bsz, seq // ATT_BLK, 2, LANES), F32),
        ],
        scratch_shapes=[pltpu.VMEM((1, LANES), F32)],
        compiler_params=_params(("arbitrary", "arbitrary")),
        name="proj_kvq",
    )(h1, ymoe, gkv, gq, wk, wvt, wf, bf, wq, wogt, kgain, qgain, gsum, gexp, sel)


def _attn_kernel(trips_ref, q_ref, k_ref, vt_ref, o_ref, m_scr, l_scr, acc_scr, sa_scr, sb_scr, xa_scr, xb_scr,
                 p_scr, *, chains):
    b, h, i = pl.program_id(0), pl.program_id(1), pl.program_id(2)
    trips = trips_ref[(b * pl.num_programs(1) + h) * pl.num_programs(2) + i]
    blk = ATT_BLK
    lane = lax.broadcasted_iota(jnp.int32, (blk, LANES), 1)
    k_null = jnp.where(lane == BIAS_LANE, NEG_BIG, 0.0).astype(BF16)

    def key_offset(g, t):
        kb = i * chains + g - t
        return kb, pl.multiple_of(jnp.maximum(kb, 0) * blk, blk)

    def scores(t, s_ref, x_ref, first):
        for g in range(chains):
            q = q_ref[0, 0, g * blk:(g + 1) * blk, :]
            kb, off = key_offset(g, t)
            kblk = k_ref[0, 0, pl.ds(off, blk), :]
            if not first:
                kblk = jnp.where(kb >= 0, kblk, k_null)
            s = _dot_nt(kblk, q)
            if first:
                ki = lax.broadcasted_iota(jnp.int32, (blk, blk), 0)
                qj = lax.broadcasted_iota(jnp.int32, (blk, blk), 1)
                s = jnp.where(ki <= qj, s, NEG_BIG)
            s_ref[g] = s
            x_ref[g] = jnp.max(s, axis=0, keepdims=True)

    def update(t, s_ref, x_ref):
        for g in range(chains):
            _, off = key_offset(g, t)
            vblk = vt_ref[0, 0, :, pl.ds(off, blk)]
            m_old = m_scr[g]
            m_new = jnp.maximum(m_old, x_ref[g])
            p = jnp.exp2(s_ref[g] - m_new)
            alpha = jnp.exp2(m_old - m_new)
            l_scr[g] = alpha * l_scr[g] + jnp.sum(p, axis=0, keepdims=True)
            p_scr[g] = p.astype(BF16)
            acc_scr[g] = alpha * acc_scr[g] + _dot(vblk, p_scr[g])
            m_scr[g] = m_new

    m_scr[...] = jnp.full(m_scr.shape, NEG_BIG, F32)
    l_scr[...] = jnp.zeros(l_scr.shape, F32)
    acc_scr[...] = jnp.zeros(acc_scr.shape, F32)
    scores(0, sa_scr, xa_scr, True)
    steps = trips - 1

    def body(pair, carry):
        t = 2 * pair
        scores(t + 1, sb_scr, xb_scr, False)
        update(t, sa_scr, xa_scr)
        scores(t + 2, sa_scr, xa_scr, False)
        update(t + 1, sb_scr, xb_scr)
        return carry

    lax.fori_loop(0, steps >> 1, body, 0)

    @pl.when((steps & 1) == 1)
    def _():
        scores(steps, sb_scr, xb_scr, False)
        update(steps - 1, sa_scr, xa_scr)
        update(steps, sb_scr, xb_scr)

    @pl.when((steps & 1) == 0)
    def _():
        update(steps, sa_scr, xa_scr)

    for g in range(chains):
        o_ref[0, 0, :, g * blk:(g + 1) * blk] = acc_scr[g] / l_scr[g]


def _attn_trips(cstat, kgain, qgain, chains):
    nbmax = jnp.swapaxes(cstat[:, :, 0, :N_HEADS_B], 1, 2)
    nbmin = jnp.swapaxes(cstat[:, :, 1, :N_HEADS_B], 1, 2)
    nblk = nbmax.shape[-1]
    xb = DH_B * jnp.max(jnp.abs(kgain)) * jnp.max(jnp.abs(qgain))
    slack = 2.0 * xb * 1.02 + 2.0 + SKIP_LOG2
    j = jnp.arange(nblk)
    keep = (nbmax[:, :, None, :] - nbmin[:, :, :, None] + slack >= 0.0) & (j[None, :] <= j[:, None])
    jmin = jnp.min(jnp.where(keep, j[None, :], nblk), axis=-1)
    need = j - jnp.minimum(jmin, j) + 1
    return jnp.max(need.reshape(need.shape[0], need.shape[1], nblk // chains, chains), axis=-1).astype(jnp.int32)


def _attn(qaug, kaug, vt, trips, chains):
    bsz, nh, seq, _ = qaug.shape
    tq = chains * ATT_BLK
    grid_spec = pltpu.PrefetchScalarGridSpec(
        num_scalar_prefetch=1,
        grid=(bsz, nh, seq // tq),
        in_specs=[
            pl.BlockSpec((1, 1, tq, LANES), lambda b, h, i, tr: (b, h, i, 0)),
            pl.BlockSpec((1, 1, seq, LANES), lambda b, h, i, tr: (b, h, 0, 0)),
            pl.BlockSpec((1, 1, DH_B, seq), lambda b, h, i, tr: (b, h, 0, 0)),
        ],
        out_specs=pl.BlockSpec((1, 1, DH_B, tq), lambda b, h, i, tr: (b, h, 0, i)),
        scratch_shapes=[
            pltpu.VMEM((chains, 1, ATT_BLK), F32),
            pltpu.VMEM((chains, 1, ATT_BLK), F32),
            pltpu.VMEM((chains, DH_B, ATT_BLK), F32),
            pltpu.VMEM((chains, ATT_BLK, ATT_BLK), F32),
            pltpu.VMEM((chains, ATT_BLK, ATT_BLK), F32),
            pltpu.VMEM((chains, 1, ATT_BLK), F32),
            pltpu.VMEM((chains, 1, ATT_BLK), F32),
            pltpu.VMEM((chains, ATT_BLK, ATT_BLK), BF16),
        ],
    )
    return pl.pallas_call(
        functools.partial(_attn_kernel, chains=chains),
        grid_spec=grid_spec,
        out_shape=jax.ShapeDtypeStruct((bsz, nh, DH_B, seq), F32),
        compiler_params=_params(("parallel", "parallel", "arbitrary")),
        name="attn",
    )(trips.reshape(-1), qaug, kaug, vt)


def _out_b_kernel(ot_ref, sgt_ref, h_ref, wout_ref, gn_ref, wr_ref, br_ref,
                  h3_ref, xs_ref, route_ref, counts_ref, *, tm):
    zt = (ot_ref[0] * sgt_ref[0]).astype(BF16)
    h3 = h_ref[...] + _dot_tn(zt, wout_ref[...])
    h3_ref[...] = h3
    first_step = (pl.program_id(0) == 0) & (pl.program_id(1) == 0)
    _ffn_norm_and_route(h3, gn_ref, wr_ref, br_ref, xs_ref, route_ref, counts_ref, tm, first_step)


def _out_b(ot, sgt, h2, wout, gn, wr, br, bsz, seq, tm):
    ns = seq // tm
    row = lambda b, s: (b * ns + s, 0)
    const2 = lambda b, s: (0, 0)
    const3 = lambda b, s: (0, 0, 0)
    t = bsz * seq
    return pl.pallas_call(
        functools.partial(_out_b_kernel, tm=tm),
        grid=(bsz, ns),
        in_specs=[
            pl.BlockSpec((1, D_MODEL, tm), lambda b, s: (b, 0, s)),
            pl.BlockSpec((1, D_MODEL, tm), lambda b, s: (b, 0, s)),
            pl.BlockSpec((tm, D_MODEL), row),
            pl.BlockSpec((D_MODEL, D_MODEL), const2),
            pl.BlockSpec((1, D_MODEL), const2),
            pl.BlockSpec((2, D_MODEL, LANES), const3),
            pl.BlockSpec((1, LANES), const2),
        ],
        out_specs=_route_out_specs(tm, row, const2),
        out_shape=_route_out_shapes(t),
        compiler_params=_params(("arbitrary", "arbitrary")),
        name="out_b",
    )(ot, sgt, h2, wout, gn, wr, br)


def _hi_lo(w):
    hi = w.astype(BF16)
    lo = (w - hi.astype(F32)).astype(BF16)
    return jnp.stack([hi, lo])


def _pad_lanes(w, width=LANES):
    return jnp.pad(w, ((0, 0),) * (w.ndim - 1) + ((0, width - w.shape[-1]),))


def _router_params(w_grp, b_grp, w_exp, b_exp):
    w = _pad_lanes(jnp.concatenate([w_grp, w_exp], axis=-1))
    b = _pad_lanes(jnp.concatenate([b_grp, b_exp], axis=-1)[None, :])
    return _hi_lo(w), b


def _tile(seq, pref):
    t = pref
    while seq % t:
        t //= 2
    return t


def kernel(x, norm_mix, norm_ffn, a_w_in, a_b_gate, a_head_gain, a_w_out, kv_norm, kv_w, kv_b_f, kv_k_gain,
           b_w_in, b_q_gain, b_w_out, moe_w_grp, moe_b_grp, moe_w_exp, moe_b_exp, moe_w_gate, moe_w_up,
           moe_w_down, norm_final):
    bsz, seq, _ = x.shape
    t = bsz * seq
    assert seq % CHUNK == 0 and seq % ATT_BLK == 0
    tm = _tile(seq, 512)
    assert tm % ATT_BLK == 0
    x2d = x.reshape(t, D_MODEL)

    w_in = a_w_in[0]
    wq = w_in[:, :QK_A].astype(BF16)
    wk = w_in[:, QK_A:2 * QK_A].astype(BF16)
    wv = w_in[:, 2 * QK_A:2 * QK_A + D_MODEL].astype(BF16)
    wo = w_in[:, 2 * QK_A + D_MODEL:2 * QK_A + 2 * D_MODEL].astype(BF16)
    wgate = w_in[:, 2 * QK_A + 2 * D_MODEL:]
    wgc = _hi_lo(_pad_lanes(wgate))
    wgr = _hi_lo(wgate.T)
    bgc = _pad_lanes(a_b_gate[0][None, :])
    bgr = a_b_gate[0][:, None]
    q, k, v, o, gc, gr = _proj_a(x2d, norm_mix[0][None, :], wq, wk, wv, wo, wgc, wgr, bgc, bgr, tm)
    hn = _mlstm(q, k, v, gc, gr, bsz, seq)
    wr0, br0 = _router_params(moe_w_grp[0], moe_b_grp[0], moe_w_exp[0], moe_b_exp[0])
    h1, xs1, route1, counts1 = _out_a(hn, o, x2d, a_head_gain[0][None, :], a_w_out[0].astype(BF16),
                                      norm_ffn[0][None, :], wr0, br0, tm)
    y1 = _moe(xs1, route1, counts1, wr0[0], br0, moe_w_gate[0].astype(BF16), moe_w_up[0].astype(BF16),
              moe_w_down[0].astype(BF16))

    wkk = kv_w[:, :D_MODEL].astype(BF16)
    wvt = kv_w[:, D_MODEL:2 * D_MODEL].T.astype(BF16)
    wf = _hi_lo(_pad_lanes(kv_w[:, 2 * D_MODEL:]))
    bf = _pad_lanes(kv_b_f[None, :])
    wq1 = b_w_in[0][:, :D_MODEL].astype(BF16)
    wogt = b_w_in[0][:, D_MODEL:].T.astype(BF16)
    kgain = jnp.tile(kv_k_gain, N_HEADS_B)[None, :]
    qgain = jnp.tile(b_q_gain[0], N_HEADS_B)[None, :] * (DH_B ** -0.5 * LOG2E)
    head_of = jnp.arange(D_MODEL) // DH_B
    gsum = (head_of[:, None] == jnp.arange(LANES)[None, :]).astype(BF16) * (1.0 / DH_B)
    gexp = (jnp.arange(LANES)[:, None] == head_of[None, :]).astype(BF16)
    src = jnp.arange(LANES)[:, None]
    dst = jnp.arange(N_HEADS_B * LANES)[None, :]
    sel = ((src < 3 * N_HEADS_B) & (dst == (src % N_HEADS_B) * LANES + BIAS_LANE + src // N_HEADS_B)).astype(BF16)
    h2, kaug, vt, qaug, sgt, cstat = _proj_kvq(h1, y1, kv_norm[None, :], norm_mix[1][None, :], wkk, wvt, wf, bf,
                                               wq1, wogt, kgain, qgain, gsum, gexp, sel, bsz, seq, tm)
    chains = math.gcd(ATT_CHAINS, seq // ATT_BLK)
    ot = _attn(qaug, kaug, vt, _attn_trips(cstat, kgain, qgain, chains), chains)
    wr1, br1 = _router_params(moe_w_grp[1], moe_b_grp[1], moe_w_exp[1], moe_b_exp[1])
    h3, xs3, route3, counts3 = _out_b(ot.reshape(bsz, D_MODEL, seq), sgt, h2, b_w_out[0].astype(BF16),
                                      norm_ffn[1][None, :], wr1, br1, bsz, seq, tm)
    y3 = _moe(xs3, route3, counts3, wr1[0], br1, moe_w_gate[1].astype(BF16), moe_w_up[1].astype(BF16),
              moe_w_down[1].astype(BF16))
    out = _final(h3, y3, norm_final[None, :], tm)
    return out.reshape(bsz, seq, D_MODEL)
```

```python
import functools
import math

import jax
import jax.numpy as jnp
from jax import lax
from jax.experimental import pallas as pl
from jax.experimental.pallas import tpu as pltpu

F32 = jnp.float32
BF16 = jnp.bfloat16

D_MODEL = 1024
EPS = 1e-6

N_HEADS_A = 4
DV_A = D_MODEL // N_HEADS_A
DQK_A = DV_A // 2
QK_A = N_HEADS_A * DQK_A
GATE_CAP = 15.0
CHUNK = 128
CHUNK_SHIFT = CHUNK.bit_length() - 1

DH_B = 64
N_HEADS_B = D_MODEL // DH_B
LOG2E = 1.4426950408889634
BIAS_LANE = DH_B
NEG_BIG = -1e30
ATT_BLK = 256
ATT_CHAINS = 8
SKIP_LOG2 = 160.0

N_GROUPS = 4
EXP_PER_GROUP = 4
N_EXPERTS = N_GROUPS * EXP_PER_GROUP
D_EXPERT = D_MODEL // 2
ROUTE_LANE0 = N_GROUPS
PAIR_LO = (0, 0, 0, 1, 1, 2)
PAIR_HI = (1, 2, 3, 2, 3, 3)
N_BINS = N_GROUPS * len(PAIR_LO)
MOE_TILE = 256
ROW_SUB = 8
ROWS_PER_STEP = 1024

LANES = 128
VMEM_LIMIT = 56 * 1024 * 1024


def _params(sem, vmem=VMEM_LIMIT):
    return pltpu.CompilerParams(dimension_semantics=sem, vmem_limit_bytes=vmem)


def _dot(a, b):
    return jnp.dot(a, b, preferred_element_type=F32)


def _dot_nt(a, b):
    return lax.dot_general(a, b, (((1,), (1,)), ((), ())), preferred_element_type=F32)


def _dot_tn(a, b):
    return lax.dot_general(a, b, (((0,), (0,)), ((), ())), preferred_element_type=F32)


def _split2(x):
    hi = x.astype(BF16)
    lo = (x - hi.astype(F32)).astype(BF16)
    return hi, lo


def _split3(x):
    hi = x.astype(BF16)
    r = x - hi.astype(F32)
    mid = r.astype(BF16)
    lo = (r - mid.astype(F32)).astype(BF16)
    return hi, mid, lo


def _dot_x3(x_hi, x_lo, w_ref):
    return _dot(x_hi, w_ref[0]) + _dot(x_lo, w_ref[0]) + _dot(x_hi, w_ref[1])


def _dot_nt_x3(w_ref, x_hi, x_lo):
    return _dot_nt(w_ref[0], x_hi) + _dot_nt(w_ref[0], x_lo) + _dot_nt(w_ref[1], x_hi)


def _dot_exact_rhs(a, b_exact):
    a0, a1, a2 = _split3(a)
    return _dot(a0, b_exact) + _dot(a1, b_exact) + _dot(a2, b_exact)


def _dot_exact_lhs(a_exact, b):
    b0, b1, b2 = _split3(b)
    return _dot(a_exact, b0) + _dot(a_exact, b1) + _dot(a_exact, b2)


def _rms_scale(x):
    return lax.rsqrt(jnp.mean(x * x, axis=-1, keepdims=True) + EPS)


def _log_sigmoid(z):
    return jnp.minimum(z, 0.0) - jnp.log(1.0 + jnp.exp(-jnp.abs(z)))


def _sigmoid(z):
    return 1.0 / (1.0 + jnp.exp(-z))


def _softcap(z):
    return GATE_CAP * jnp.tanh(z / GATE_CAP)


def _proj_a_kernel(x_ref, g_ref, wq_ref, wk_ref, wv_ref, wo_ref, wgc_ref, wgr_ref, bgc_ref, bgr_ref,
                   q_ref, k_ref, v_ref, o_ref, gc_ref, gr_ref, *, tm):
    x = x_ref[...]
    xn = x * _rms_scale(x) * g_ref[...]
    xh, xl = _split2(xn)
    q_ref[...] = _dot(xh, wq_ref[...]).astype(BF16)
    k_ref[...] = (_dot(xh, wk_ref[...]) * (DQK_A ** -0.5)).astype(BF16)
    v_ref[...] = _dot(xh, wv_ref[...]).astype(BF16)
    o_ref[...] = _dot(xh, wo_ref[...])

    zc = _softcap(_dot_x3(xh, xl, wgc_ref) + bgc_ref[...])
    zr = _softcap(_dot_nt_x3(wgr_ref, xh, xl) + bgr_ref[...])
    lane = lax.broadcasted_iota(jnp.int32, zc.shape, 1)
    sub = lax.broadcasted_iota(jnp.int32, zr.shape, 0)
    vc = jnp.where(lane < N_HEADS_A, zc, _log_sigmoid(zc))
    vr = jnp.where(sub < N_HEADS_A, zr, _log_sigmoid(zr))
    ti = lax.broadcasted_iota(jnp.int32, (tm, tm), 0)
    tj = lax.broadcasted_iota(jnp.int32, (tm, tm), 1)
    same = (ti >> CHUNK_SHIFT) == (tj >> CHUNK_SHIFT)
    tril = jnp.where(same & (tj <= ti), 1.0, 0.0).astype(BF16)
    triu = jnp.where(same & (ti <= tj), 1.0, 0.0).astype(BF16)
    cc = _dot_exact_lhs(tril, vc)
    cr = _dot_exact_rhs(vr, triu)
    gc_ref[...] = jnp.where(lane < N_HEADS_A, vc, cc)
    gr_ref[...] = jnp.where(sub < N_HEADS_A, vr, cr)


def _proj_a(x2d, g, wq, wk, wv, wo, wgc, wgr, bgc, bgr, tm):
    t = x2d.shape[0]
    row = lambda i: (i, 0)
    const2 = lambda i: (0, 0)
    const3 = lambda i: (0, 0, 0)
    return pl.pallas_call(
        functools.partial(_proj_a_kernel, tm=tm),
        grid=(t // tm,),
        in_specs=[
            pl.BlockSpec((tm, D_MODEL), row),
            pl.BlockSpec((1, D_MODEL), const2),
            pl.BlockSpec((D_MODEL, QK_A), const2),
            pl.BlockSpec((D_MODEL, QK_A), const2),
            pl.BlockSpec((D_MODEL, D_MODEL), const2),
            pl.BlockSpec((D_MODEL, D_MODEL), const2),
            pl.BlockSpec((2, D_MODEL, LANES), const3),
            pl.BlockSpec((2, 8, D_MODEL), const3),
            pl.BlockSpec((1, LANES), const2),
            pl.BlockSpec((8, 1), const2),
        ],
        out_specs=[
            pl.BlockSpec((tm, QK_A), row),
            pl.BlockSpec((tm, QK_A), row),
            pl.BlockSpec((tm, D_MODEL), row),
            pl.BlockSpec((tm, D_MODEL), row),
            pl.BlockSpec((tm, LANES), row),
            pl.BlockSpec((8, tm), lambda i: (0, i)),
        ],
        out_shape=[
            jax.ShapeDtypeStruct((t, QK_A), BF16),
            jax.ShapeDtypeStruct((t, QK_A), BF16),
            jax.ShapeDtypeStruct((t, D_MODEL), BF16),
            jax.ShapeDtypeStruct((t, D_MODEL), F32),
            jax.ShapeDtypeStruct((t, LANES), F32),
            jax.ShapeDtypeStruct((8, t), F32),
        ],
        compiler_params=_params(("parallel",)),
        name="proj_a",
    )(x2d, g, wq, wk, wv, wo, wgc, wgr, bgc, bgr)


def _mlstm_kernel(q_ref, k_ref, v_ref, gc_ref, gr_ref, h_ref, c_scr, n_scr, m_scr):
    @pl.when(pl.program_id(1) == 0)
    def _():
        c_scr[...] = jnp.zeros_like(c_scr)
        n_scr[...] = jnp.zeros_like(n_scr)
        m_scr[...] = jnp.zeros_like(m_scr)

    ti = lax.broadcasted_iota(jnp.int32, (CHUNK, CHUNK), 0)
    si = lax.broadcasted_iota(jnp.int32, (CHUNK, CHUNK), 1)
    causal = si <= ti
    gc = gc_ref[...]
    gr = gr_ref[...]
    heads = range(N_HEADS_A)
    qs = [q_ref[:, h * DQK_A:(h + 1) * DQK_A] for h in heads]
    ks = [k_ref[:, h * DQK_A:(h + 1) * DQK_A] for h in heads]
    vs = [v_ref[:, h * DV_A:(h + 1) * DV_A] for h in heads]

    gates = []
    for h in heads:
        li_c = gc[:, h:h + 1]
        b_c = gc[:, N_HEADS_A + h:N_HEADS_A + h + 1]
        li_r = gr[h:h + 1, :]
        b_r = gr[N_HEADS_A + h:N_HEADS_A + h + 1, :]
        a_r = li_r - b_r
        a_c = li_c - b_c
        b_end = b_c[CHUNK - 1:CHUNK, :]
        m_run = m_scr[h][:, 0:1]
        log_inter = b_c + m_run
        log_d = b_c + a_r
        m_intra = jnp.max(jnp.where(causal, log_d, -jnp.inf), axis=-1, keepdims=True)
        m_t = jnp.maximum(log_inter, m_intra)
        d_mat = jnp.where(causal, jnp.exp(log_d - m_t), 0.0)
        w_inter = jnp.exp(log_inter - m_t)
        m_new = jnp.maximum(b_end + m_run, jnp.max(b_end + a_r, axis=-1, keepdims=True))
        w_c = jnp.exp(b_end + a_c - m_new)
        decay = jnp.exp(b_end + m_run - m_new)
        gates.append((m_t, d_mat, w_inter, m_new, w_c, decay))

    kws = [ks[h].astype(F32) * gates[h][4] for h in heads]
    s_raw = [_dot_nt(qs[h], ks[h]) for h in heads]
    inter = [_dot(qs[h], c_scr[h].astype(BF16)) for h in heads]
    c_upd = [_dot_tn(kws[h].astype(BF16), vs[h]) for h in heads]
    for h in heads:
        m_t, d_mat, w_inter, m_new, w_c, decay = gates[h]
        n_vec = n_scr[h]
        s = s_raw[h] * d_mat
        num = _dot(s.astype(BF16), vs[h]) + w_inter * inter[h]
        qn = jnp.sum(qs[h].astype(F32) * n_vec, axis=-1, keepdims=True)
        den = jnp.sum(s, axis=-1, keepdims=True) + w_inter * qn
        hh = num / jnp.maximum(jnp.abs(den), jnp.exp(-m_t))
        h_ref[:, h * DV_A:(h + 1) * DV_A] = hh * _rms_scale(hh)
        c_scr[h] = decay * c_scr[h] + c_upd[h]
        n_scr[h] = decay * n_vec + jnp.sum(kws[h], axis=0, keepdims=True)
        m_scr[h] = jnp.broadcast_to(m_new, (1, LANES))


def _mlstm(q, k, v, gc, gr, bsz, seq):
    nc = seq // CHUNK
    row = lambda b, c: (b * nc + c, 0)
    return pl.pallas_call(
        _mlstm_kernel,
        grid=(bsz, nc),
        in_specs=[
            pl.BlockSpec((CHUNK, QK_A), row),
            pl.BlockSpec((CHUNK, QK_A), row),
            pl.BlockSpec((CHUNK, D_MODEL), row),
            pl.BlockSpec((CHUNK, LANES), row),
            pl.BlockSpec((8, CHUNK), lambda b, c: (0, b * nc + c)),
        ],
        out_specs=pl.BlockSpec((CHUNK, D_MODEL), row),
        out_shape=jax.ShapeDtypeStruct((bsz * seq, D_MODEL), F32),
        scratch_shapes=[
            pltpu.VMEM((N_HEADS_A, DQK_A, DV_A), F32),
            pltpu.VMEM((N_HEADS_A, 1, DQK_A), F32),
            pltpu.VMEM((N_HEADS_A, 1, LANES), F32),
        ],
        compiler_params=_params(("arbitrary", "arbitrary")),
        name="mlstm",
    )(q, k, v, gc, gr)


def _route_bins(logits):
    lane = lax.broadcasted_iota(jnp.int32, logits.shape, 1)
    big = jnp.int32(10 ** 6)

    def top(mask):
        mx = jnp.max(jnp.where(mask, logits, -jnp.inf), axis=-1, keepdims=True)
        return jnp.min(jnp.where(mask & (logits == mx), lane, big), axis=-1, keepdims=True)

    gidx = top(lane < N_GROUPS)
    lo = ROUTE_LANE0 + EXP_PER_GROUP * gidx
    emask = (lane >= lo) & (lane < lo + EXP_PER_GROUP)
    i1 = top(emask)
    i2 = top(emask & (lane != i1))
    p_lo = jnp.minimum(i1, i2) - lo
    p_hi = jnp.maximum(i1, i2) - lo
    pair = jnp.where(p_lo == 0, p_hi - 1, jnp.where(p_lo == 1, p_hi + 1, len(PAIR_LO) - 1))
    return gidx * len(PAIR_LO) + pair


def _pair_weights(logits, grp, e_lo, e_hi):
    lane = lax.broadcasted_iota(jnp.int32, logits.shape, 1)

    def pick(idx):
        return jnp.sum(jnp.where(lane == idx, logits, 0.0), axis=-1, keepdims=True)

    def softmax_stats(mask):
        mx = jnp.max(jnp.where(mask, logits, -jnp.inf), axis=-1, keepdims=True)
        return mx, jnp.sum(jnp.where(mask, jnp.exp(logits - mx), 0.0), axis=-1, keepdims=True)

    gmax, gsum = softmax_stats(lane < N_GROUPS)
    g_p = jnp.exp(pick(grp) - gmax) / gsum
    lo = ROUTE_LANE0 + EXP_PER_GROUP * grp
    emax, esum = softmax_stats((lane >= lo) & (lane < lo + EXP_PER_GROUP))
    p_lo = jnp.exp(pick(ROUTE_LANE0 + e_lo) - emax) / esum
    p_hi = jnp.exp(pick(ROUTE_LANE0 + e_hi) - emax) / esum
    return g_p * (p_lo / (p_lo + p_hi)), g_p * (p_hi / (p_lo + p_hi))


def _store_rows(rows_ref, x, tm):
    for c in range(ROW_SUB):
        rows_ref[pl.ds(c, tm, stride=ROW_SUB), :] = x[:, c * LANES:(c + 1) * LANES]


def _load_rows(rows_ref, tm):
    return jnp.concatenate([rows_ref[pl.ds(c, tm, stride=ROW_SUB), :] for c in range(ROW_SUB)], axis=-1)


def _ffn_norm_and_route(h, gn_ref, wr_ref, br_ref, xs_ref, route_ref, counts_ref, tm, first_step):
    @pl.when(first_step)
    def _():
        counts_ref[...] = jnp.zeros_like(counts_ref)

    xn = h * _rms_scale(h) * gn_ref[...]
    _store_rows(xs_ref, xn, tm)
    xh, xl = _split2(xn)
    bins = _route_bins(_dot_x3(xh, xl, wr_ref) + br_ref[...])
    lane = lax.broadcasted_iota(jnp.int32, (tm, LANES), 1)
    onehot = jnp.where(lane == bins, 1.0, 0.0)
    ti = lax.broadcasted_iota(jnp.int32, (tm, tm), 0)
    tj = lax.broadcasted_iota(jnp.int32, (tm, tm), 1)
    earlier = jnp.where(tj < ti, 1.0, 0.0).astype(BF16)
    before = _dot(earlier, onehot.astype(BF16))
    counts = counts_ref[...]
    rank = jnp.sum(onehot * (before + counts), axis=-1, keepdims=True)
    counts_ref[...] = counts + jnp.sum(onehot, axis=0, keepdims=True)
    route_ref[...] = jnp.where(lane == 0, bins.astype(F32), jnp.where(lane == 1, rank, 0.0))


def _out_a_kernel(hn_ref, o_ref, x_ref, hg_ref, wout_ref, gn_ref, wr_ref, br_ref,
                  h1_ref, xs_ref, route_ref, counts_ref, *, tm):
    z = (hn_ref[...] * hg_ref[...] * _sigmoid(o_ref[...])).astype(BF16)
    h1 = x_ref[...] + _dot(z, wout_ref[...])
    h1_ref[...] = h1
    _ffn_norm_and_route(h1, gn_ref, wr_ref, br_ref, xs_ref, route_ref, counts_ref, tm, pl.program_id(0) == 0)


def _route_out_specs(tm, row, const2):
    return [
        pl.BlockSpec((tm, D_MODEL), row),
        pl.BlockSpec((tm * ROW_SUB, LANES), row),
        pl.BlockSpec((tm, LANES), row),
        pl.BlockSpec((1, LANES), const2),
    ]


def _route_out_shapes(t):
    return [
        jax.ShapeDtypeStruct((t, D_MODEL), F32),
        jax.ShapeDtypeStruct((t * ROW_SUB, LANES), F32),
        jax.ShapeDtypeStruct((t, LANES), F32),
        jax.ShapeDtypeStruct((1, LANES), F32),
    ]


def _out_a(hn, o, x2d, hg, wout, gn, wr, br, tm):
    t = x2d.shape[0]
    row = lambda i: (i, 0)
    const2 = lambda i: (0, 0)
    const3 = lambda i: (0, 0, 0)
    return pl.pallas_call(
        functools.partial(_out_a_kernel, tm=tm),
        grid=(t // tm,),
        in_specs=[
            pl.BlockSpec((tm, D_MODEL), row),
            pl.BlockSpec((tm, D_MODEL), row),
            pl.BlockSpec((tm, D_MODEL), row),
            pl.BlockSpec((1, D_MODEL), const2),
            pl.BlockSpec((D_MODEL, D_MODEL), const2),
            pl.BlockSpec((1, D_MODEL), const2),
            pl.BlockSpec((2, D_MODEL, LANES), const3),
            pl.BlockSpec((1, LANES), const2),
        ],
        out_specs=_route_out_specs(tm, row, const2),
        out_shape=_route_out_shapes(t),
        compiler_params=_params(("arbitrary",)),
        name="out_a",
    )(hn, o, x2d, hg, wout, gn, wr, br)


def _dispatch_kernel(slot_ref, src_ref, zeros_ref, dst_ref, sem, *, td):
    del zeros_ref

    def body(j, carry):
        row = pl.multiple_of(slot_ref[0, 0, j] * ROW_SUB, ROW_SUB)
        pltpu.make_async_copy(src_ref.at[pl.ds(pl.multiple_of(j * ROW_SUB, ROW_SUB), ROW_SUB)],
                              dst_ref.at[pl.ds(row, ROW_SUB)], sem).start()
        return carry

    lax.fori_loop(0, td, body, 0, unroll=8)
    pltpu.make_async_copy(src_ref, dst_ref.at[pl.ds(0, td * ROW_SUB)], sem).wait()


def _combine_kernel(slot_ref, src_ref, out_ref, sem, *, td):
    def body(j, carry):
        row = pl.multiple_of(slot_ref[0, 0, j] * ROW_SUB, ROW_SUB)
        pltpu.make_async_copy(src_ref.at[pl.ds(row, ROW_SUB)],
                              out_ref.at[pl.ds(pl.multiple_of(j * ROW_SUB, ROW_SUB), ROW_SUB)], sem).start()
        return carry

    lax.fori_loop(0, td, body, 0, unroll=8)
    pltpu.make_async_copy(src_ref.at[pl.ds(0, td * ROW_SUB)], out_ref, sem).wait()


def _move_rows(slots, src, n_dst_tokens, scatter):
    t = slots.shape[0]
    td = math.gcd(ROWS_PER_STEP, t)
    smem_tok = pl.BlockSpec((1, 1, td), lambda i: (i, 0, 0), memory_space=pltpu.SMEM)
    vmem_rows = pl.BlockSpec((td * ROW_SUB, LANES), lambda i: (i, 0))
    any_spec = pl.BlockSpec(memory_space=pl.ANY)
    slots3 = slots.reshape(t // td, 1, td)
    if scatter:
        operands = (slots3, src, jnp.zeros((n_dst_tokens * ROW_SUB, LANES), F32))
        body, in_specs, out_spec, aliases = _dispatch_kernel, [smem_tok, vmem_rows, any_spec], any_spec, {2: 0}
    else:
        operands = (slots3, src)
        body, in_specs, out_spec, aliases = _combine_kernel, [smem_tok, any_spec], vmem_rows, {}
    return pl.pallas_call(
        functools.partial(body, td=td),
        grid=(t // td,),
        in_specs=in_specs,
        out_specs=out_spec,
        scratch_shapes=[pltpu.SemaphoreType.DMA(())],
        out_shape=jax.ShapeDtypeStruct((n_dst_tokens * ROW_SUB, LANES), F32),
        input_output_aliases=aliases,
        compiler_params=_params(("arbitrary",)),
        name="moe_dispatch" if scatter else "moe_combine",
    )(*operands)


def _moe_kernel(elo_ref, ehi_ref, grp_ref, valid_ref, xs_ref, wr_ref, br_ref,
                wg_lo_ref, wu_lo_ref, wd_lo_ref, wg_hi_ref, wu_hi_ref, wd_hi_ref, ys_ref):
    i = pl.program_id(0)

    @pl.when(valid_ref[i] == 0)
    def _():
        ys_ref[...] = jnp.zeros_like(ys_ref)

    @pl.when(valid_ref[i] != 0)
    def _():
        x = _load_rows(xs_ref, MOE_TILE).astype(BF16)
        w_lo, w_hi = _pair_weights(_dot(x, wr_ref[...]) + br_ref[...], grp_ref[i], elo_ref[i], ehi_ref[i])
        y = None
        for wg_ref, wu_ref, wd_ref, w in ((wg_lo_ref, wu_lo_ref, wd_lo_ref, w_lo),
                                           (wg_hi_ref, wu_hi_ref, wd_hi_ref, w_hi)):
            g = _dot(x, wg_ref[0])
            u = _dot(x, wu_ref[0])
            d = _dot((g * _sigmoid(g) * u * w).astype(BF16), wd_ref[0])
            y = d if y is None else y + d
        _store_rows(ys_ref, y, MOE_TILE)


def _moe_plan(route, counts, n_tiles):
    cnt = counts[0, :N_BINS].astype(jnp.int32)
    padded = (cnt + (MOE_TILE - 1)) // MOE_TILE * MOE_TILE
    ends = jnp.cumsum(padded)
    starts = ends - padded
    slots = starts[route[:, 0].astype(jnp.int32)] + route[:, 1].astype(jnp.int32)
    last_tile = ends[-1] // MOE_TILE - 1
    tile = jnp.arange(n_tiles, dtype=jnp.int32)
    first_row = jnp.minimum(tile, last_tile) * MOE_TILE
    tbin = jnp.sum((ends[None, :] <= first_row[:, None]).astype(jnp.int32), axis=1)
    grp = tbin // len(PAIR_LO)
    pair = tbin % len(PAIR_LO)
    elo = grp * EXP_PER_GROUP + jnp.asarray(PAIR_LO, jnp.int32)[pair]
    ehi = grp * EXP_PER_GROUP + jnp.asarray(PAIR_HI, jnp.int32)[pair]
    valid = (tile <= last_tile).astype(jnp.int32)
    return slots, elo, ehi, grp, valid


def _moe_experts(xs_sorted, elo, ehi, grp, valid, wr, br, wg, wu, wd):
    n_tiles = xs_sorted.shape[0] // (MOE_TILE * ROW_SUB)
    rows = pl.BlockSpec((MOE_TILE * ROW_SUB, LANES), lambda i, *_: (i, 0))
    lo3 = lambda i, elo, ehi, grp, valid: (elo[i], 0, 0)
    hi3 = lambda i, elo, ehi, grp, valid: (ehi[i], 0, 0)
    const2 = lambda i, *_: (0, 0)
    up_shape, down_shape = (1, D_MODEL, D_EXPERT), (1, D_EXPERT, D_MODEL)
    return pl.pallas_call(
        _moe_kernel,
        grid_spec=pltpu.PrefetchScalarGridSpec(
            num_scalar_prefetch=4,
            grid=(n_tiles,),
            in_specs=[
                rows,
                pl.BlockSpec((D_MODEL, LANES), const2),
                pl.BlockSpec((1, LANES), const2),
                pl.BlockSpec(up_shape, lo3), pl.BlockSpec(up_shape, lo3), pl.BlockSpec(down_shape, lo3),
                pl.BlockSpec(up_shape, hi3), pl.BlockSpec(up_shape, hi3), pl.BlockSpec(down_shape, hi3),
            ],
            out_specs=rows,
        ),
        out_shape=jax.ShapeDtypeStruct(xs_sorted.shape, F32),
        compiler_params=_params(("arbitrary",)),
        name="moe_experts",
    )(elo, ehi, grp, valid, xs_sorted, wr, br, wg, wu, wd, wg, wu, wd)


def _moe(xs, route, counts, wr, br, wg, wu, wd):
    t = route.shape[0]
    n_sorted = t + N_BINS * MOE_TILE
    slots, elo, ehi, grp, valid = _moe_plan(route, counts, n_sorted // MOE_TILE)
    xs_sorted = _move_rows(slots, xs, n_sorted, scatter=True)
    ys_sorted = _moe_experts(xs_sorted, elo, ehi, grp, valid, wr, br, wg, wu, wd)
    return _move_rows(slots, ys_sorted, t, scatter=False)


def _final_kernel(h_ref, y_ref, g_ref, out_ref, *, tm):
    h = h_ref[...] + _load_rows(y_ref, tm)
    out_ref[...] = h * _rms_scale(h) * g_ref[...]


def _final(h3, y, g, tm):
    t = h3.shape[0]
    row = lambda i: (i, 0)
    return pl.pallas_call(
        functools.partial(_final_kernel, tm=tm),
        grid=(t // tm,),
        in_specs=[
            pl.BlockSpec((tm, D_MODEL), row),
            pl.BlockSpec((tm * ROW_SUB, LANES), row),
            pl.BlockSpec((1, D_MODEL), lambda i: (0, 0)),
        ],
        out_specs=pl.BlockSpec((tm, D_MODEL), row),
        out_shape=jax.ShapeDtypeStruct((t, D_MODEL), F32),
        compiler_params=_params(("parallel",)),
        name="final_norm",
    )(h3, y, g)


def _head_rms(x, gsum_ref, gexp_ref):
    s_hi, s_lo = _split2(x * x)
    ms = _dot(s_hi, gsum_ref[...]) + _dot(s_lo, gsum_ref[...])
    r_hi, r_lo = _split2(lax.rsqrt(ms + EPS))
    return _dot(r_hi, gexp_ref[...]) + _dot(r_lo, gexp_ref[...])


def _proj_kvq_kernel(h_ref, ymoe_ref, gkv_ref, gq_ref, wk_ref, wvt_ref, wf_ref, bf_ref, wq_ref, wogt_ref,
                     kgain_ref, qgain_ref, gsum_ref, gexp_ref, sel_ref,
                     h2_ref, kaug_ref, vt_ref, qaug_ref, sgt_ref, cstat_ref, carry_ref, *, tm):
    @pl.when(pl.program_id(1) == 0)
    def _():
        carry_ref[...] = jnp.zeros_like(carry_ref)

    hres = h_ref[...] + _load_rows(ymoe_ref, tm)
    h2_ref[...] = hres
    y = hres * _rms_scale(hres)
    a = y * gkv_ref[...]
    ah, al = _split2(a)
    bh = (y * gq_ref[...]).astype(BF16)

    logf = _log_sigmoid(_dot_x3(ah, al, wf_ref) + bf_ref[...])
    ti = lax.broadcasted_iota(jnp.int32, (tm, tm), 0)
    tj = lax.broadcasted_iota(jnp.int32, (tm, tm), 1)
    tril = jnp.where(tj <= ti, 1.0, 0.0).astype(BF16)
    c = _dot_exact_lhs(tril, logf) + carry_ref[...]
    carry_ref[...] = c[tm - 1:tm, :]

    nb = c * (-LOG2E)
    for blk in range(tm // ATT_BLK):
        nb_blk = nb[blk * ATT_BLK:(blk + 1) * ATT_BLK, :]
        cstat_ref[0, blk, 0:1, :] = jnp.max(nb_blk, axis=0, keepdims=True)
        cstat_ref[0, blk, 1:2, :] = jnp.min(nb_blk, axis=0, keepdims=True)
    n0, n1, n2 = _split3(nb)
    lane = lax.broadcasted_iota(jnp.int32, (tm, LANES), 1)
    packed = jnp.where(lane < N_HEADS_B, n0.astype(F32),
                       jnp.where(lane < 2 * N_HEADS_B, pltpu.roll(n1.astype(F32), N_HEADS_B, 1),
                                 pltpu.roll(n2.astype(F32), 2 * N_HEADS_B, 1)))
    packed = jnp.where(lane < 3 * N_HEADS_B, packed, 0.0).astype(BF16)
    extras = _dot(packed, sel_ref[...])

    k = _dot(ah, wk_ref[...])
    kn = k * _head_rms(k, gsum_ref, gexp_ref) * kgain_ref[...]
    q = _dot(bh, wq_ref[...])
    qn = q * _head_rms(q, gsum_ref, gexp_ref) * qgain_ref[...]
    ones = jnp.where((lane >= BIAS_LANE) & (lane < BIAS_LANE + 3), 1.0, 0.0)
    for j in range(N_HEADS_B // 2):
        kj = kn[:, j * LANES:(j + 1) * LANES]
        qj = qn[:, j * LANES:(j + 1) * LANES]
        for half, (kk, qq) in enumerate(((kj, qj), (pltpu.roll(kj, DH_B, 1), pltpu.roll(qj, DH_B, 1)))):
            hd = 2 * j + half
            ex = extras[:, hd * LANES:(hd + 1) * LANES]
            kaug_ref[0, hd] = jnp.where(lane < DH_B, kk, ex).astype(BF16)
            qaug_ref[0, hd] = jnp.where(lane < DH_B, qq, ones).astype(BF16)

    vt = _dot_nt(wvt_ref[...], ah)
    sgt = _sigmoid(_dot_nt(wogt_ref[...], bh))
    for hd in range(N_HEADS_B):
        vt_ref[0, hd] = vt[hd * DH_B:(hd + 1) * DH_B, :].astype(BF16)
    sgt_ref[0] = sgt


def _proj_kvq(h1, ymoe, gkv, gq, wk, wvt, wf, bf, wq, wogt, kgain, qgain, gsum, gexp, sel, bsz, seq, tm):
    ns = seq // tm
    row = lambda b, s: (b * ns + s, 0)
    const2 = lambda b, s: (0, 0)
    const3 = lambda b, s: (0, 0, 0)
    return pl.pallas_call(
        functools.partial(_proj_kvq_kernel, tm=tm),
        grid=(bsz, ns),
        in_specs=[
            pl.BlockSpec((tm, D_MODEL), row),
            pl.BlockSpec((tm * ROW_SUB, LANES), row),
            pl.BlockSpec((1, D_MODEL), const2),
            pl.BlockSpec((1, D_MODEL), const2),
            pl.BlockSpec((D_MODEL, D_MODEL), const2),
            pl.BlockSpec((D_MODEL, D_MODEL), const2),
            pl.BlockSpec((2, D_MODEL, LANES), const3),
            pl.BlockSpec((1, LANES), const2),
            pl.BlockSpec((D_MODEL, D_MODEL), const2),
            pl.BlockSpec((D_MODEL, D_MODEL), const2),
            pl.BlockSpec((1, D_MODEL), const2),
            pl.BlockSpec((1, D_MODEL), const2),
            pl.BlockSpec((D_MODEL, LANES), const2),
            pl.BlockSpec((LANES, D_MODEL), const2),
            pl.BlockSpec((LANES, N_HEADS_B * LANES), const2),
        ],
        out_specs=[
            pl.BlockSpec((tm, D_MODEL), row),
            pl.BlockSpec((1, N_HEADS_B, tm, LANES), lambda b, s: (b, 0, s, 0)),
            pl.BlockSpec((1, N_HEADS_B, DH_B, tm), lambda b, s: (b, 0, 0, s)),
            pl.BlockSpec((1, N_HEADS_B, tm, LANES), lambda b, s: (b, 0, s, 0)),
            pl.BlockSpec((1, D_MODEL, tm), lambda b, s: (b, 0, s)),
            pl.BlockSpec((1, tm // ATT_BLK, 2, LANES), lambda b, s: (b, s, 0, 0)),
        ],
        out_shape=[
            jax.ShapeDtypeStruct((bsz * seq, D_MODEL), F32),
            jax.ShapeDtypeStruct((bsz, N_HEADS_B, seq, LANES), BF16),
            jax.ShapeDtypeStruct((bsz, N_HEADS_B, DH_B, seq), BF16),
            jax.ShapeDtypeStruct((bsz, N_HEADS_B, seq, LANES), BF16),
            jax.ShapeDtypeStruct((bsz, D_MODEL, seq), F32),
            jax.ShapeDtypeStruct((bsz, seq // ATT_BLK, 2, LANES), F32),
        ],
        scratch_shapes=[pltpu.VMEM((1, LANES), F32)],
        compiler_params=_params(("arbitrary", "arbitrary")),
        name="proj_kvq",
    )(h1, ymoe, gkv, gq, wk, wvt, wf, bf, wq, wogt, kgain, qgain, gsum, gexp, sel)


def _attn_kernel(trips_ref, q_ref, k_ref, vt_ref, o_ref, m_scr, l_scr, acc_scr, sa_scr, sb_scr, xa_scr, xb_scr,
                 p_scr, *, chains):
    b, h, i = pl.program_id(0), pl.program_id(1), pl.program_id(2)
    trips = trips_ref[(b * pl.num_programs(1) + h) * pl.num_programs(2) + i]
    blk = ATT_BLK
    lane = lax.broadcasted_iota(jnp.int32, (blk, LANES), 1)
    k_null = jnp.where(lane == BIAS_LANE, NEG_BIG, 0.0).astype(BF16)

    def key_offset(g, t):
        kb = i * chains + g - t
        return kb, pl.multiple_of(jnp.maximum(kb, 0) * blk, blk)

    def scores(t, s_ref, x_ref, first):
        for g in range(chains):
            q = q_ref[0, 0, g * blk:(g + 1) * blk, :]
            kb, off = key_offset(g, t)
            kblk = k_ref[0, 0, pl.ds(off, blk), :]
            if not first:
                kblk = jnp.where(kb >= 0, kblk, k_null)
            s = _dot_nt(kblk, q)
            if first:
                ki = lax.broadcasted_iota(jnp.int32, (blk, blk), 0)
                qj = lax.broadcasted_iota(jnp.int32, (blk, blk), 1)
                s = jnp.where(ki <= qj, s, NEG_BIG)
            s_ref[g] = s
            x_ref[g] = jnp.max(s, axis=0, keepdims=True)

    def update(t, s_ref, x_ref):
        for g in range(chains):
            _, off = key_offset(g, t)
            vblk = vt_ref[0, 0, :, pl.ds(off, blk)]
            m_old = m_scr[g]
            m_new = jnp.maximum(m_old, x_ref[g])
            p = jnp.exp2(s_ref[g] - m_new)
            alpha = jnp.exp2(m_old - m_new)
            l_scr[g] = alpha * l_scr[g] + jnp.sum(p, axis=0, keepdims=True)
            p_scr[g] = p.astype(BF16)
            acc_scr[g] = alpha * acc_scr[g] + _dot(vblk, p_scr[g])
            m_scr[g] = m_new

    m_scr[...] = jnp.full(m_scr.shape, NEG_BIG, F32)
    l_scr[...] = jnp.zeros(l_scr.shape, F32)
    acc_scr[...] = jnp.zeros(acc_scr.shape, F32)
    scores(0, sa_scr, xa_scr, True)
    steps = trips - 1

    def body(pair, carry):
        t = 2 * pair
        scores(t + 1, sb_scr, xb_scr, False)
        update(t, sa_scr, xa_scr)
        scores(t + 2, sa_scr, xa_scr, False)
        update(t + 1, sb_scr, xb_scr)
        return carry

    lax.fori_loop(0, steps >> 1, body, 0)

    @pl.when((steps & 1) == 1)
    def _():
        scores(steps, sb_scr, xb_scr, False)
        update(steps - 1, sa_scr, xa_scr)
        update(steps, sb_scr, xb_scr)

    @pl.when((steps & 1) == 0)
    def _():
        update(steps, sa_scr, xa_scr)

    for g in range(chains):
        o_ref[0, 0, :, g * blk:(g + 1) * blk] = acc_scr[g] / l_scr[g]


def _attn_trips(cstat, kgain, qgain, chains):
    nbmax = jnp.swapaxes(cstat[:, :, 0, :N_HEADS_B], 1, 2)
    nbmin = jnp.swapaxes(cstat[:, :, 1, :N_HEADS_B], 1, 2)
    nblk = nbmax.shape[-1]
    xb = DH_B * jnp.max(jnp.abs(kgain)) * jnp.max(jnp.abs(qgain))
    slack = 2.0 * xb * 1.02 + 2.0 + SKIP_LOG2
    j = jnp.arange(nblk)
    keep = (nbmax[:, :, None, :] - nbmin[:, :, :, None] + slack >= 0.0) & (j[None, :] <= j[:, None])
    jmin = jnp.min(jnp.where(keep, j[None, :], nblk), axis=-1)
    need = j - jnp.minimum(jmin, j) + 1
    return jnp.max(need.reshape(need.shape[0], need.shape[1], nblk // chains, chains), axis=-1).astype(jnp.int32)


def _attn(qaug, kaug, vt, trips, chains):
    bsz, nh, seq, _ = qaug.shape
    tq = chains * ATT_BLK
    grid_spec = pltpu.PrefetchScalarGridSpec(
        num_scalar_prefetch=1,
        grid=(bsz, nh, seq // tq),
        in_specs=[
            pl.BlockSpec((1, 1, tq, LANES), lambda b, h, i, tr: (b, h, i, 0)),
            pl.BlockSpec((1, 1, seq, LANES), lambda b, h, i, tr: (b, h, 0, 0)),
            pl.BlockSpec((1, 1, DH_B, seq), lambda b, h, i, tr: (b, h, 0, 0)),
        ],
        out_specs=pl.BlockSpec((1, 1, DH_B, tq), lambda b, h, i, tr: (b, h, 0, i)),
        scratch_shapes=[
            pltpu.VMEM((chains, 1, ATT_BLK), F32),
            pltpu.VMEM((chains, 1, ATT_BLK), F32),
            pltpu.VMEM((chains, DH_B, ATT_BLK), F32),
            pltpu.VMEM((chains, ATT_BLK, ATT_BLK), F32),
            pltpu.VMEM((chains, ATT_BLK, ATT_BLK), F32),
            pltpu.VMEM((chains, 1, ATT_BLK), F32),
            pltpu.VMEM((chains, 1, ATT_BLK), F32),
            pltpu.VMEM((chains, ATT_BLK, ATT_BLK), BF16),
        ],
    )
    return pl.pallas_call(
        functools.partial(_attn_kernel, chains=chains),
        grid_spec=grid_spec,
        out_shape=jax.ShapeDtypeStruct((bsz, nh, DH_B, seq), F32),
        compiler_params=_params(("parallel", "parallel", "arbitrary")),
        name="attn",
    )(trips.reshape(-1), qaug, kaug, vt)


def _out_b_kernel(ot_ref, sgt_ref, h_ref, wout_ref, gn_ref, wr_ref, br_ref,
                  h3_ref, xs_ref, route_ref, counts_ref, *, tm):
    zt = (ot_ref[0] * sgt_ref[0]).astype(BF16)
    h3 = h_ref[...] + _dot_tn(zt, wout_ref[...])
    h3_ref[...] = h3
    first_step = (pl.program_id(0) == 0) & (pl.program_id(1) == 0)
    _ffn_norm_and_route(h3, gn_ref, wr_ref, br_ref, xs_ref, route_ref, counts_ref, tm, first_step)


def _out_b(ot, sgt, h2, wout, gn, wr, br, bsz, seq, tm):
    ns = seq // tm
    row = lambda b, s: (b * ns + s, 0)
    const2 = lambda b, s: (0, 0)
    const3 = lambda b, s: (0, 0, 0)
    t = bsz * seq
    return pl.pallas_call(
        functools.partial(_out_b_kernel, tm=tm),
        grid=(bsz, ns),
        in_specs=[
            pl.BlockSpec((1, D_MODEL, tm), lambda b, s: (b, 0, s)),
            pl.BlockSpec((1, D_MODEL, tm), lambda b, s: (b, 0, s)),
            pl.BlockSpec((tm, D_MODEL), row),
            pl.BlockSpec((D_MODEL, D_MODEL), const2),
            pl.BlockSpec((1, D_MODEL), const2),
            pl.BlockSpec((2, D_MODEL, LANES), const3),
            pl.BlockSpec((1, LANES), const2),
        ],
        out_specs=_route_out_specs(tm, row, const2),
        out_shape=_route_out_shapes(t),
        compiler_params=_params(("arbitrary", "arbitrary")),
        name="out_b",
    )(ot, sgt, h2, wout, gn, wr, br)


def _hi_lo(w):
    hi = w.astype(BF16)
    lo = (w - hi.astype(F32)).astype(BF16)
    return jnp.stack([hi, lo])


def _pad_lanes(w, width=LANES):
    return jnp.pad(w, ((0, 0),) * (w.ndim - 1) + ((0, width - w.shape[-1]),))


def _router_params(w_grp, b_grp, w_exp, b_exp):
    w = _pad_lanes(jnp.concatenate([w_grp, w_exp], axis=-1))
    b = _pad_lanes(jnp.concatenate([b_grp, b_exp], axis=-1)[None, :])
    return _hi_lo(w), b


def _tile(seq, pref):
    t = pref
    while seq % t:
        t //= 2
    return t


def kernel(x, norm_mix, norm_ffn, a_w_in, a_b_gate, a_head_gain, a_w_out, kv_norm, kv_w, kv_b_f, kv_k_gain,
           b_w_in, b_q_gain, b_w_out, moe_w_grp, moe_b_grp, moe_w_exp, moe_b_exp, moe_w_gate, moe_w_up,
           moe_w_down, norm_final):
    bsz, seq, _ = x.shape
    t = bsz * seq
    assert seq % CHUNK == 0 and seq % ATT_BLK == 0
    tm = _tile(seq, 512)
    assert tm % ATT_BLK == 0
    x2d = x.reshape(t, D_MODEL)

    w_in = a_w_in[0]
    wq = w_in[:, :QK_A].astype(BF16)
    wk = w_in[:, QK_A:2 * QK_A].astype(BF16)
    wv = w_in[:, 2 * QK_A:2 * QK_A + D_MODEL].astype(BF16)
    wo = w_in[:, 2 * QK_A + D_MODEL:2 * QK_A + 2 * D_MODEL].astype(BF16)
    wgate = w_in[:, 2 * QK_A + 2 * D_MODEL:]
    wgc = _hi_lo(_pad_lanes(wgate))
    wgr = _hi_lo(wgate.T)
    bgc = _pad_lanes(a_b_gate[0][None, :])
    bgr = a_b_gate[0][:, None]
    q, k, v, o, gc, gr = _proj_a(x2d, norm_mix[0][None, :], wq, wk, wv, wo, wgc, wgr, bgc, bgr, tm)
    hn = _mlstm(q, k, v, gc, gr, bsz, seq)
    wr0, br0 = _router_params(moe_w_grp[0], moe_b_grp[0], moe_w_exp[0], moe_b_exp[0])
    h1, xs1, route1, counts1 = _out_a(hn, o, x2d, a_head_gain[0][None, :], a_w_out[0].astype(BF16),
                                      norm_ffn[0][None, :], wr0, br0, tm)
    y1 = _moe(xs1, route1, counts1, wr0[0], br0, moe_w_gate[0].astype(BF16), moe_w_up[0].astype(BF16),
              moe_w_down[0].astype(BF16))

    wkk = kv_w[:, :D_MODEL].astype(BF16)
    wvt = kv_w[:, D_MODEL:2 * D_MODEL].T.astype(BF16)
    wf = _hi_lo(_pad_lanes(kv_w[:, 2 * D_MODEL:]))
    bf = _pad_lanes(kv_b_f[None, :])
    wq1 = b_w_in[0][:, :D_MODEL].astype(BF16)
    wogt = b_w_in[0][:, D_MODEL:].T.astype(BF16)
    kgain = jnp.tile(kv_k_gain, N_HEADS_B)[None, :]
    qgain = jnp.tile(b_q_gain[0], N_HEADS_B)[None, :] * (DH_B ** -0.5 * LOG2E)
    head_of = jnp.arange(D_MODEL) // DH_B
    gsum = (head_of[:, None] == jnp.arange(LANES)[None, :]).astype(BF16) * (1.0 / DH_B)
    gexp = (jnp.arange(LANES)[:, None] == head_of[None, :]).astype(BF16)
    src = jnp.arange(LANES)[:, None]
    dst = jnp.arange(N_HEADS_B * LANES)[None, :]
    sel = ((src < 3 * N_HEADS_B) & (dst == (src % N_HEADS_B) * LANES + BIAS_LANE + src // N_HEADS_B)).astype(BF16)
    h2, kaug, vt, qaug, sgt, cstat = _proj_kvq(h1, y1, kv_norm[None, :], norm_mix[1][None, :], wkk, wvt, wf, bf,
                                               wq1, wogt, kgain, qgain, gsum, gexp, sel, bsz, seq, tm)
    chains = math.gcd(ATT_CHAINS, seq // ATT_BLK)
    ot = _attn(qaug, kaug, vt, _attn_trips(cstat, kgain, qgain, chains), chains)
    wr1, br1 = _router_params(moe_w_grp[1], moe_b_grp[1], moe_w_exp[1], moe_b_exp[1])
    h3, xs3, route3, counts3 = _out_b(ot.reshape(bsz, D_MODEL, seq), sgt, h2, b_w_out[0].astype(BF16),
                                      norm_ffn[1][None, :], wr1, br1, bsz, seq, tm)
    y3 = _moe(xs3, route3, counts3, wr1[0], br1, moe_w_gate[1].astype(BF16), moe_w_up[1].astype(BF16),
              moe_w_down[1].astype(BF16))
    out = _final(h3, y3, norm_final[None, :], tm)
    return out.reshape(bsz, seq, D_MODEL)
```

```python
import functools
import math

import jax
import jax.numpy as jnp
from jax import lax
from jax.experimental import pallas as pl
from jax.experimental.pallas import tpu as pltpu

F32 = jnp.float32
BF16 = jnp.bfloat16

D_MODEL = 1024
EPS = 1e-6

N_HEADS_A = 4
DV_A = D_MODEL // N_HEADS_A
DQK_A = DV_A // 2
QK_A = N_HEADS_A * DQK_A
GATE_CAP = 15.0
CHUNK = 128
CHUNK_SHIFT = CHUNK.bit_length() - 1

DH_B = 64
N_HEADS_B = D_MODEL // DH_B
LOG2E = 1.4426950408889634
BIAS_LANE = DH_B
NEG_BIG = -1e30
ATT_BLK = 256
ATT_CHAINS = 8
SKIP_LOG2 = 160.0
DEN_ROWS = 16

N_GROUPS = 4
EXP_PER_GROUP = 4
N_EXPERTS = N_GROUPS * EXP_PER_GROUP
D_EXPERT = D_MODEL // 2
ROUTE_LANE0 = N_GROUPS
PAIR_LO = (0, 0, 0, 1, 1, 2)
PAIR_HI = (1, 2, 3, 2, 3, 3)
N_BINS = N_GROUPS * len(PAIR_LO)
MOE_TILE = 256
ROW_SUB = 8
ROWS_PER_STEP = 1024

LANES = 128
VMEM_LIMIT = 56 * 1024 * 1024


def _params(sem, vmem=VMEM_LIMIT):
    return pltpu.CompilerParams(dimension_semantics=sem, vmem_limit_bytes=vmem)


def _dot(a, b):
    return jnp.dot(a, b, preferred_element_type=F32)


def _dot_nt(a, b):
    return lax.dot_general(a, b, (((1,), (1,)), ((), ())), preferred_element_type=F32)


def _dot_tn(a, b):
    return lax.dot_general(a, b, (((0,), (0,)), ((), ())), preferred_element_type=F32)


def _split2(x):
    hi = x.astype(BF16)
    lo = (x - hi.astype(F32)).astype(BF16)
    return hi, lo


def _split3(x):
    hi = x.astype(BF16)
    r = x - hi.astype(F32)
    mid = r.astype(BF16)
    lo = (r - mid.astype(F32)).astype(BF16)
    return hi, mid, lo


def _dot_x3(x_hi, x_lo, w_ref):
    return _dot(x_hi, w_ref[0]) + _dot(x_lo, w_ref[0]) + _dot(x_hi, w_ref[1])


def _dot_nt_x3(w_ref, x_hi, x_lo):
    return _dot_nt(w_ref[0], x_hi) + _dot_nt(w_ref[0], x_lo) + _dot_nt(w_ref[1], x_hi)


def _dot_exact_rhs(a, b_exact):
    a0, a1, a2 = _split3(a)
    return _dot(a0, b_exact) + _dot(a1, b_exact) + _dot(a2, b_exact)


def _dot_exact_lhs(a_exact, b):
    b0, b1, b2 = _split3(b)
    return _dot(a_exact, b0) + _dot(a_exact, b1) + _dot(a_exact, b2)


def _rms_scale(x):
    return lax.rsqrt(jnp.mean(x * x, axis=-1, keepdims=True) + EPS)


def _log_sigmoid(z):
    return jnp.minimum(z, 0.0) - jnp.log(1.0 + jnp.exp(-jnp.abs(z)))


def _sigmoid(z):
    return 1.0 / (1.0 + jnp.exp(-z))


def _softcap(z):
    return GATE_CAP * jnp.tanh(z / GATE_CAP)


def _proj_a_kernel(x_ref, g_ref, wq_ref, wk_ref, wv_ref, wo_ref, wgc_ref, wgr_ref, bgc_ref, bgr_ref,
                   q_ref, k_ref, v_ref, o_ref, gc_ref, gr_ref, *, tm):
    x = x_ref[...]
    xn = x * _rms_scale(x) * g_ref[...]
    xh, xl = _split2(xn)
    q_ref[...] = _dot(xh, wq_ref[...]).astype(BF16)
    k_ref[...] = (_dot(xh, wk_ref[...]) * (DQK_A ** -0.5)).astype(BF16)
    v_ref[...] = _dot(xh, wv_ref[...]).astype(BF16)
    o_ref[...] = _dot(xh, wo_ref[...])

    zc = _softcap(_dot_x3(xh, xl, wgc_ref) + bgc_ref[...])
    zr = _softcap(_dot_nt_x3(wgr_ref, xh, xl) + bgr_ref[...])
    lane = lax.broadcasted_iota(jnp.int32, zc.shape, 1)
    sub = lax.broadcasted_iota(jnp.int32, zr.shape, 0)
    vc = jnp.where(lane < N_HEADS_A, zc, _log_sigmoid(zc))
    vr = jnp.where(sub < N_HEADS_A, zr, _log_sigmoid(zr))
    ti = lax.broadcasted_iota(jnp.int32, (tm, tm), 0)
    tj = lax.broadcasted_iota(jnp.int32, (tm, tm), 1)
    same = (ti >> CHUNK_SHIFT) == (tj >> CHUNK_SHIFT)
    tril = jnp.where(same & (tj <= ti), 1.0, 0.0).astype(BF16)
    triu = jnp.where(same & (ti <= tj), 1.0, 0.0).astype(BF16)
    cc = _dot_exact_lhs(tril, vc)
    cr = _dot_exact_rhs(vr, triu)
    gc = jnp.where(lane < N_HEADS_A, vc, cc)
    gr = jnp.where(sub < N_HEADS_A, vr, cr)
    a_rows = gr[:N_HEADS_A, :] - gr[N_HEADS_A:, :]
    ci = lax.broadcasted_iota(jnp.int32, (CHUNK, CHUNK), 0)
    cj = lax.broadcasted_iota(jnp.int32, (CHUNK, CHUNK), 1)
    for h in range(N_HEADS_A):
        col = jnp.concatenate(
            [jnp.max(jnp.where(cj <= ci, a_rows[h:h + 1, c * CHUNK:(c + 1) * CHUNK], -jnp.inf), axis=-1, keepdims=True)
             for c in range(tm // CHUNK)], axis=0)
        gc = jnp.where(lane == 2 * N_HEADS_A + h, col, gc)
    gc_ref[...] = gc
    gr_ref[...] = gr


def _proj_a(x2d, g, wq, wk, wv, wo, wgc, wgr, bgc, bgr, tm):
    t = x2d.shape[0]
    row = lambda i: (i, 0)
    const2 = lambda i: (0, 0)
    const3 = lambda i: (0, 0, 0)
    return pl.pallas_call(
        functools.partial(_proj_a_kernel, tm=tm),
        grid=(t // tm,),
        in_specs=[
            pl.BlockSpec((tm, D_MODEL), row),
            pl.BlockSpec((1, D_MODEL), const2),
            pl.BlockSpec((D_MODEL, QK_A), const2),
            pl.BlockSpec((D_MODEL, QK_A), const2),
            pl.BlockSpec((D_MODEL, D_MODEL), const2),
            pl.BlockSpec((D_MODEL, D_MODEL), const2),
            pl.BlockSpec((2, D_MODEL, LANES), const3),
            pl.BlockSpec((2, 8, D_MODEL), const3),
            pl.BlockSpec((1, LANES), const2),
            pl.BlockSpec((8, 1), const2),
        ],
        out_specs=[
            pl.BlockSpec((tm, QK_A), row),
            pl.BlockSpec((tm, QK_A), row),
            pl.BlockSpec((tm, D_MODEL), row),
            pl.BlockSpec((tm, D_MODEL), row),
            pl.BlockSpec((tm, LANES), row),
            pl.BlockSpec((8, tm), lambda i: (0, i)),
        ],
        out_shape=[
            jax.ShapeDtypeStruct((t, QK_A), BF16),
            jax.ShapeDtypeStruct((t, QK_A), BF16),
            jax.ShapeDtypeStruct((t, D_MODEL), BF16),
            jax.ShapeDtypeStruct((t, D_MODEL), F32),
            jax.ShapeDtypeStruct((t, LANES), F32),
            jax.ShapeDtypeStruct((8, t), F32),
        ],
        compiler_params=_params(("parallel",)),
        name="proj_a",
    )(x2d, g, wq, wk, wv, wo, wgc, wgr, bgc, bgr)


def _mlstm_kernel(q_ref, k_ref, v_ref, gc_ref, gr_ref, h_ref, c_scr, m_scr):
    @pl.when(pl.program_id(1) == 0)
    def _():
        c_scr[...] = jnp.zeros_like(c_scr)
        m_scr[...] = jnp.zeros_like(m_scr)

    ti = lax.broadcasted_iota(jnp.int32, (CHUNK, CHUNK), 0)
    si = lax.broadcasted_iota(jnp.int32, (CHUNK, CHUNK), 1)
    causal = si <= ti
    gc = gc_ref[...]
    gr = gr_ref[...]
    heads = range(N_HEADS_A)
    ones_blk = jnp.ones((CHUNK, LANES), BF16)
    ones_sq = jnp.ones((DV_A, LANES), BF16)
    qs = [q_ref[:, h * DQK_A:(h + 1) * DQK_A] for h in heads]
    ks = [k_ref[:, h * DQK_A:(h + 1) * DQK_A] for h in heads]
    vs = [jnp.concatenate([v_ref[:, h * DV_A:(h + 1) * DV_A], ones_blk], axis=1) for h in heads]

    gates = []
    for h in heads:
        b_c = gc[:, N_HEADS_A + h:N_HEADS_A + h + 1]
        a_c = gc[:, h:h + 1] - b_c
        amax_c = gc[:, 2 * N_HEADS_A + h:2 * N_HEADS_A + h + 1]
        a_r = gr[h:h + 1, :] - gr[N_HEADS_A + h:N_HEADS_A + h + 1, :]
        b_end = b_c[CHUNK - 1:CHUNK, :]
        m_run = m_scr[h][:, 0:1]
        mx_c = jnp.maximum(m_run, amax_c)
        mx_end = mx_c[CHUNK - 1:CHUNK, :]
        d_mat = jnp.where(causal, jnp.exp(a_r - mx_c), 0.0)
        w_inter = jnp.broadcast_to(jnp.exp(m_run - mx_c), (CHUNK, LANES))
        floor = jnp.broadcast_to(jnp.exp(-(b_c + mx_c)), (CHUNK, LANES))
        w_c = jnp.exp(a_c - mx_end)
        decay = jnp.exp(m_run - mx_end)
        gates.append((d_mat, w_inter, floor, b_end + mx_end, w_c, decay))

    kws = [(ks[h].astype(F32) * gates[h][4]).astype(BF16) for h in heads]
    s_raw = [_dot_nt(qs[h], ks[h]) for h in heads]
    inter = [_dot(qs[h], c_scr[h].astype(BF16)) for h in heads]
    c_upd = [_dot_tn(kws[h], vs[h]) for h in heads]
    for h in heads:
        d_mat, w_inter, floor, m_new, w_c, decay = gates[h]
        tot = _dot((s_raw[h] * d_mat).astype(BF16), vs[h])
        den = tot[:, DV_A:] + w_inter * inter[h][:, DV_A:]
        rden = 1.0 / jnp.maximum(jnp.abs(den), floor)
        halves = [(tot[:, j * LANES:(j + 1) * LANES] + w_inter * inter[h][:, j * LANES:(j + 1) * LANES]) * rden
                  for j in range(DV_A // LANES)]
        hh = jnp.concatenate(halves, axis=1)
        ms = _dot((hh * hh).astype(BF16), ones_sq) * (1.0 / DV_A)
        rs = lax.rsqrt(ms + EPS)
        for j in range(DV_A // LANES):
            h_ref[:, h * DV_A + j * LANES:h * DV_A + (j + 1) * LANES] = halves[j] * rs
        c_scr[h] = decay * c_scr[h] + c_upd[h]
        m_scr[h] = jnp.broadcast_to(m_new, (1, LANES))


def _mlstm(q, k, v, gc, gr, bsz, seq):
    nc = seq // CHUNK
    row = lambda b, c: (b * nc + c, 0)
    return pl.pallas_call(
        _mlstm_kernel,
        grid=(bsz, nc),
        in_specs=[
            pl.BlockSpec((CHUNK, QK_A), row),
            pl.BlockSpec((CHUNK, QK_A), row),
            pl.BlockSpec((CHUNK, D_MODEL), row),
            pl.BlockSpec((CHUNK, LANES), row),
            pl.BlockSpec((8, CHUNK), lambda b, c: (0, b * nc + c)),
        ],
        out_specs=pl.BlockSpec((CHUNK, D_MODEL), row),
        out_shape=jax.ShapeDtypeStruct((bsz * seq, D_MODEL), F32),
        scratch_shapes=[
            pltpu.VMEM((N_HEADS_A, DQK_A, DV_A + LANES), F32),
            pltpu.VMEM((N_HEADS_A, 1, LANES), F32),
        ],
        compiler_params=_params(("arbitrary", "arbitrary")),
        name="mlstm",
    )(q, k, v, gc, gr)


def _route_bins(logits):
    lane = lax.broadcasted_iota(jnp.int32, logits.shape, 1)
    big = jnp.int32(10 ** 6)

    def top(mask):
        mx = jnp.max(jnp.where(mask, logits, -jnp.inf), axis=-1, keepdims=True)
        return jnp.min(jnp.where(mask & (logits == mx), lane, big), axis=-1, keepdims=True)

    gidx = top(lane < N_GROUPS)
    lo = ROUTE_LANE0 + EXP_PER_GROUP * gidx
    emask = (lane >= lo) & (lane < lo + EXP_PER_GROUP)
    i1 = top(emask)
    i2 = top(emask & (lane != i1))
    p_lo = jnp.minimum(i1, i2) - lo
    p_hi = jnp.maximum(i1, i2) - lo
    pair = jnp.where(p_lo == 0, p_hi - 1, jnp.where(p_lo == 1, p_hi + 1, len(PAIR_LO) - 1))
    return gidx * len(PAIR_LO) + pair


def _pair_weights(logits, grp, e_lo, e_hi):
    lane = lax.broadcasted_iota(jnp.int32, logits.shape, 1)

    def pick(idx):
        return jnp.sum(jnp.where(lane == idx, logits, 0.0), axis=-1, keepdims=True)

    def softmax_stats(mask):
        mx = jnp.max(jnp.where(mask, logits, -jnp.inf), axis=-1, keepdims=True)
        return mx, jnp.sum(jnp.where(mask, jnp.exp(logits - mx), 0.0), axis=-1, keepdims=True)

    gmax, gsum = softmax_stats(lane < N_GROUPS)
    g_p = jnp.exp(pick(grp) - gmax) / gsum
    lo = ROUTE_LANE0 + EXP_PER_GROUP * grp
    emax, esum = softmax_stats((lane >= lo) & (lane < lo + EXP_PER_GROUP))
    p_lo = jnp.exp(pick(ROUTE_LANE0 + e_lo) - emax) / esum
    p_hi = jnp.exp(pick(ROUTE_LANE0 + e_hi) - emax) / esum
    return g_p * (p_lo / (p_lo + p_hi)), g_p * (p_hi / (p_lo + p_hi))


def _store_rows(rows_ref, x, tm):
    for c in range(ROW_SUB):
        rows_ref[pl.ds(c, tm, stride=ROW_SUB), :] = x[:, c * LANES:(c + 1) * LANES]


def _load_rows(rows_ref, tm):
    return jnp.concatenate([rows_ref[pl.ds(c, tm, stride=ROW_SUB), :] for c in range(ROW_SUB)], axis=-1)


def _ffn_norm_and_route(h, gn_ref, wr_ref, br_ref, xs_ref, route_ref, counts_ref, tm, first_step):
    @pl.when(first_step)
    def _():
        counts_ref[...] = jnp.zeros_like(counts_ref)

    xn = h * _rms_scale(h) * gn_ref[...]
    _store_rows(xs_ref, xn, tm)
    xh, xl = _split2(xn)
    bins = _route_bins(_dot_x3(xh, xl, wr_ref) + br_ref[...])
    lane = lax.broadcasted_iota(jnp.int32, (tm, LANES), 1)
    onehot = jnp.where(lane == bins, 1.0, 0.0)
    ti = lax.broadcasted_iota(jnp.int32, (tm, tm), 0)
    tj = lax.broadcasted_iota(jnp.int32, (tm, tm), 1)
    earlier = jnp.where(tj < ti, 1.0, 0.0).astype(BF16)
    before = _dot(earlier, onehot.astype(BF16))
    counts = counts_ref[...]
    rank = jnp.sum(onehot * (before + counts), axis=-1, keepdims=True)
    counts_ref[...] = counts + jnp.sum(onehot, axis=0, keepdims=True)
    rank_hi = jnp.floor(rank * (1.0 / 256.0))
    cols = jnp.where(lane == 0, bins.astype(F32),
                     jnp.where(lane == 1, rank_hi, jnp.where(lane == 2, rank - 256.0 * rank_hi, 0.0)))
    pick = jnp.where(lax.broadcasted_iota(jnp.int32, (8, LANES), 0) == lax.broadcasted_iota(jnp.int32, (8, LANES), 1),
                     1.0, 0.0).astype(BF16)
    route_ref[...] = _dot_nt(pick, cols.astype(BF16))


def _out_a_kernel(hn_ref, o_ref, x_ref, hg_ref, wout_ref, gn_ref, wr_ref, br_ref,
                  h1_ref, xs_ref, route_ref, counts_ref, *, tm):
    z = (hn_ref[...] * hg_ref[...] * _sigmoid(o_ref[...])).astype(BF16)
    h1 = x_ref[...] + _dot(z, wout_ref[...])
    h1_ref[...] = h1
    _ffn_norm_and_route(h1, gn_ref, wr_ref, br_ref, xs_ref, route_ref, counts_ref, tm, pl.program_id(0) == 0)


def _route_out_specs(tm, row, const2):
    return [
        pl.BlockSpec((tm, D_MODEL), row),
        pl.BlockSpec((tm * ROW_SUB, LANES), row),
        pl.BlockSpec((8, tm), lambda *idx: (0, row(*idx)[0])),
        pl.BlockSpec((1, LANES), const2),
    ]


def _route_out_shapes(t):
    return [
        jax.ShapeDtypeStruct((t, D_MODEL), F32),
        jax.ShapeDtypeStruct((t * ROW_SUB, LANES), F32),
        jax.ShapeDtypeStruct((8, t), F32),
        jax.ShapeDtypeStruct((1, LANES), F32),
    ]


def _out_a(hn, o, x2d, hg, wout, gn, wr, br, tm):
    t = x2d.shape[0]
    row = lambda i: (i, 0)
    const2 = lambda i: (0, 0)
    const3 = lambda i: (0, 0, 0)
    return pl.pallas_call(
        functools.partial(_out_a_kernel, tm=tm),
        grid=(t // tm,),
        in_specs=[
            pl.BlockSpec((tm, D_MODEL), row),
            pl.BlockSpec((tm, D_MODEL), row),
            pl.BlockSpec((tm, D_MODEL), row),
            pl.BlockSpec((1, D_MODEL), const2),
            pl.BlockSpec((D_MODEL, D_MODEL), const2),
            pl.BlockSpec((1, D_MODEL), const2),
            pl.BlockSpec((2, D_MODEL, LANES), const3),
            pl.BlockSpec((1, LANES), const2),
        ],
        out_specs=_route_out_specs(tm, row, const2),
        out_shape=_route_out_shapes(t),
        compiler_params=_params(("arbitrary",)),
        name="out_a",
    )(hn, o, x2d, hg, wout, gn, wr, br)


def _dispatch_kernel(slot_ref, src_ref, zeros_ref, dst_ref, sem, *, td):
    del zeros_ref

    def body(j, carry):
        row = pl.multiple_of(slot_ref[0, 0, j] * ROW_SUB, ROW_SUB)
        pltpu.make_async_copy(src_ref.at[pl.ds(pl.multiple_of(j * ROW_SUB, ROW_SUB), ROW_SUB)],
                              dst_ref.at[pl.ds(row, ROW_SUB)], sem).start()
        return carry

    lax.fori_loop(0, td, body, 0, unroll=8)
    pltpu.make_async_copy(src_ref, dst_ref.at[pl.ds(0, td * ROW_SUB)], sem).wait()


def _combine_kernel(slot_ref, src_ref, out_ref, sem, *, td):
    def body(j, carry):
        row = pl.multiple_of(slot_ref[0, 0, j] * ROW_SUB, ROW_SUB)
        pltpu.make_async_copy(src_ref.at[pl.ds(row, ROW_SUB)],
                              out_ref.at[pl.ds(pl.multiple_of(j * ROW_SUB, ROW_SUB), ROW_SUB)], sem).start()
        return carry

    lax.fori_loop(0, td, body, 0, unroll=8)
    pltpu.make_async_copy(src_ref.at[pl.ds(0, td * ROW_SUB)], out_ref, sem).wait()


def _move_rows(slots, src, n_dst_tokens, scatter):
    t = slots.shape[0]
    td = math.gcd(ROWS_PER_STEP, t)
    smem_tok = pl.BlockSpec((1, 1, td), lambda i: (i, 0, 0), memory_space=pltpu.SMEM)
    vmem_rows = pl.BlockSpec((td * ROW_SUB, LANES), lambda i: (i, 0))
    any_spec = pl.BlockSpec(memory_space=pl.ANY)
    slots3 = slots.reshape(t // td, 1, td)
    if scatter:
        operands = (slots3, src, jnp.zeros((n_dst_tokens * ROW_SUB, LANES), F32))
        body, in_specs, out_spec, aliases = _dispatch_kernel, [smem_tok, vmem_rows, any_spec], any_spec, {2: 0}
    else:
        operands = (slots3, src)
        body, in_specs, out_spec, aliases = _combine_kernel, [smem_tok, any_spec], vmem_rows, {}
    return pl.pallas_call(
        functools.partial(body, td=td),
        grid=(t // td,),
        in_specs=in_specs,
        out_specs=out_spec,
        scratch_shapes=[pltpu.SemaphoreType.DMA(())],
        out_shape=jax.ShapeDtypeStruct((n_dst_tokens * ROW_SUB, LANES), F32),
        input_output_aliases=aliases,
        compiler_params=_params(("arbitrary",)),
        name="moe_dispatch" if scatter else "moe_combine",
    )(*operands)


def _moe_kernel(elo_ref, ehi_ref, grp_ref, valid_ref, xs_ref, wr_ref, br_ref,
                wg_lo_ref, wu_lo_ref, wd_lo_ref, wg_hi_ref, wu_hi_ref, wd_hi_ref, ys_ref):
    i = pl.program_id(0)

    @pl.when(valid_ref[i] == 0)
    def _():
        ys_ref[...] = jnp.zeros_like(ys_ref)

    @pl.when(valid_ref[i] != 0)
    def _():
        x = _load_rows(xs_ref, MOE_TILE).astype(BF16)
        w_lo, w_hi = _pair_weights(_dot(x, wr_ref[...]) + br_ref[...], grp_ref[i], elo_ref[i], ehi_ref[i])
        y = None
        for wg_ref, wu_ref, wd_ref, w in ((wg_lo_ref, wu_lo_ref, wd_lo_ref, w_lo),
                                           (wg_hi_ref, wu_hi_ref, wd_hi_ref, w_hi)):
            g = _dot(x, wg_ref[0])
            u = _dot(x, wu_ref[0])
            d = _dot((g * _sigmoid(g) * u * w).astype(BF16), wd_ref[0])
            y = d if y is None else y + d
        _store_rows(ys_ref, y, MOE_TILE)


def _moe_plan(route, counts, n_tiles):
    cnt = counts[0, :N_BINS].astype(jnp.int32)
    padded = (cnt + (MOE_TILE - 1)) // MOE_TILE * MOE_TILE
    ends = jnp.cumsum(padded)
    starts = ends - padded
    rank = (route[1] * 256.0 + route[2]).astype(jnp.int32)
    slots = starts[route[0].astype(jnp.int32)] + rank
    last_tile = ends[-1] // MOE_TILE - 1
    tile = jnp.arange(n_tiles, dtype=jnp.int32)
    first_row = jnp.minimum(tile, last_tile) * MOE_TILE
    tbin = jnp.sum((ends[None, :] <= first_row[:, None]).astype(jnp.int32), axis=1)
    grp = tbin // len(PAIR_LO)
    pair = tbin % len(PAIR_LO)
    elo = grp * EXP_PER_GROUP + jnp.asarray(PAIR_LO, jnp.int32)[pair]
    ehi = grp * EXP_PER_GROUP + jnp.asarray(PAIR_HI, jnp.int32)[pair]
    valid = (tile <= last_tile).astype(jnp.int32)
    return slots, elo, ehi, grp, valid


def _moe_experts(xs_sorted, elo, ehi, grp, valid, wr, br, wg, wu, wd):
    n_tiles = xs_sorted.shape[0] // (MOE_TILE * ROW_SUB)
    rows = pl.BlockSpec((MOE_TILE * ROW_SUB, LANES), lambda i, *_: (i, 0))
    lo3 = lambda i, elo, ehi, grp, valid: (elo[i], 0, 0)
    hi3 = lambda i, elo, ehi, grp, valid: (ehi[i], 0, 0)
    const2 = lambda i, *_: (0, 0)
    up_shape, down_shape = (1, D_MODEL, D_EXPERT), (1, D_EXPERT, D_MODEL)
    return pl.pallas_call(
        _moe_kernel,
        grid_spec=pltpu.PrefetchScalarGridSpec(
            num_scalar_prefetch=4,
            grid=(n_tiles,),
            in_specs=[
                rows,
                pl.BlockSpec((D_MODEL, LANES), const2),
                pl.BlockSpec((1, LANES), const2),
                pl.BlockSpec(up_shape, lo3), pl.BlockSpec(up_shape, lo3), pl.BlockSpec(down_shape, lo3),
                pl.BlockSpec(up_shape, hi3), pl.BlockSpec(up_shape, hi3), pl.BlockSpec(down_shape, hi3),
            ],
            out_specs=rows,
        ),
        out_shape=jax.ShapeDtypeStruct(xs_sorted.shape, F32),
        compiler_params=_params(("arbitrary",)),
        name="moe_experts",
    )(elo, ehi, grp, valid, xs_sorted, wr, br, wg, wu, wd, wg, wu, wd)


def _moe(xs, route, counts, wr, br, wg, wu, wd):
    t = route.shape[1]
    n_sorted = t + N_BINS * MOE_TILE
    slots, elo, ehi, grp, valid = _moe_plan(route, counts, n_sorted // MOE_TILE)
    xs_sorted = _move_rows(slots, xs, n_sorted, scatter=True)
    ys_sorted = _moe_experts(xs_sorted, elo, ehi, grp, valid, wr, br, wg, wu, wd)
    return _move_rows(slots, ys_sorted, t, scatter=False)


def _final_kernel(h_ref, y_ref, g_ref, out_ref, *, tm):
    h = h_ref[...] + _load_rows(y_ref, tm)
    out_ref[...] = h * _rms_scale(h) * g_ref[...]


def _final(h3, y, g, tm):
    t = h3.shape[0]
    row = lambda i: (i, 0)
    return pl.pallas_call(
        functools.partial(_final_kernel, tm=tm),
        grid=(t // tm,),
        in_specs=[
            pl.BlockSpec((tm, D_MODEL), row),
            pl.BlockSpec((tm * ROW_SUB, LANES), row),
            pl.BlockSpec((1, D_MODEL), lambda i: (0, 0)),
        ],
        out_specs=pl.BlockSpec((tm, D_MODEL), row),
        out_shape=jax.ShapeDtypeStruct((t, D_MODEL), F32),
        compiler_params=_params(("parallel",)),
        name="final_norm",
    )(h3, y, g)


def _head_rms(x, gsum_ref, gexp_ref):
    s_hi, s_lo = _split2(x * x)
    ms = _dot(s_hi, gsum_ref[...]) + _dot(s_lo, gsum_ref[...])
    r_hi, r_lo = _split2(lax.rsqrt(ms + EPS))
    return _dot(r_hi, gexp_ref[...]) + _dot(r_lo, gexp_ref[...])


def _proj_kvq_kernel(h_ref, ymoe_ref, gkv_ref, gq_ref, wk_ref, wvt_ref, wf_ref, bf_ref, wq_ref, wogt_ref,
                     kgain_ref, qgain_ref, gsum_ref, gexp_ref, sel_ref,
                     h2_ref, kaug_ref, vt_ref, qaug_ref, sgt_ref, cstat_ref, carry_ref, *, tm):
    @pl.when(pl.program_id(1) == 0)
    def _():
        carry_ref[...] = jnp.zeros_like(carry_ref)

    hres = h_ref[...] + _load_rows(ymoe_ref, tm)
    h2_ref[...] = hres
    y = hres * _rms_scale(hres)
    a = y * gkv_ref[...]
    ah, al = _split2(a)
    bh = (y * gq_ref[...]).astype(BF16)

    logf = _log_sigmoid(_dot_x3(ah, al, wf_ref) + bf_ref[...])
    ti = lax.broadcasted_iota(jnp.int32, (tm, tm), 0)
    tj = lax.broadcasted_iota(jnp.int32, (tm, tm), 1)
    tril = jnp.where(tj <= ti, 1.0, 0.0).astype(BF16)
    c = _dot_exact_lhs(tril, logf) + carry_ref[...]
    carry_ref[...] = c[tm - 1:tm, :]

    nb = c * (-LOG2E)
    for blk in range(tm // ATT_BLK):
        nb_blk = nb[blk * ATT_BLK:(blk + 1) * ATT_BLK, :]
        cstat_ref[0, blk, 0:1, :] = jnp.max(nb_blk, axis=0, keepdims=True)
        cstat_ref[0, blk, 1:2, :] = jnp.min(nb_blk, axis=0, keepdims=True)
    n0, n1, n2 = _split3(nb)
    lane = lax.broadcasted_iota(jnp.int32, (tm, LANES), 1)
    packed = jnp.where(lane < N_HEADS_B, n0.astype(F32),
                       jnp.where(lane < 2 * N_HEADS_B, pltpu.roll(n1.astype(F32), N_HEADS_B, 1),
                                 pltpu.roll(n2.astype(F32), 2 * N_HEADS_B, 1)))
    packed = jnp.where(lane < 3 * N_HEADS_B, packed, 0.0).astype(BF16)
    extras = _dot(packed, sel_ref[...])

    k = _dot(ah, wk_ref[...])
    kn = k * _head_rms(k, gsum_ref, gexp_ref) * kgain_ref[...]
    q = _dot(bh, wq_ref[...])
    qn = q * _head_rms(q, gsum_ref, gexp_ref) * qgain_ref[...]
    ones = jnp.where((lane >= BIAS_LANE) & (lane < BIAS_LANE + 3), 1.0, 0.0)
    for j in range(N_HEADS_B // 2):
        kj = kn[:, j * LANES:(j + 1) * LANES]
        qj = qn[:, j * LANES:(j + 1) * LANES]
        for half, (kk, qq) in enumerate(((kj, qj), (pltpu.roll(kj, DH_B, 1), pltpu.roll(qj, DH_B, 1)))):
            hd = 2 * j + half
            ex = extras[:, hd * LANES:(hd + 1) * LANES]
            kaug_ref[0, hd] = jnp.where(lane < DH_B, kk, ex).astype(BF16)
            qaug_ref[0, hd] = jnp.where(lane < DH_B, qq, ones).astype(BF16)

    vt = _dot_nt(wvt_ref[...], ah)
    sgt = _sigmoid(_dot_nt(wogt_ref[...], bh))
    for hd in range(N_HEADS_B):
        vt_ref[0, hd] = vt[hd * DH_B:(hd + 1) * DH_B, :].astype(BF16)
    sgt_ref[0] = sgt


def _proj_kvq(h1, ymoe, gkv, gq, wk, wvt, wf, bf, wq, wogt, kgain, qgain, gsum, gexp, sel, bsz, seq, tm):
    ns = seq // tm
    row = lambda b, s: (b * ns + s, 0)
    const2 = lambda b, s: (0, 0)
    const3 = lambda b, s: (0, 0, 0)
    return pl.pallas_call(
        functools.partial(_proj_kvq_kernel, tm=tm),
        grid=(bsz, ns),
        in_specs=[
            pl.BlockSpec((tm, D_MODEL), row),
            pl.BlockSpec((tm * ROW_SUB, LANES), row),
            pl.BlockSpec((1, D_MODEL), const2),
            pl.BlockSpec((1, D_MODEL), const2),
            pl.BlockSpec((D_MODEL, D_MODEL), const2),
            pl.BlockSpec((D_MODEL, D_MODEL), const2),
            pl.BlockSpec((2, D_MODEL, LANES), const3),
            pl.BlockSpec((1, LANES), const2),
            pl.BlockSpec((D_MODEL, D_MODEL), const2),
            pl.BlockSpec((D_MODEL, D_MODEL), const2),
            pl.BlockSpec((1, D_MODEL), const2),
            pl.BlockSpec((1, D_MODEL), const2),
            pl.BlockSpec((D_MODEL, LANES), const2),
            pl.BlockSpec((LANES, D_MODEL), const2),
            pl.BlockSpec((LANES, N_HEADS_B * LANES), const2),
        ],
        out_specs=[
            pl.BlockSpec((tm, D_MODEL), row),
            pl.BlockSpec((1, N_HEADS_B, tm, LANES), lambda b, s: (b, 0, s, 0)),
            pl.BlockSpec((1, N_HEADS_B, DH_B, tm), lambda b, s: (b, 0, 0, s)),
            pl.BlockSpec((1, N_HEADS_B, tm, LANES), lambda b, s: (b, 0, s, 0)),
            pl.BlockSpec((1, D_MODEL, tm), lambda b, s: (b, 0, s)),
            pl.BlockSpec((1, tm // ATT_BLK, 2, LANES), lambda b, s: (b, s, 0, 0)),
        ],
        out_shape=[
            jax.ShapeDtypeStruct((bsz * seq, D_MODEL), F32),
            jax.ShapeDtypeStruct((bsz, N_HEADS_B, seq, LANES), BF16),
            jax.ShapeDtypeStruct((bsz, N_HEADS_B, DH_B, seq), BF16),
            jax.ShapeDtypeStruct((bsz, N_HEADS_B, seq, LANES), BF16),
            jax.ShapeDtypeStruct((bsz, D_MODEL, seq), F32),
            jax.ShapeDtypeStruct((bsz, seq // ATT_BLK, 2, LANES), F32),
        ],
        scratch_shapes=[pltpu.VMEM((1, LANES), F32)],
        compiler_params=_params(("arbitrary", "arbitrary")),
        name="proj_kvq",
    )(h1, ymoe, gkv, gq, wk, wvt, wf, bf, wq, wogt, kgain, qgain, gsum, gexp, sel)


def _attn_kernel(trips_ref, q_ref, k_ref, vt_ref, o_ref, m_scr, acc_scr, sa_scr, sb_scr, xa_scr, xb_scr,
                 p_scr, *, chains):
    b, h, i = pl.program_id(0), pl.program_id(1), pl.program_id(2)
    trips = trips_ref[(b * pl.num_programs(1) + h) * pl.num_programs(2) + i]
    blk = ATT_BLK
    lane = lax.broadcasted_iota(jnp.int32, (blk, LANES), 1)
    k_null = jnp.where(lane == BIAS_LANE, NEG_BIG, 0.0).astype(BF16)

    def key_offset(g, t):
        kb = i * chains + g - t
        return kb, pl.multiple_of(jnp.maximum(kb, 0) * blk, blk)

    all_chains = range(chains)
    ones_rows = jnp.ones((DEN_ROWS, blk), BF16)

    def scores(t, s_ref, x_ref, first, gs=all_chains):
        for g in gs:
            q = q_ref[0, 0, g * blk:(g + 1) * blk, :]
            kb, off = key_offset(g, t)
            kblk = k_ref[0, 0, pl.ds(off, blk), :]
            if not first:
                kblk = jnp.where(kb >= 0, kblk, k_null)
            s = _dot_nt(kblk, q)
            if first:
                ki = lax.broadcasted_iota(jnp.int32, (blk, blk), 0)
                qj = lax.broadcasted_iota(jnp.int32, (blk, blk), 1)
                s = jnp.where(ki <= qj, s, NEG_BIG)
            s_ref[g] = s
            x_ref[g] = jnp.max(s, axis=0, keepdims=True)

    def update(t, s_ref, x_ref, gs=all_chains):
        for g in gs:
            _, off = key_offset(g, t)
            v_den = jnp.concatenate([vt_ref[0, 0, :, pl.ds(off, blk)], ones_rows], axis=0)
            m_old = m_scr[g]
            m_new = jnp.maximum(m_old, x_ref[g])
            p_scr[g] = jnp.exp2(s_ref[g] - m_new).astype(BF16)
            acc_scr[g] = jnp.exp2(m_old - m_new) * acc_scr[g] + _dot(v_den, p_scr[g])
            m_scr[g] = m_new

    m_scr[...] = jnp.full(m_scr.shape, NEG_BIG, F32)
    acc_scr[...] = jnp.zeros(acc_scr.shape, F32)
    scores(0, sa_scr, xa_scr, True)
    steps = trips - 1

    def body(pair, carry):
        t = 2 * pair
        for g in all_chains:
            scores(t + 1, sb_scr, xb_scr, False, [g])
            update(t, sa_scr, xa_scr, [g])
        for g in all_chains:
            scores(t + 2, sa_scr, xa_scr, False, [g])
            update(t + 1, sb_scr, xb_scr, [g])
        return carry

    lax.fori_loop(0, steps >> 1, body, 0)

    @pl.when((steps & 1) == 1)
    def _():
        for g in all_chains:
            scores(steps, sb_scr, xb_scr, False, [g])
            update(steps - 1, sa_scr, xa_scr, [g])
        update(steps, sb_scr, xb_scr)

    @pl.when((steps & 1) == 0)
    def _():
        update(steps, sa_scr, xa_scr)

    for g in all_chains:
        o_ref[0, 0, :, g * blk:(g + 1) * blk] = acc_scr[g, :DH_B, :] / acc_scr[g, DH_B:DH_B + 1, :]


def _attn_trips(cstat, kgain, qgain, chains):
    nbmax = jnp.swapaxes(cstat[:, :, 0, :N_HEADS_B], 1, 2)
    nbmin = jnp.swapaxes(cstat[:, :, 1, :N_HEADS_B], 1, 2)
    nblk = nbmax.shape[-1]
    xb = DH_B * jnp.max(jnp.abs(kgain)) * jnp.max(jnp.abs(qgain))
    slack = 2.0 * xb * 1.02 + 2.0 + SKIP_LOG2
    j = jnp.arange(nblk)
    keep = (nbmax[:, :, None, :] - nbmin[:, :, :, None] + slack >= 0.0) & (j[None, :] <= j[:, None])
    jmin = jnp.min(jnp.where(keep, j[None, :], nblk), axis=-1)
    need = j - jnp.minimum(jmin, j) + 1
    return jnp.max(need.reshape(need.shape[0], need.shape[1], nblk // chains, chains), axis=-1).astype(jnp.int32)


def _attn(qaug, kaug, vt, trips, chains):
    bsz, nh, seq, _ = qaug.shape
    tq = chains * ATT_BLK
    grid_spec = pltpu.PrefetchScalarGridSpec(
        num_scalar_prefetch=1,
        grid=(bsz, nh, seq // tq),
        in_specs=[
            pl.BlockSpec((1, 1, tq, LANES), lambda b, h, i, tr: (b, h, i, 0)),
            pl.BlockSpec((1, 1, seq, LANES), lambda b, h, i, tr: (b, h, 0, 0)),
            pl.BlockSpec((1, 1, DH_B, seq), lambda b, h, i, tr: (b, h, 0, 0)),
        ],
        out_specs=pl.BlockSpec((1, 1, DH_B, tq), lambda b, h, i, tr: (b, h, 0, i)),
        scratch_shapes=[
            pltpu.VMEM((chains, 1, ATT_BLK), F32),
            pltpu.VMEM((chains, DH_B + DEN_ROWS, ATT_BLK), F32),
            pltpu.VMEM((chains, ATT_BLK, ATT_BLK), F32),
            pltpu.VMEM((chains, ATT_BLK, ATT_BLK), F32),
            pltpu.VMEM((chains, 1, ATT_BLK), F32),
            pltpu.VMEM((chains, 1, ATT_BLK), F32),
            pltpu.VMEM((chains, ATT_BLK, ATT_BLK), BF16),
        ],
    )
    return pl.pallas_call(
        functools.partial(_attn_kernel, chains=chains),
        grid_spec=grid_spec,
        out_shape=jax.ShapeDtypeStruct((bsz, nh, DH_B, seq), F32),
        compiler_params=_params(("parallel", "parallel", "arbitrary")),
        name="attn",
    )(trips.reshape(-1), qaug, kaug, vt)


def _out_b_kernel(ot_ref, sgt_ref, h_ref, wout_ref, gn_ref, wr_ref, br_ref,
                  h3_ref, xs_ref, route_ref, counts_ref, *, tm):
    zt = (ot_ref[0] * sgt_ref[0]).astype(BF16)
    h3 = h_ref[...] + _dot_tn(zt, wout_ref[...])
    h3_ref[...] = h3
    first_step = (pl.program_id(0) == 0) & (pl.program_id(1) == 0)
    _ffn_norm_and_route(h3, gn_ref, wr_ref, br_ref, xs_ref, route_ref, counts_ref, tm, first_step)


def _out_b(ot, sgt, h2, wout, gn, wr, br, bsz, seq, tm):
    ns = seq // tm
    row = lambda b, s: (b * ns + s, 0)
    const2 = lambda b, s: (0, 0)
    const3 = lambda b, s: (0, 0, 0)
    t = bsz * seq
    return pl.pallas_call(
        functools.partial(_out_b_kernel, tm=tm),
        grid=(bsz, ns),
        in_specs=[
            pl.BlockSpec((1, D_MODEL, tm), lambda b, s: (b, 0, s)),
            pl.BlockSpec((1, D_MODEL, tm), lambda b, s: (b, 0, s)),
            pl.BlockSpec((tm, D_MODEL), row),
            pl.BlockSpec((D_MODEL, D_MODEL), const2),
            pl.BlockSpec((1, D_MODEL), const2),
            pl.BlockSpec((2, D_MODEL, LANES), const3),
            pl.BlockSpec((1, LANES), const2),
        ],
        out_specs=_route_out_specs(tm, row, const2),
        out_shape=_route_out_shapes(t),
        compiler_params=_params(("arbitrary", "arbitrary")),
        name="out_b",
    )(ot, sgt, h2, wout, gn, wr, br)


def _hi_lo(w):
    hi = w.astype(BF16)
    lo = (w - hi.astype(F32)).astype(BF16)
    return jnp.stack([hi, lo])


def _pad_lanes(w, width=LANES):
    return jnp.pad(w, ((0, 0),) * (w.ndim - 1) + ((0, width - w.shape[-1]),))


def _router_params(w_grp, b_grp, w_exp, b_exp):
    w = _pad_lanes(jnp.concatenate([w_grp, w_exp], axis=-1))
    b = _pad_lanes(jnp.concatenate([b_grp, b_exp], axis=-1)[None, :])
    return _hi_lo(w), b


def _tile(seq, pref):
    t = pref
    while seq % t:
        t //= 2
    return t


def kernel(x, norm_mix, norm_ffn, a_w_in, a_b_gate, a_head_gain, a_w_out, kv_norm, kv_w, kv_b_f, kv_k_gain,
           b_w_in, b_q_gain, b_w_out, moe_w_grp, moe_b_grp, moe_w_exp, moe_b_exp, moe_w_gate, moe_w_up,
           moe_w_down, norm_final):
    bsz, seq, _ = x.shape
    t = bsz * seq
    assert seq % CHUNK == 0 and seq % ATT_BLK == 0
    tm = _tile(seq, 512)
    assert tm % ATT_BLK == 0
    x2d = x.reshape(t, D_MODEL)

    w_in = a_w_in[0]
    wq = w_in[:, :QK_A].astype(BF16)
    wk = w_in[:, QK_A:2 * QK_A].astype(BF16)
    wv = w_in[:, 2 * QK_A:2 * QK_A + D_MODEL].astype(BF16)
    wo = w_in[:, 2 * QK_A + D_MODEL:2 * QK_A + 2 * D_MODEL].astype(BF16)
    wgate = w_in[:, 2 * QK_A + 2 * D_MODEL:]
    wgc = _hi_lo(_pad_lanes(wgate))
    wgr = _hi_lo(wgate.T)
    bgc = _pad_lanes(a_b_gate[0][None, :])
    bgr = a_b_gate[0][:, None]
    q, k, v, o, gc, gr = _proj_a(x2d, norm_mix[0][None, :], wq, wk, wv, wo, wgc, wgr, bgc, bgr, tm)
    hn = _mlstm(q, k, v, gc, gr, bsz, seq)
    wr0, br0 = _router_params(moe_w_grp[0], moe_b_grp[0], moe_w_exp[0], moe_b_exp[0])
    h1, xs1, route1, counts1 = _out_a(hn, o, x2d, a_head_gain[0][None, :], a_w_out[0].astype(BF16),
                                      norm_ffn[0][None, :], wr0, br0, tm)
    y1 = _moe(xs1, route1, counts1, wr0[0], br0, moe_w_gate[0].astype(BF16), moe_w_up[0].astype(BF16),
              moe_w_down[0].astype(BF16))

    wkk = kv_w[:, :D_MODEL].astype(BF16)
    wvt = kv_w[:, D_MODEL:2 * D_MODEL].T.astype(BF16)
    wf = _hi_lo(_pad_lanes(kv_w[:, 2 * D_MODEL:]))
    bf = _pad_lanes(kv_b_f[None, :])
    wq1 = b_w_in[0][:, :D_MODEL].astype(BF16)
    wogt = b_w_in[0][:, D_MODEL:].T.astype(BF16)
    kgain = jnp.tile(kv_k_gain, N_HEADS_B)[None, :]
    qgain = jnp.tile(b_q_gain[0], N_HEADS_B)[None, :] * (DH_B ** -0.5 * LOG2E)
    head_of = jnp.arange(D_MODEL) // DH_B
    gsum = (head_of[:, None] == jnp.arange(LANES)[None, :]).astype(BF16) * (1.0 / DH_B)
    gexp = (jnp.arange(LANES)[:, None] == head_of[None, :]).astype(BF16)
    src = jnp.arange(LANES)[:, None]
    dst = jnp.arange(N_HEADS_B * LANES)[None, :]
    sel = ((src < 3 * N_HEADS_B) & (dst == (src % N_HEADS_B) * LANES + BIAS_LANE + src // N_HEADS_B)).astype(BF16)
    h2, kaug, vt, qaug, sgt, cstat = _proj_kvq(h1, y1, kv_norm[None, :], norm_mix[1][None, :], wkk, wvt, wf, bf,
                                               wq1, wogt, kgain, qgain, gsum, gexp, sel, bsz, seq, tm)
    chains = math.gcd(ATT_CHAINS, seq // ATT_BLK)
    ot = _attn(qaug, kaug, vt, _attn_trips(cstat, kgain, qgain, chains), chains)
    wr1, br1 = _router_params(moe_w_grp[1], moe_b_grp[1], moe_w_exp[1], moe_b_exp[1])
    h3, xs3, route3, counts3 = _out_b(ot.reshape(bsz, D_MODEL, seq), sgt, h2, b_w_out[0].astype(BF16),
                                      norm_ffn[1][None, :], wr1, br1, bsz, seq, tm)
    y3 = _moe(xs3, route3, counts3, wr1[0], br1, moe_w_gate[1].astype(BF16), moe_w_up[1].astype(BF16),
              moe_w_down[1].astype(BF16))
    out = _final(h3, y3, norm_final[None, :], tm)
    return out.reshape(bsz, seq, D_MODEL)
```

```python
import functools
import math

import jax
import jax.numpy as jnp
from jax import lax
from jax.experimental import pallas as pl
from jax.experimental.pallas import tpu as pltpu

F32 = jnp.float32
BF16 = jnp.bfloat16

D_MODEL = 1024
EPS = 1e-6

N_HEADS_A = 4
DV_A = D_MODEL // N_HEADS_A
DQK_A = DV_A // 2
QK_A = N_HEADS_A * DQK_A
GATE_CAP = 15.0
CHUNK = 128
CHUNK_SHIFT = CHUNK.bit_length() - 1

DH_B = 64
N_HEADS_B = D_MODEL // DH_B
LOG2E = 1.4426950408889634
BIAS_LANE = DH_B
NEG_BIG = -1e30
ATT_BLK = 256
ATT_CHAINS = 8
SKIP_LOG2 = 160.0
DEN_ROWS = 16

N_GROUPS = 4
EXP_PER_GROUP = 4
N_EXPERTS = N_GROUPS * EXP_PER_GROUP
D_EXPERT = D_MODEL // 2
ROUTE_LANE0 = N_GROUPS
PAIR_LO = (0, 0, 0, 1, 1, 2)
PAIR_HI = (1, 2, 3, 2, 3, 3)
N_BINS = N_GROUPS * len(PAIR_LO)
MOE_TILE = 256
ROW_SUB = 8
ROWS_PER_STEP = 1024
ROUTE_PIECE = 256

LANES = 128
VMEM_LIMIT = 56 * 1024 * 1024


def _params(sem, vmem=VMEM_LIMIT):
    return pltpu.CompilerParams(dimension_semantics=sem, vmem_limit_bytes=vmem)


def _dot(a, b):
    return jnp.dot(a, b, preferred_element_type=F32)


def _dot_nt(a, b):
    return lax.dot_general(a, b, (((1,), (1,)), ((), ())), preferred_element_type=F32)


def _dot_tn(a, b):
    return lax.dot_general(a, b, (((0,), (0,)), ((), ())), preferred_element_type=F32)


def _split2(x):
    hi = x.astype(BF16)
    lo = (x - hi.astype(F32)).astype(BF16)
    return hi, lo


def _split3(x):
    hi = x.astype(BF16)
    r = x - hi.astype(F32)
    mid = r.astype(BF16)
    lo = (r - mid.astype(F32)).astype(BF16)
    return hi, mid, lo


def _dot_x3(x_hi, x_lo, w_ref):
    return _dot(x_hi, w_ref[0]) + _dot(x_lo, w_ref[0]) + _dot(x_hi, w_ref[1])


def _dot_nt_x3(w_ref, x_hi, x_lo):
    return _dot_nt(w_ref[0], x_hi) + _dot_nt(w_ref[0], x_lo) + _dot_nt(w_ref[1], x_hi)


def _dot_exact_rhs(a, b_exact):
    a0, a1, a2 = _split3(a)
    return _dot(a0, b_exact) + _dot(a1, b_exact) + _dot(a2, b_exact)


def _dot_exact_lhs(a_exact, b):
    b0, b1, b2 = _split3(b)
    return _dot(a_exact, b0) + _dot(a_exact, b1) + _dot(a_exact, b2)


def _rms_scale(x):
    return lax.rsqrt(jnp.mean(x * x, axis=-1, keepdims=True) + EPS)


def _log_sigmoid(z):
    return jnp.minimum(z, 0.0) - jnp.log(1.0 + jnp.exp(-jnp.abs(z)))


def _sigmoid(z):
    return 1.0 / (1.0 + jnp.exp(-z))


def _softcap(z):
    return GATE_CAP * jnp.tanh(z / GATE_CAP)


def _proj_a_kernel(x_ref, g_ref, wq_ref, wk_ref, wv_ref, wo_ref, wgc_ref, wgr_ref, bgc_ref, bgr_ref,
                   q_ref, k_ref, v_ref, o_ref, gc_ref, gr_ref, *, tm):
    x = x_ref[...]
    xn = x * _rms_scale(x) * g_ref[...]
    xh, xl = _split2(xn)
    q_ref[...] = _dot(xh, wq_ref[...]).astype(BF16)
    k_ref[...] = (_dot(xh, wk_ref[...]) * (DQK_A ** -0.5)).astype(BF16)
    v_ref[...] = _dot(xh, wv_ref[...]).astype(BF16)
    o_ref[...] = _dot(xh, wo_ref[...])

    zc = _softcap(_dot_x3(xh, xl, wgc_ref) + bgc_ref[...])
    zr = _softcap(_dot_nt_x3(wgr_ref, xh, xl) + bgr_ref[...])
    lane = lax.broadcasted_iota(jnp.int32, zc.shape, 1)
    sub = lax.broadcasted_iota(jnp.int32, zr.shape, 0)
    vc = jnp.where(lane < N_HEADS_A, zc, _log_sigmoid(zc))
    vr = jnp.where(sub < N_HEADS_A, zr, _log_sigmoid(zr))
    ti = lax.broadcasted_iota(jnp.int32, (tm, tm), 0)
    tj = lax.broadcasted_iota(jnp.int32, (tm, tm), 1)
    same = (ti >> CHUNK_SHIFT) == (tj >> CHUNK_SHIFT)
    tril = jnp.where(same & (tj <= ti), 1.0, 0.0).astype(BF16)
    triu = jnp.where(same & (ti <= tj), 1.0, 0.0).astype(BF16)
    cc = _dot_exact_lhs(tril, vc)
    cr = _dot_exact_rhs(vr, triu)
    gc = jnp.where(lane < N_HEADS_A, vc, cc)
    gr = jnp.where(sub < N_HEADS_A, vr, cr)
    a_rows = gr[:N_HEADS_A, :] - gr[N_HEADS_A:, :]
    ci = lax.broadcasted_iota(jnp.int32, (CHUNK, CHUNK), 0)
    cj = lax.broadcasted_iota(jnp.int32, (CHUNK, CHUNK), 1)
    for h in range(N_HEADS_A):
        col = jnp.concatenate(
            [jnp.max(jnp.where(cj <= ci, a_rows[h:h + 1, c * CHUNK:(c + 1) * CHUNK], -jnp.inf), axis=-1, keepdims=True)
             for c in range(tm // CHUNK)], axis=0)
        gc = jnp.where(lane == 2 * N_HEADS_A + h, col, gc)
    gc_ref[...] = gc
    gr_ref[...] = gr


def _proj_a(x2d, g, wq, wk, wv, wo, wgc, wgr, bgc, bgr, tm):
    t = x2d.shape[0]
    row = lambda i: (i, 0)
    const2 = lambda i: (0, 0)
    const3 = lambda i: (0, 0, 0)
    return pl.pallas_call(
        functools.partial(_proj_a_kernel, tm=tm),
        grid=(t // tm,),
        in_specs=[
            pl.BlockSpec((tm, D_MODEL), row),
            pl.BlockSpec((1, D_MODEL), const2),
            pl.BlockSpec((D_MODEL, QK_A), const2),
            pl.BlockSpec((D_MODEL, QK_A), const2),
            pl.BlockSpec((D_MODEL, D_MODEL), const2),
            pl.BlockSpec((D_MODEL, D_MODEL), const2),
            pl.BlockSpec((2, D_MODEL, LANES), const3),
            pl.BlockSpec((2, 8, D_MODEL), const3),
            pl.BlockSpec((1, LANES), const2),
            pl.BlockSpec((8, 1), const2),
        ],
        out_specs=[
            pl.BlockSpec((tm, QK_A), row),
            pl.BlockSpec((tm, QK_A), row),
            pl.BlockSpec((tm, D_MODEL), row),
            pl.BlockSpec((tm, D_MODEL), row),
            pl.BlockSpec((tm, LANES), row),
            pl.BlockSpec((8, tm), lambda i: (0, i)),
        ],
        out_shape=[
            jax.ShapeDtypeStruct((t, QK_A), BF16),
            jax.ShapeDtypeStruct((t, QK_A), BF16),
            jax.ShapeDtypeStruct((t, D_MODEL), BF16),
            jax.ShapeDtypeStruct((t, D_MODEL), F32),
            jax.ShapeDtypeStruct((t, LANES), F32),
            jax.ShapeDtypeStruct((8, t), F32),
        ],
        compiler_params=_params(("parallel",)),
        name="proj_a",
    )(x2d, g, wq, wk, wv, wo, wgc, wgr, bgc, bgr)


def _mlstm_kernel(q_ref, k_ref, v_ref, gc_ref, gr_ref, h_ref, c_scr, m_scr):
    @pl.when(pl.program_id(1) == 0)
    def _():
        c_scr[...] = jnp.zeros_like(c_scr)
        m_scr[...] = jnp.zeros_like(m_scr)

    ti = lax.broadcasted_iota(jnp.int32, (CHUNK, CHUNK), 0)
    si = lax.broadcasted_iota(jnp.int32, (CHUNK, CHUNK), 1)
    causal = si <= ti
    gc = gc_ref[...]
    gr = gr_ref[...]
    heads = range(N_HEADS_A)
    ones_blk = jnp.ones((CHUNK, LANES), BF16)
    ones_sq = jnp.ones((DV_A, LANES), BF16)
    qs = [q_ref[:, h * DQK_A:(h + 1) * DQK_A] for h in heads]
    ks = [k_ref[:, h * DQK_A:(h + 1) * DQK_A] for h in heads]
    vs = [jnp.concatenate([v_ref[:, h * DV_A:(h + 1) * DV_A], ones_blk], axis=1) for h in heads]

    gates = []
    for h in heads:
        b_c = gc[:, N_HEADS_A + h:N_HEADS_A + h + 1]
        a_c = gc[:, h:h + 1] - b_c
        amax_c = gc[:, 2 * N_HEADS_A + h:2 * N_HEADS_A + h + 1]
        a_r = gr[h:h + 1, :] - gr[N_HEADS_A + h:N_HEADS_A + h + 1, :]
        b_end = b_c[CHUNK - 1:CHUNK, :]
        m_run = m_scr[h][:, 0:1]
        mx_c = jnp.maximum(m_run, amax_c)
        mx_end = mx_c[CHUNK - 1:CHUNK, :]
        d_mat = jnp.where(causal, jnp.exp(a_r - mx_c), 0.0)
        w_inter = jnp.broadcast_to(jnp.exp(m_run - mx_c), (CHUNK, LANES))
        floor = jnp.broadcast_to(jnp.exp(-(b_c + mx_c)), (CHUNK, LANES))
        w_c = jnp.exp(a_c - mx_end)
        decay = jnp.exp(m_run - mx_end)
        gates.append((d_mat, w_inter, floor, b_end + mx_end, w_c, decay))

    kws = [(ks[h].astype(F32) * gates[h][4]).astype(BF16) for h in heads]
    s_raw = [_dot_nt(qs[h], ks[h]) for h in heads]
    inter = [_dot(qs[h], c_scr[h].astype(BF16)) for h in heads]
    c_upd = [_dot_tn(kws[h], vs[h]) for h in heads]
    for h in heads:
        d_mat, w_inter, floor, m_new, w_c, decay = gates[h]
        tot = _dot((s_raw[h] * d_mat).astype(BF16), vs[h])
        den = tot[:, DV_A:] + w_inter * inter[h][:, DV_A:]
        rden = 1.0 / jnp.maximum(jnp.abs(den), floor)
        halves = [(tot[:, j * LANES:(j + 1) * LANES] + w_inter * inter[h][:, j * LANES:(j + 1) * LANES]) * rden
                  for j in range(DV_A // LANES)]
        hh = jnp.concatenate(halves, axis=1)
        ms = _dot((hh * hh).astype(BF16), ones_sq) * (1.0 / DV_A)
        rs = lax.rsqrt(ms + EPS)
        for j in range(DV_A // LANES):
            h_ref[:, h * DV_A + j * LANES:h * DV_A + (j + 1) * LANES] = halves[j] * rs
        c_scr[h] = decay * c_scr[h] + c_upd[h]
        m_scr[h] = jnp.broadcast_to(m_new, (1, LANES))


def _mlstm(q, k, v, gc, gr, bsz, seq):
    nc = seq // CHUNK
    row = lambda b, c: (b * nc + c, 0)
    return pl.pallas_call(
        _mlstm_kernel,
        grid=(bsz, nc),
        in_specs=[
            pl.BlockSpec((CHUNK, QK_A), row),
            pl.BlockSpec((CHUNK, QK_A), row),
            pl.BlockSpec((CHUNK, D_MODEL), row),
            pl.BlockSpec((CHUNK, LANES), row),
            pl.BlockSpec((8, CHUNK), lambda b, c: (0, b * nc + c)),
        ],
        out_specs=pl.BlockSpec((CHUNK, D_MODEL), row),
        out_shape=jax.ShapeDtypeStruct((bsz * seq, D_MODEL), F32),
        scratch_shapes=[
            pltpu.VMEM((N_HEADS_A, DQK_A, DV_A + LANES), F32),
            pltpu.VMEM((N_HEADS_A, 1, LANES), F32),
        ],
        compiler_params=_params(("arbitrary", "arbitrary")),
        name="mlstm",
    )(q, k, v, gc, gr)


def _route_bins(logits):
    lane = lax.broadcasted_iota(jnp.int32, logits.shape, 1)
    big = jnp.int32(10 ** 6)

    def top(mask):
        mx = jnp.max(jnp.where(mask, logits, -jnp.inf), axis=-1, keepdims=True)
        return jnp.min(jnp.where(mask & (logits == mx), lane, big), axis=-1, keepdims=True)

    gidx = top(lane < N_GROUPS)
    lo = ROUTE_LANE0 + EXP_PER_GROUP * gidx
    emask = (lane >= lo) & (lane < lo + EXP_PER_GROUP)
    i1 = top(emask)
    i2 = top(emask & (lane != i1))
    p_lo = jnp.minimum(i1, i2) - lo
    p_hi = jnp.maximum(i1, i2) - lo
    pair = jnp.where(p_lo == 0, p_hi - 1, jnp.where(p_lo == 1, p_hi + 1, len(PAIR_LO) - 1))
    return gidx * len(PAIR_LO) + pair


def _pair_weights(logits, grp, e_lo, e_hi):
    lane = lax.broadcasted_iota(jnp.int32, logits.shape, 1)

    def pick(idx):
        return jnp.sum(jnp.where(lane == idx, logits, 0.0), axis=-1, keepdims=True)

    def softmax_stats(mask):
        mx = jnp.max(jnp.where(mask, logits, -jnp.inf), axis=-1, keepdims=True)
        return mx, jnp.sum(jnp.where(mask, jnp.exp(logits - mx), 0.0), axis=-1, keepdims=True)

    gmax, gsum = softmax_stats(lane < N_GROUPS)
    g_p = jnp.exp(pick(grp) - gmax) / gsum
    lo = ROUTE_LANE0 + EXP_PER_GROUP * grp
    emax, esum = softmax_stats((lane >= lo) & (lane < lo + EXP_PER_GROUP))
    p_lo = jnp.exp(pick(ROUTE_LANE0 + e_lo) - emax) / esum
    p_hi = jnp.exp(pick(ROUTE_LANE0 + e_hi) - emax) / esum
    return g_p * (p_lo / (p_lo + p_hi)), g_p * (p_hi / (p_lo + p_hi))


def _store_rows(rows_ref, x, tm, tok0=0):
    for c in range(ROW_SUB):
        rows_ref[pl.ds(tok0 * ROW_SUB + c, tm, stride=ROW_SUB), :] = x[:, c * LANES:(c + 1) * LANES]


def _load_rows(rows_ref, tm):
    return jnp.concatenate([rows_ref[pl.ds(c, tm, stride=ROW_SUB), :] for c in range(ROW_SUB)], axis=-1)


def _ffn_norm_and_route(h_parts, gn_ref, wr_ref, br_ref, xs_ref, route_ref, counts_ref, first_step):
    @pl.when(first_step)
    def _():
        counts_ref[...] = jnp.zeros_like(counts_ref)

    tp = h_parts[0].shape[0]
    pieces = range(len(h_parts))
    xns = [h * _rms_scale(h) * gn_ref[...] for h in h_parts]
    for p in pieces:
        _store_rows(xs_ref, xns[p], tp, p * tp)
    splits = [_split2(xn) for xn in xns]
    logits = [_dot_x3(xh, xl, wr_ref) + br_ref[...] for xh, xl in splits]
    bins = [_route_bins(lg) for lg in logits]
    lane = lax.broadcasted_iota(jnp.int32, (tp, LANES), 1)
    onehots = [jnp.where(lane == b, 1.0, 0.0) for b in bins]
    ti = lax.broadcasted_iota(jnp.int32, (tp, tp), 0)
    tj = lax.broadcasted_iota(jnp.int32, (tp, tp), 1)
    earlier = jnp.where(tj < ti, 1.0, 0.0).astype(BF16)
    befores = [_dot(earlier, oh.astype(BF16)) for oh in onehots]
    pick = jnp.where(lax.broadcasted_iota(jnp.int32, (8, LANES), 0) == lax.broadcasted_iota(jnp.int32, (8, LANES), 1),
                     1.0, 0.0).astype(BF16)
    counts = counts_ref[...]
    for p in pieces:
        rank = jnp.sum(onehots[p] * (befores[p] + counts), axis=-1, keepdims=True)
        counts = counts + jnp.sum(onehots[p], axis=0, keepdims=True)
        rank_hi = jnp.floor(rank * (1.0 / 256.0))
        cols = jnp.where(lane == 0, bins[p].astype(F32),
                         jnp.where(lane == 1, rank_hi, jnp.where(lane == 2, rank - 256.0 * rank_hi, 0.0)))
        route_ref[:, p * tp:(p + 1) * tp] = _dot_nt(pick, cols.astype(BF16))
    counts_ref[...] = counts


def _out_a_kernel(hn_ref, o_ref, x_ref, hg_ref, wout_ref, gn_ref, wr_ref, br_ref,
                  h1_ref, xs_ref, route_ref, counts_ref, *, tm):
    h_parts = []
    for p in range(tm // ROUTE_PIECE):
        rows = slice(p * ROUTE_PIECE, (p + 1) * ROUTE_PIECE)
        z = (hn_ref[rows, :] * hg_ref[...] * _sigmoid(o_ref[rows, :])).astype(BF16)
        h1 = x_ref[rows, :] + _dot(z, wout_ref[...])
        h1_ref[rows, :] = h1
        h_parts.append(h1)
    _ffn_norm_and_route(h_parts, gn_ref, wr_ref, br_ref, xs_ref, route_ref, counts_ref, pl.program_id(0) == 0)


def _route_out_specs(tm, row, const2):
    return [
        pl.BlockSpec((tm, D_MODEL), row),
        pl.BlockSpec((tm * ROW_SUB, LANES), row),
        pl.BlockSpec((8, tm), lambda *idx: (0, row(*idx)[0])),
        pl.BlockSpec((1, LANES), const2),
    ]


def _route_out_shapes(t):
    return [
        jax.ShapeDtypeStruct((t, D_MODEL), F32),
        jax.ShapeDtypeStruct((t * ROW_SUB, LANES), F32),
        jax.ShapeDtypeStruct((8, t), F32),
        jax.ShapeDtypeStruct((1, LANES), F32),
    ]


def _out_a(hn, o, x2d, hg, wout, gn, wr, br, tm):
    t = x2d.shape[0]
    row = lambda i: (i, 0)
    const2 = lambda i: (0, 0)
    const3 = lambda i: (0, 0, 0)
    return pl.pallas_call(
        functools.partial(_out_a_kernel, tm=tm),
        grid=(t // tm,),
        in_specs=[
            pl.BlockSpec((tm, D_MODEL), row),
            pl.BlockSpec((tm, D_MODEL), row),
            pl.BlockSpec((tm, D_MODEL), row),
            pl.BlockSpec((1, D_MODEL), const2),
            pl.BlockSpec((D_MODEL, D_MODEL), const2),
            pl.BlockSpec((1, D_MODEL), const2),
            pl.BlockSpec((2, D_MODEL, LANES), const3),
            pl.BlockSpec((1, LANES), const2),
        ],
        out_specs=_route_out_specs(tm, row, const2),
        out_shape=_route_out_shapes(t),
        compiler_params=_params(("arbitrary",)),
        name="out_a",
    )(hn, o, x2d, hg, wout, gn, wr, br)


def _dispatch_kernel(slot_ref, src_ref, zeros_ref, dst_ref, sem, *, td):
    del zeros_ref

    def body(j, carry):
        row = pl.multiple_of(slot_ref[0, 0, j] * ROW_SUB, ROW_SUB)
        pltpu.make_async_copy(src_ref.at[pl.ds(pl.multiple_of(j * ROW_SUB, ROW_SUB), ROW_SUB)],
                              dst_ref.at[pl.ds(row, ROW_SUB)], sem).start()
        return carry

    lax.fori_loop(0, td, body, 0, unroll=8)
    pltpu.make_async_copy(src_ref, dst_ref.at[pl.ds(0, td * ROW_SUB)], sem).wait()


def _dispatch(slots, src, n_dst_tokens):
    t = slots.shape[0]
    td = math.gcd(ROWS_PER_STEP, t)
    any_spec = pl.BlockSpec(memory_space=pl.ANY)
    return pl.pallas_call(
        functools.partial(_dispatch_kernel, td=td),
        grid=(t // td,),
        in_specs=[
            pl.BlockSpec((1, 1, td), lambda i: (i, 0, 0), memory_space=pltpu.SMEM),
            pl.BlockSpec((td * ROW_SUB, LANES), lambda i: (i, 0)),
            any_spec,
        ],
        out_specs=any_spec,
        scratch_shapes=[pltpu.SemaphoreType.DMA(())],
        out_shape=jax.ShapeDtypeStruct((n_dst_tokens * ROW_SUB, LANES), F32),
        input_output_aliases={2: 0},
        compiler_params=_params(("arbitrary",)),
        name="moe_dispatch",
    )(slots.reshape(t // td, 1, td), src, jnp.zeros((n_dst_tokens * ROW_SUB, LANES), F32))


def _moe_kernel(elo_ref, ehi_ref, grp_ref, valid_ref, xs_ref, wr_ref, br_ref,
                wg_lo_ref, wu_lo_ref, wd_lo_ref, wg_hi_ref, wu_hi_ref, wd_hi_ref, ys_ref):
    i = pl.program_id(0)

    @pl.when(valid_ref[i] == 0)
    def _():
        ys_ref[...] = jnp.zeros_like(ys_ref)

    @pl.when(valid_ref[i] != 0)
    def _():
        x = _load_rows(xs_ref, MOE_TILE).astype(BF16)
        w_lo, w_hi = _pair_weights(_dot(x, wr_ref[...]) + br_ref[...], grp_ref[i], elo_ref[i], ehi_ref[i])
        y = None
        for wg_ref, wu_ref, wd_ref, w in ((wg_lo_ref, wu_lo_ref, wd_lo_ref, w_lo),
                                           (wg_hi_ref, wu_hi_ref, wd_hi_ref, w_hi)):
            g = _dot(x, wg_ref[0])
            u = _dot(x, wu_ref[0])
            d = _dot((g * _sigmoid(g) * u * w).astype(BF16), wd_ref[0])
            y = d if y is None else y + d
        _store_rows(ys_ref, y, MOE_TILE)


def _moe_plan(route, counts, n_tiles):
    cnt = counts[0, :N_BINS].astype(jnp.int32)
    padded = (cnt + (MOE_TILE - 1)) // MOE_TILE * MOE_TILE
    ends = jnp.cumsum(padded)
    starts = ends - padded
    rank = (route[1] * 256.0 + route[2]).astype(jnp.int32)
    slots = starts[route[0].astype(jnp.int32)] + rank
    last_tile = ends[-1] // MOE_TILE - 1
    tile = jnp.arange(n_tiles, dtype=jnp.int32)
    first_row = jnp.minimum(tile, last_tile) * MOE_TILE
    tbin = jnp.sum((ends[None, :] <= first_row[:, None]).astype(jnp.int32), axis=1)
    grp = tbin // len(PAIR_LO)
    pair = tbin % len(PAIR_LO)
    elo = grp * EXP_PER_GROUP + jnp.asarray(PAIR_LO, jnp.int32)[pair]
    ehi = grp * EXP_PER_GROUP + jnp.asarray(PAIR_HI, jnp.int32)[pair]
    valid = (tile <= last_tile).astype(jnp.int32)
    return slots, elo, ehi, grp, valid


def _moe_experts(xs_sorted, elo, ehi, grp, valid, wr, br, wg, wu, wd):
    n_tiles = xs_sorted.shape[0] // (MOE_TILE * ROW_SUB)
    rows = pl.BlockSpec((MOE_TILE * ROW_SUB, LANES), lambda i, *_: (i, 0))
    lo3 = lambda i, elo, ehi, grp, valid: (elo[i], 0, 0)
    hi3 = lambda i, elo, ehi, grp, valid: (ehi[i], 0, 0)
    const2 = lambda i, *_: (0, 0)
    up_shape, down_shape = (1, D_MODEL, D_EXPERT), (1, D_EXPERT, D_MODEL)
    return pl.pallas_call(
        _moe_kernel,
        grid_spec=pltpu.PrefetchScalarGridSpec(
            num_scalar_prefetch=4,
            grid=(n_tiles,),
            in_specs=[
                rows,
                pl.BlockSpec((D_MODEL, LANES), const2),
                pl.BlockSpec((1, LANES), const2),
                pl.BlockSpec(up_shape, lo3), pl.BlockSpec(up_shape, lo3), pl.BlockSpec(down_shape, lo3),
                pl.BlockSpec(up_shape, hi3), pl.BlockSpec(up_shape, hi3), pl.BlockSpec(down_shape, hi3),
            ],
            out_specs=rows,
        ),
        out_shape=jax.ShapeDtypeStruct(xs_sorted.shape, F32),
        compiler_params=_params(("arbitrary",)),
        name="moe_experts",
    )(elo, ehi, grp, valid, xs_sorted, wr, br, wg, wu, wd, wg, wu, wd)


def _moe(xs, route, counts, wr, br, wg, wu, wd):
    t = route.shape[1]
    n_sorted = t + N_BINS * MOE_TILE
    slots, elo, ehi, grp, valid = _moe_plan(route, counts, n_sorted // MOE_TILE)
    xs_sorted = _dispatch(slots, xs, n_sorted)
    return _moe_experts(xs_sorted, elo, ehi, grp, valid, wr, br, wg, wu, wd), slots


def _gather_start(slot_ref, src_ref, buf_ref, sem, tm):
    def body(j, carry):
        row = pl.multiple_of(slot_ref[0, 0, j] * ROW_SUB, ROW_SUB)
        pltpu.make_async_copy(src_ref.at[pl.ds(row, ROW_SUB)],
                              buf_ref.at[pl.ds(pl.multiple_of(j * ROW_SUB, ROW_SUB), ROW_SUB)], sem).start()
        return carry

    lax.fori_loop(0, tm, body, 0, unroll=8)


def _gathered_rows(step, n_steps, slot_cur_ref, slot_next_ref, src_ref, buf_ref, sem, tm):
    cur = step & 1

    @pl.when(step == 0)
    def _():
        _gather_start(slot_cur_ref, src_ref, buf_ref.at[0], sem.at[0], tm)

    @pl.when(step + 1 < n_steps)
    def _():
        _gather_start(slot_next_ref, src_ref, buf_ref.at[1 - cur], sem.at[1 - cur], tm)

    pltpu.make_async_copy(src_ref.at[pl.ds(0, tm * ROW_SUB)], buf_ref.at[cur], sem.at[cur]).wait()
    return _load_rows(buf_ref.at[cur], tm)


def _gather_specs(tm, n_steps, step_of):
    cur = lambda *idx: (step_of(*idx), 0, 0)
    nxt = lambda *idx: (jnp.minimum(step_of(*idx) + 1, n_steps - 1), 0, 0)
    return [pl.BlockSpec((1, 1, tm), cur, memory_space=pltpu.SMEM),
            pl.BlockSpec((1, 1, tm), nxt, memory_space=pltpu.SMEM),
            pl.BlockSpec(memory_space=pl.ANY)]


def _gather_scratch(tm):
    return [pltpu.VMEM((2, tm * ROW_SUB, LANES), F32), pltpu.SemaphoreType.DMA((2,))]


def _final_kernel(h_ref, slot_cur_ref, slot_next_ref, ys_ref, g_ref, out_ref, ybuf, sem, *, tm):
    y = _gathered_rows(pl.program_id(0), pl.num_programs(0), slot_cur_ref, slot_next_ref, ys_ref, ybuf, sem, tm)
    h = h_ref[...] + y
    out_ref[...] = h * _rms_scale(h) * g_ref[...]


def _final(h3, ys_sorted, slots, g, tm):
    t = h3.shape[0]
    n_steps = t // tm
    row = lambda i: (i, 0)
    slots3 = slots.reshape(n_steps, 1, tm)
    return pl.pallas_call(
        functools.partial(_final_kernel, tm=tm),
        grid=(n_steps,),
        in_specs=[pl.BlockSpec((tm, D_MODEL), row)] + _gather_specs(tm, n_steps, lambda i: i)
        + [pl.BlockSpec((1, D_MODEL), lambda i: (0, 0))],
        out_specs=pl.BlockSpec((tm, D_MODEL), row),
        out_shape=jax.ShapeDtypeStruct((t, D_MODEL), F32),
        scratch_shapes=_gather_scratch(tm),
        compiler_params=_params(("arbitrary",)),
        name="final_norm",
    )(h3, slots3, slots3, ys_sorted, g)


def _head_rms(x, gsum_ref, gexp_ref):
    s_hi, s_lo = _split2(x * x)
    ms = _dot(s_hi, gsum_ref[...]) + _dot(s_lo, gsum_ref[...])
    r_hi, r_lo = _split2(lax.rsqrt(ms + EPS))
    return _dot(r_hi, gexp_ref[...]) + _dot(r_lo, gexp_ref[...])


def _proj_kvq_kernel(h_ref, slot_cur_ref, slot_next_ref, ys_ref, gkv_ref, gq_ref, wk_ref, wvt_ref, wf_ref, bf_ref,
                     wq_ref, wogt_ref, kgain_ref, qgain_ref, gsum_ref, gexp_ref, sel_ref,
                     h2_ref, kaug_ref, vt_ref, qaug_ref, sgt_ref, cstat_ref, carry_ref, ybuf, sem, *, tm):
    @pl.when(pl.program_id(1) == 0)
    def _():
        carry_ref[...] = jnp.zeros_like(carry_ref)

    step = pl.program_id(0) * pl.num_programs(1) + pl.program_id(1)
    n_steps = pl.num_programs(0) * pl.num_programs(1)
    ymoe = _gathered_rows(step, n_steps, slot_cur_ref, slot_next_ref, ys_ref, ybuf, sem, tm)
    hres = h_ref[...] + ymoe
    h2_ref[...] = hres
    y = hres * _rms_scale(hres)
    a = y * gkv_ref[...]
    ah, al = _split2(a)
    bh = (y * gq_ref[...]).astype(BF16)

    logf = _log_sigmoid(_dot_x3(ah, al, wf_ref) + bf_ref[...])
    ti = lax.broadcasted_iota(jnp.int32, (tm, tm), 0)
    tj = lax.broadcasted_iota(jnp.int32, (tm, tm), 1)
    tril = jnp.where(tj <= ti, 1.0, 0.0).astype(BF16)
    c = _dot_exact_lhs(tril, logf) + carry_ref[...]
    carry_ref[...] = c[tm - 1:tm, :]

    nb = c * (-LOG2E)
    for blk in range(tm // ATT_BLK):
        nb_blk = nb[blk * ATT_BLK:(blk + 1) * ATT_BLK, :]
        cstat_ref[0, blk, 0:1, :] = jnp.max(nb_blk, axis=0, keepdims=True)
        cstat_ref[0, blk, 1:2, :] = jnp.min(nb_blk, axis=0, keepdims=True)
    n0, n1, n2 = _split3(nb)
    lane = lax.broadcasted_iota(jnp.int32, (tm, LANES), 1)
    packed = jnp.where(lane < N_HEADS_B, n0.astype(F32),
                       jnp.where(lane < 2 * N_HEADS_B, pltpu.roll(n1.astype(F32), N_HEADS_B, 1),
                                 pltpu.roll(n2.astype(F32), 2 * N_HEADS_B, 1)))
    packed = jnp.where(lane < 3 * N_HEADS_B, packed, 0.0).astype(BF16)
    extras = _dot(packed, sel_ref[...])

    k = _dot(ah, wk_ref[...])
    kn = k * _head_rms(k, gsum_ref, gexp_ref) * kgain_ref[...]
    q = _dot(bh, wq_ref[...])
    qn = q * _head_rms(q, gsum_ref, gexp_ref) * qgain_ref[...]
    ones = jnp.where((lane >= BIAS_LANE) & (lane < BIAS_LANE + 3), 1.0, 0.0)
    for j in range(N_HEADS_B // 2):
        kj = kn[:, j * LANES:(j + 1) * LANES]
        qj = qn[:, j * LANES:(j + 1) * LANES]
        for half, (kk, qq) in enumerate(((kj, qj), (pltpu.roll(kj, DH_B, 1), pltpu.roll(qj, DH_B, 1)))):
            hd = 2 * j + half
            ex = extras[:, hd * LANES:(hd + 1) * LANES]
            kaug_ref[0, hd] = jnp.where(lane < DH_B, kk, ex).astype(BF16)
            qaug_ref[0, hd] = jnp.where(lane < DH_B, qq, ones).astype(BF16)

    vt = _dot_nt(wvt_ref[...], ah)
    sgt = _sigmoid(_dot_nt(wogt_ref[...], bh))
    for hd in range(N_HEADS_B):
        vt_ref[0, hd] = vt[hd * DH_B:(hd + 1) * DH_B, :].astype(BF16)
    sgt_ref[0] = sgt


def _proj_kvq(h1, ys_sorted, slots, gkv, gq, wk, wvt, wf, bf, wq, wogt, kgain, qgain, gsum, gexp, sel,
              bsz, seq, tm):
    ns = seq // tm
    row = lambda b, s: (b * ns + s, 0)
    const2 = lambda b, s: (0, 0)
    const3 = lambda b, s: (0, 0, 0)
    slots3 = slots.reshape(bsz * ns, 1, tm)
    return pl.pallas_call(
        functools.partial(_proj_kvq_kernel, tm=tm),
        grid=(bsz, ns),
        in_specs=[pl.BlockSpec((tm, D_MODEL), row)] + _gather_specs(tm, bsz * ns, lambda b, s: b * ns + s) + [
            pl.BlockSpec((1, D_MODEL), const2),
            pl.BlockSpec((1, D_MODEL), const2),
            pl.BlockSpec((D_MODEL, D_MODEL), const2),
            pl.BlockSpec((D_MODEL, D_MODEL), const2),
            pl.BlockSpec((2, D_MODEL, LANES), const3),
            pl.BlockSpec((1, LANES), const2),
            pl.BlockSpec((D_MODEL, D_MODEL), const2),
            pl.BlockSpec((D_MODEL, D_MODEL), const2),
            pl.BlockSpec((1, D_MODEL), const2),
            pl.BlockSpec((1, D_MODEL), const2),
            pl.BlockSpec((D_MODEL, LANES), const2),
            pl.BlockSpec((LANES, D_MODEL), const2),
            pl.BlockSpec((LANES, N_HEADS_B * LANES), const2),
        ],
        out_specs=[
            pl.BlockSpec((tm, D_MODEL), row),
            pl.BlockSpec((1, N_HEADS_B, tm, LANES), lambda b, s: (b, 0, s, 0)),
            pl.BlockSpec((1, N_HEADS_B, DH_B, tm), lambda b, s: (b, 0, 0, s)),
            pl.BlockSpec((1, N_HEADS_B, tm, LANES), lambda b, s: (b, 0, s, 0)),
            pl.BlockSpec((1, D_MODEL, tm), lambda b, s: (b, 0, s)),
            pl.BlockSpec((1, tm // ATT_BLK, 2, LANES), lambda b, s: (b, s, 0, 0)),
        ],
        out_shape=[
            jax.ShapeDtypeStruct((bsz * seq, D_MODEL), F32),
            jax.ShapeDtypeStruct((bsz, N_HEADS_B, seq, LANES), BF16),
            jax.ShapeDtypeStruct((bsz, N_HEADS_B, DH_B, seq), BF16),
            jax.ShapeDtypeStruct((bsz, N_HEADS_B, seq, LANES), BF16),
            jax.ShapeDtypeStruct((bsz, D_MODEL, seq), F32),
            jax.ShapeDtypeStruct((bsz, seq // ATT_BLK, 2, LANES), F32),
        ],
        scratch_shapes=[pltpu.VMEM((1, LANES), F32)] + _gather_scratch(tm),
        compiler_params=_params(("arbitrary", "arbitrary")),
        name="proj_kvq",
    )(h1, slots3, slots3, ys_sorted, gkv, gq, wk, wvt, wf, bf, wq, wogt, kgain, qgain, gsum, gexp, sel)


def _attn_kernel(trips_ref, q_ref, k_ref, vt_ref, o_ref, m_scr, acc_scr, sa_scr, sb_scr, xa_scr, xb_scr,
                 p_scr, *, chains):
    b, h, i = pl.program_id(0), pl.program_id(1), pl.program_id(2)
    trips = trips_ref[(b * pl.num_programs(1) + h) * pl.num_programs(2) + i]
    blk = ATT_BLK
    lane = lax.broadcasted_iota(jnp.int32, (blk, LANES), 1)
    k_null = jnp.where(lane == BIAS_LANE, NEG_BIG, 0.0).astype(BF16)

    def key_offset(g, t):
        kb = i * chains + g - t
        return kb, pl.multiple_of(jnp.maximum(kb, 0) * blk, blk)

    all_chains = range(chains)
    ones_rows = jnp.ones((DEN_ROWS, blk), BF16)

    def scores(t, s_ref, x_ref, first, gs=all_chains):
        for g in gs:
            q = q_ref[0, 0, g * blk:(g + 1) * blk, :]
            kb, off = key_offset(g, t)
            kblk = k_ref[0, 0, pl.ds(off, blk), :]
            if not first:
                kblk = jnp.where(kb >= 0, kblk, k_null)
            s = _dot_nt(kblk, q)
            if first:
                ki = lax.broadcasted_iota(jnp.int32, (blk, blk), 0)
                qj = lax.broadcasted_iota(jnp.int32, (blk, blk), 1)
                s = jnp.where(ki <= qj, s, NEG_BIG)
            s_ref[g] = s
            x_ref[g] = jnp.max(s, axis=0, keepdims=True)

    def update(t, s_ref, x_ref, gs=all_chains):
        for g in gs:
            _, off = key_offset(g, t)
            v_den = jnp.concatenate([vt_ref[0, 0, :, pl.ds(off, blk)], ones_rows], axis=0)
            m_old = m_scr[g]
            m_new = jnp.maximum(m_old, x_ref[g])
            p_scr[g] = jnp.exp2(s_ref[g] - m_new).astype(BF16)
            acc_scr[g] = jnp.exp2(m_old - m_new) * acc_scr[g] + _dot(v_den, p_scr[g])
            m_scr[g] = m_new

    m_scr[...] = jnp.full(m_scr.shape, NEG_BIG, F32)
    acc_scr[...] = jnp.zeros(acc_scr.shape, F32)
    scores(0, sa_scr, xa_scr, True)
    steps = trips - 1

    def body(pair, carry):
        t = 2 * pair
        for g in all_chains:
            scores(t + 1, sb_scr, xb_scr, False, [g])
            update(t, sa_scr, xa_scr, [g])
        for g in all_chains:
            scores(t + 2, sa_scr, xa_scr, False, [g])
            update(t + 1, sb_scr, xb_scr, [g])
        return carry

    lax.fori_loop(0, steps >> 1, body, 0)

    @pl.when((steps & 1) == 1)
    def _():
        for g in all_chains:
            scores(steps, sb_scr, xb_scr, False, [g])
            update(steps - 1, sa_scr, xa_scr, [g])
        update(steps, sb_scr, xb_scr)

    @pl.when((steps & 1) == 0)
    def _():
        update(steps, sa_scr, xa_scr)

    for g in all_chains:
        o_ref[0, 0, :, g * blk:(g + 1) * blk] = acc_scr[g, :DH_B, :] / acc_scr[g, DH_B:DH_B + 1, :]


def _attn_trips(cstat, kgain, qgain, chains):
    nbmax = jnp.swapaxes(cstat[:, :, 0, :N_HEADS_B], 1, 2)
    nbmin = jnp.swapaxes(cstat[:, :, 1, :N_HEADS_B], 1, 2)
    nblk = nbmax.shape[-1]
    xb = DH_B * jnp.max(jnp.abs(kgain)) * jnp.max(jnp.abs(qgain))
    slack = 2.0 * xb * 1.02 + 2.0 + SKIP_LOG2
    j = jnp.arange(nblk)
    keep = (nbmax[:, :, None, :] - nbmin[:, :, :, None] + slack >= 0.0) & (j[None, :] <= j[:, None])
    jmin = jnp.min(jnp.where(keep, j[None, :], nblk), axis=-1)
    need = j - jnp.minimum(jmin, j) + 1
    return jnp.max(need.reshape(need.shape[0], need.shape[1], nblk // chains, chains), axis=-1).astype(jnp.int32)


def _attn(qaug, kaug, vt, trips, chains):
    bsz, nh, seq, _ = qaug.shape
    tq = chains * ATT_BLK
    grid_spec = pltpu.PrefetchScalarGridSpec(
        num_scalar_prefetch=1,
        grid=(bsz, nh, seq // tq),
        in_specs=[
            pl.BlockSpec((1, 1, tq, LANES), lambda b, h, i, tr: (b, h, i, 0)),
            pl.BlockSpec((1, 1, seq, LANES), lambda b, h, i, tr: (b, h, 0, 0)),
            pl.BlockSpec((1, 1, DH_B, seq), lambda b, h, i, tr: (b, h, 0, 0)),
        ],
        out_specs=pl.BlockSpec((1, 1, DH_B, tq), lambda b, h, i, tr: (b, h, 0, i)),
        scratch_shapes=[
            pltpu.VMEM((chains, 1, ATT_BLK), F32),
            pltpu.VMEM((chains, DH_B + DEN_ROWS, ATT_BLK), F32),
            pltpu.VMEM((chains, ATT_BLK, ATT_BLK), F32),
            pltpu.VMEM((chains, ATT_BLK, ATT_BLK), F32),
            pltpu.VMEM((chains, 1, ATT_BLK), F32),
            pltpu.VMEM((chains, 1, ATT_BLK), F32),
            pltpu.VMEM((chains, ATT_BLK, ATT_BLK), BF16),
        ],
    )
    return pl.pallas_call(
        functools.partial(_attn_kernel, chains=chains),
        grid_spec=grid_spec,
        out_shape=jax.ShapeDtypeStruct((bsz, nh, DH_B, seq), F32),
        compiler_params=_params(("parallel", "parallel", "arbitrary")),
        name="attn",
    )(trips.reshape(-1), qaug, kaug, vt)


def _out_b_kernel(ot_ref, sgt_ref, h_ref, wout_ref, gn_ref, wr_ref, br_ref,
                  h3_ref, xs_ref, route_ref, counts_ref, *, tm):
    h_parts = []
    for p in range(tm // ROUTE_PIECE):
        rows = slice(p * ROUTE_PIECE, (p + 1) * ROUTE_PIECE)
        zt = (ot_ref[0, :, rows] * sgt_ref[0, :, rows]).astype(BF16)
        h3 = h_ref[rows, :] + _dot_tn(zt, wout_ref[...])
        h3_ref[rows, :] = h3
        h_parts.append(h3)
    first_step = (pl.program_id(0) == 0) & (pl.program_id(1) == 0)
    _ffn_norm_and_route(h_parts, gn_ref, wr_ref, br_ref, xs_ref, route_ref, counts_ref, first_step)


def _out_b(ot, sgt, h2, wout, gn, wr, br, bsz, seq, tm):
    ns = seq // tm
    row = lambda b, s: (b * ns + s, 0)
    const2 = lambda b, s: (0, 0)
    const3 = lambda b, s: (0, 0, 0)
    t = bsz * seq
    return pl.pallas_call(
        functools.partial(_out_b_kernel, tm=tm),
        grid=(bsz, ns),
        in_specs=[
            pl.BlockSpec((1, D_MODEL, tm), lambda b, s: (b, 0, s)),
            pl.BlockSpec((1, D_MODEL, tm), lambda b, s: (b, 0, s)),
            pl.BlockSpec((tm, D_MODEL), row),
            pl.BlockSpec((D_MODEL, D_MODEL), const2),
            pl.BlockSpec((1, D_MODEL), const2),
            pl.BlockSpec((2, D_MODEL, LANES), const3),
            pl.BlockSpec((1, LANES), const2),
        ],
        out_specs=_route_out_specs(tm, row, const2),
        out_shape=_route_out_shapes(t),
        compiler_params=_params(("arbitrary", "arbitrary")),
        name="out_b",
    )(ot, sgt, h2, wout, gn, wr, br)


def _hi_lo(w):
    hi = w.astype(BF16)
    lo = (w - hi.astype(F32)).astype(BF16)
    return jnp.stack([hi, lo])


def _pad_lanes(w, width=LANES):
    return jnp.pad(w, ((0, 0),) * (w.ndim - 1) + ((0, width - w.shape[-1]),))


def _router_params(w_grp, b_grp, w_exp, b_exp):
    w = _pad_lanes(jnp.concatenate([w_grp, w_exp], axis=-1))
    b = _pad_lanes(jnp.concatenate([b_grp, b_exp], axis=-1)[None, :])
    return _hi_lo(w), b


def _tile(seq, pref):
    t = pref
    while seq % t:
        t //= 2
    return t


def kernel(x, norm_mix, norm_ffn, a_w_in, a_b_gate, a_head_gain, a_w_out, kv_norm, kv_w, kv_b_f, kv_k_gain,
           b_w_in, b_q_gain, b_w_out, moe_w_grp, moe_b_grp, moe_w_exp, moe_b_exp, moe_w_gate, moe_w_up,
           moe_w_down, norm_final):
    bsz, seq, _ = x.shape
    t = bsz * seq
    assert seq % CHUNK == 0 and seq % ATT_BLK == 0
    tm = _tile(seq, 512)
    assert tm % ATT_BLK == 0
    x2d = x.reshape(t, D_MODEL)

    w_in = a_w_in[0]
    wq = w_in[:, :QK_A].astype(BF16)
    wk = w_in[:, QK_A:2 * QK_A].astype(BF16)
    wv = w_in[:, 2 * QK_A:2 * QK_A + D_MODEL].astype(BF16)
    wo = w_in[:, 2 * QK_A + D_MODEL:2 * QK_A + 2 * D_MODEL].astype(BF16)
    wgate = w_in[:, 2 * QK_A + 2 * D_MODEL:]
    wgc = _hi_lo(_pad_lanes(wgate))
    wgr = _hi_lo(wgate.T)
    bgc = _pad_lanes(a_b_gate[0][None, :])
    bgr = a_b_gate[0][:, None]
    q, k, v, o, gc, gr = _proj_a(x2d, norm_mix[0][None, :], wq, wk, wv, wo, wgc, wgr, bgc, bgr, tm)
    hn = _mlstm(q, k, v, gc, gr, bsz, seq)
    wr0, br0 = _router_params(moe_w_grp[0], moe_b_grp[0], moe_w_exp[0], moe_b_exp[0])
    h1, xs1, route1, counts1 = _out_a(hn, o, x2d, a_head_gain[0][None, :], a_w_out[0].astype(BF16),
                                      norm_ffn[0][None, :], wr0, br0, tm)
    ys1, slots1 = _moe(xs1, route1, counts1, wr0[0], br0, moe_w_gate[0].astype(BF16), moe_w_up[0].astype(BF16),
              moe_w_down[0].astype(BF16))

    wkk = kv_w[:, :D_MODEL].astype(BF16)
    wvt = kv_w[:, D_MODEL:2 * D_MODEL].T.astype(BF16)
    wf = _hi_lo(_pad_lanes(kv_w[:, 2 * D_MODEL:]))
    bf = _pad_lanes(kv_b_f[None, :])
    wq1 = b_w_in[0][:, :D_MODEL].astype(BF16)
    wogt = b_w_in[0][:, D_MODEL:].T.astype(BF16)
    kgain = jnp.tile(kv_k_gain, N_HEADS_B)[None, :]
    qgain = jnp.tile(b_q_gain[0], N_HEADS_B)[None, :] * (DH_B ** -0.5 * LOG2E)
    head_of = jnp.arange(D_MODEL) // DH_B
    gsum = (head_of[:, None] == jnp.arange(LANES)[None, :]).astype(BF16) * (1.0 / DH_B)
    gexp = (jnp.arange(LANES)[:, None] == head_of[None, :]).astype(BF16)
    src = jnp.arange(LANES)[:, None]
    dst = jnp.arange(N_HEADS_B * LANES)[None, :]
    sel = ((src < 3 * N_HEADS_B) & (dst == (src % N_HEADS_B) * LANES + BIAS_LANE + src // N_HEADS_B)).astype(BF16)
    h2, kaug, vt, qaug, sgt, cstat = _proj_kvq(h1, ys1, slots1, kv_norm[None, :], norm_mix[1][None, :], wkk, wvt, wf, bf,
                                               wq1, wogt, kgain, qgain, gsum, gexp, sel, bsz, seq, tm)
    chains = math.gcd(ATT_CHAINS, seq // ATT_BLK)
    ot = _attn(qaug, kaug, vt, _attn_trips(cstat, kgain, qgain, chains), chains)
    wr1, br1 = _router_params(moe_w_grp[1], moe_b_grp[1], moe_w_exp[1], moe_b_exp[1])
    h3, xs3, route3, counts3 = _out_b(ot.reshape(bsz, D_MODEL, seq), sgt, h2, b_w_out[0].astype(BF16),
                                      norm_ffn[1][None, :], wr1, br1, bsz, seq, tm)
    ys3, slots3 = _moe(xs3, route3, counts3, wr1[0], br1, moe_w_gate[1].astype(BF16), moe_w_up[1].astype(BF16),
              moe_w_down[1].astype(BF16))
    out = _final(h3, ys3, slots3, norm_final[None, :], tm)
    return out.reshape(bsz, seq, D_MODEL)
```

```python
import functools
import math

import jax
import jax.numpy as jnp
from jax import lax
from jax.experimental import pallas as pl
from jax.experimental.pallas import tpu as pltpu

F32 = jnp.float32
BF16 = jnp.bfloat16

D_MODEL = 1024
EPS = 1e-6

N_HEADS_A = 4
DV_A = D_MODEL // N_HEADS_A
DQK_A = DV_A // 2
QK_A = N_HEADS_A * DQK_A
GATE_CAP = 15.0
CHUNK = 128
CHUNK_SHIFT = CHUNK.bit_length() - 1

DH_B = 64
N_HEADS_B = D_MODEL // DH_B
LOG2E = 1.4426950408889634
BIAS_LANE = DH_B
NEG_BIG = -1e30
ATT_BLK = 256
ATT_CHAINS = 8
SKIP_LOG2 = 160.0
DEN_ROWS = 16

N_GROUPS = 4
EXP_PER_GROUP = 4
N_EXPERTS = N_GROUPS * EXP_PER_GROUP
D_EXPERT = D_MODEL // 2
ROUTE_LANE0 = N_GROUPS
PAIR_LO = (0, 0, 0, 1, 1, 2)
PAIR_HI = (1, 2, 3, 2, 3, 3)
N_BINS = N_GROUPS * len(PAIR_LO)
MOE_TILE = 256
ROW_SUB = 8
ROWS_PER_STEP = 1024
ROUTE_PIECE = 256

LANES = 128
VMEM_LIMIT = 56 * 1024 * 1024


def _params(sem, vmem=VMEM_LIMIT):
    return pltpu.CompilerParams(dimension_semantics=sem, vmem_limit_bytes=vmem)


def _dot(a, b):
    return jnp.dot(a, b, preferred_element_type=F32)


def _dot_nt(a, b):
    return lax.dot_general(a, b, (((1,), (1,)), ((), ())), preferred_element_type=F32)


def _dot_tn(a, b):
    return lax.dot_general(a, b, (((0,), (0,)), ((), ())), preferred_element_type=F32)


def _split2(x):
    hi = x.astype(BF16)
    lo = (x - hi.astype(F32)).astype(BF16)
    return hi, lo


def _split3(x):
    hi = x.astype(BF16)
    r = x - hi.astype(F32)
    mid = r.astype(BF16)
    lo = (r - mid.astype(F32)).astype(BF16)
    return hi, mid, lo


def _dot_x3(x_hi, x_lo, w_ref):
    return _dot(x_hi, w_ref[0]) + _dot(x_lo, w_ref[0]) + _dot(x_hi, w_ref[1])


def _dot_nt_x3(w_ref, x_hi, x_lo):
    return _dot_nt(w_ref[0], x_hi) + _dot_nt(w_ref[0], x_lo) + _dot_nt(w_ref[1], x_hi)


def _dot_exact_rhs(a, b_exact):
    a0, a1, a2 = _split3(a)
    return _dot(a0, b_exact) + _dot(a1, b_exact) + _dot(a2, b_exact)


def _dot_exact_lhs(a_exact, b):
    b0, b1, b2 = _split3(b)
    return _dot(a_exact, b0) + _dot(a_exact, b1) + _dot(a_exact, b2)


def _rms_scale(x):
    return lax.rsqrt(jnp.mean(x * x, axis=-1, keepdims=True) + EPS)


def _log_sigmoid(z):
    return jnp.minimum(z, 0.0) - jnp.log(1.0 + jnp.exp(-jnp.abs(z)))


def _sigmoid(z):
    return 1.0 / (1.0 + jnp.exp(-z))


def _softcap(z):
    return GATE_CAP * jnp.tanh(z / GATE_CAP)


def _proj_a_kernel(x_ref, g_ref, wq_ref, wk_ref, wv_ref, wo_ref, wgc_ref, wgr_ref, bgc_ref, bgr_ref,
                   q_ref, k_ref, v_ref, o_ref, gc_ref, gr_ref, *, tm):
    x = x_ref[...]
    xn = x * _rms_scale(x) * g_ref[...]
    xh, xl = _split2(xn)
    q_ref[...] = _dot(xh, wq_ref[...]).astype(BF16)
    k_ref[...] = (_dot(xh, wk_ref[...]) * (DQK_A ** -0.5)).astype(BF16)
    v_ref[...] = _dot(xh, wv_ref[...]).astype(BF16)
    o_ref[...] = _dot(xh, wo_ref[...]).astype(BF16)

    zc = _softcap(_dot_x3(xh, xl, wgc_ref) + bgc_ref[...])
    zr = _softcap(_dot_nt_x3(wgr_ref, xh, xl) + bgr_ref[...])
    lane = lax.broadcasted_iota(jnp.int32, zc.shape, 1)
    sub = lax.broadcasted_iota(jnp.int32, zr.shape, 0)
    vc = jnp.where(lane < N_HEADS_A, zc, _log_sigmoid(zc))
    vr = jnp.where(sub < N_HEADS_A, zr, _log_sigmoid(zr))
    ti = lax.broadcasted_iota(jnp.int32, (tm, tm), 0)
    tj = lax.broadcasted_iota(jnp.int32, (tm, tm), 1)
    same = (ti >> CHUNK_SHIFT) == (tj >> CHUNK_SHIFT)
    tril = jnp.where(same & (tj <= ti), 1.0, 0.0).astype(BF16)
    triu = jnp.where(same & (ti <= tj), 1.0, 0.0).astype(BF16)
    cc = _dot_exact_lhs(tril, vc)
    cr = _dot_exact_rhs(vr, triu)
    gc = jnp.where(lane < N_HEADS_A, vc, cc)
    gr = jnp.where(sub < N_HEADS_A, vr, cr)
    a_rows = gr[:N_HEADS_A, :] - gr[N_HEADS_A:, :]
    ci = lax.broadcasted_iota(jnp.int32, (CHUNK, CHUNK), 0)
    cj = lax.broadcasted_iota(jnp.int32, (CHUNK, CHUNK), 1)
    for h in range(N_HEADS_A):
        col = jnp.concatenate(
            [jnp.max(jnp.where(cj <= ci, a_rows[h:h + 1, c * CHUNK:(c + 1) * CHUNK], -jnp.inf), axis=-1, keepdims=True)
             for c in range(tm // CHUNK)], axis=0)
        gc = jnp.where(lane == 2 * N_HEADS_A + h, col, gc)
    gc_ref[...] = gc
    gr_ref[...] = gr


def _proj_a(x2d, g, wq, wk, wv, wo, wgc, wgr, bgc, bgr, tm):
    t = x2d.shape[0]
    row = lambda i: (i, 0)
    const2 = lambda i: (0, 0)
    const3 = lambda i: (0, 0, 0)
    return pl.pallas_call(
        functools.partial(_proj_a_kernel, tm=tm),
        grid=(t // tm,),
        in_specs=[
            pl.BlockSpec((tm, D_MODEL), row),
            pl.BlockSpec((1, D_MODEL), const2),
            pl.BlockSpec((D_MODEL, QK_A), const2),
            pl.BlockSpec((D_MODEL, QK_A), const2),
            pl.BlockSpec((D_MODEL, D_MODEL), const2),
            pl.BlockSpec((D_MODEL, D_MODEL), const2),
            pl.BlockSpec((2, D_MODEL, LANES), const3),
            pl.BlockSpec((2, 8, D_MODEL), const3),
            pl.BlockSpec((1, LANES), const2),
            pl.BlockSpec((8, 1), const2),
        ],
        out_specs=[
            pl.BlockSpec((tm, QK_A), row),
            pl.BlockSpec((tm, QK_A), row),
            pl.BlockSpec((tm, D_MODEL), row),
            pl.BlockSpec((tm, D_MODEL), row),
            pl.BlockSpec((tm, LANES), row),
            pl.BlockSpec((8, tm), lambda i: (0, i)),
        ],
        out_shape=[
            jax.ShapeDtypeStruct((t, QK_A), BF16),
            jax.ShapeDtypeStruct((t, QK_A), BF16),
            jax.ShapeDtypeStruct((t, D_MODEL), BF16),
            jax.ShapeDtypeStruct((t, D_MODEL), BF16),
            jax.ShapeDtypeStruct((t, LANES), F32),
            jax.ShapeDtypeStruct((8, t), F32),
        ],
        compiler_params=_params(("parallel",)),
        name="proj_a",
    )(x2d, g, wq, wk, wv, wo, wgc, wgr, bgc, bgr)


def _mlstm_kernel(q_ref, k_ref, v_ref, gc_ref, gr_ref, h_ref, c_scr, m_scr):
    @pl.when(pl.program_id(1) == 0)
    def _():
        c_scr[...] = jnp.zeros_like(c_scr)
        m_scr[...] = jnp.zeros_like(m_scr)

    ti = lax.broadcasted_iota(jnp.int32, (CHUNK, CHUNK), 0)
    si = lax.broadcasted_iota(jnp.int32, (CHUNK, CHUNK), 1)
    causal = si <= ti
    gc = gc_ref[...]
    gr = gr_ref[...]
    heads = range(N_HEADS_A)
    ones_blk = jnp.ones((CHUNK, LANES), BF16)
    ones_sq = jnp.ones((DV_A, LANES), BF16)
    qs = [q_ref[:, h * DQK_A:(h + 1) * DQK_A] for h in heads]
    ks = [k_ref[:, h * DQK_A:(h + 1) * DQK_A] for h in heads]
    vs = [jnp.concatenate([v_ref[:, h * DV_A:(h + 1) * DV_A], ones_blk], axis=1) for h in heads]

    gates = []
    for h in heads:
        b_c = gc[:, N_HEADS_A + h:N_HEADS_A + h + 1]
        a_c = gc[:, h:h + 1] - b_c
        amax_c = gc[:, 2 * N_HEADS_A + h:2 * N_HEADS_A + h + 1]
        a_r = gr[h:h + 1, :] - gr[N_HEADS_A + h:N_HEADS_A + h + 1, :]
        b_end = b_c[CHUNK - 1:CHUNK, :]
        m_run = m_scr[h][:, 0:1]
        mx_c = jnp.maximum(m_run, amax_c)
        mx_end = mx_c[CHUNK - 1:CHUNK, :]
        d_mat = jnp.where(causal, jnp.exp(a_r - mx_c), 0.0)
        w_inter = jnp.broadcast_to(jnp.exp(m_run - mx_c), (CHUNK, LANES))
        floor = jnp.broadcast_to(jnp.exp(-(b_c + mx_c)), (CHUNK, LANES))
        w_c = jnp.exp(a_c - mx_end)
        decay = jnp.exp(m_run - mx_end)
        gates.append((d_mat, w_inter, floor, b_end + mx_end, w_c, decay))

    kws = [(ks[h].astype(F32) * gates[h][4]).astype(BF16) for h in heads]
    s_raw = [_dot_nt(qs[h], ks[h]) for h in heads]
    inter = [_dot(qs[h], c_scr[h].astype(BF16)) for h in heads]
    c_upd = [_dot_tn(kws[h], vs[h]) for h in heads]
    for h in heads:
        d_mat, w_inter, floor, m_new, w_c, decay = gates[h]
        tot = _dot((s_raw[h] * d_mat).astype(BF16), vs[h])
        den = tot[:, DV_A:] + w_inter * inter[h][:, DV_A:]
        rden = 1.0 / jnp.maximum(jnp.abs(den), floor)
        halves = [(tot[:, j * LANES:(j + 1) * LANES] + w_inter * inter[h][:, j * LANES:(j + 1) * LANES]) * rden
                  for j in range(DV_A // LANES)]
        hh = jnp.concatenate(halves, axis=1)
        ms = _dot((hh * hh).astype(BF16), ones_sq) * (1.0 / DV_A)
        rs = lax.rsqrt(ms + EPS)
        for j in range(DV_A // LANES):
            h_ref[:, h * DV_A + j * LANES:h * DV_A + (j + 1) * LANES] = (halves[j] * rs).astype(BF16)
        c_scr[h] = decay * c_scr[h] + c_upd[h]
        m_scr[h] = jnp.broadcast_to(m_new, (1, LANES))


def _mlstm(q, k, v, gc, gr, bsz, seq):
    nc = seq // CHUNK
    row = lambda b, c: (b * nc + c, 0)
    return pl.pallas_call(
        _mlstm_kernel,
        grid=(bsz, nc),
        in_specs=[
            pl.BlockSpec((CHUNK, QK_A), row),
            pl.BlockSpec((CHUNK, QK_A), row),
            pl.BlockSpec((CHUNK, D_MODEL), row),
            pl.BlockSpec((CHUNK, LANES), row),
            pl.BlockSpec((8, CHUNK), lambda b, c: (0, b * nc + c)),
        ],
        out_specs=pl.BlockSpec((CHUNK, D_MODEL), row),
        out_shape=jax.ShapeDtypeStruct((bsz * seq, D_MODEL), BF16),
        scratch_shapes=[
            pltpu.VMEM((N_HEADS_A, DQK_A, DV_A + LANES), F32),
            pltpu.VMEM((N_HEADS_A, 1, LANES), F32),
        ],
        compiler_params=_params(("arbitrary", "arbitrary")),
        name="mlstm",
    )(q, k, v, gc, gr)


def _route_bins(logits):
    lane = lax.broadcasted_iota(jnp.int32, logits.shape, 1)
    big = jnp.int32(10 ** 6)

    def top(mask):
        mx = jnp.max(jnp.where(mask, logits, -jnp.inf), axis=-1, keepdims=True)
        return jnp.min(jnp.where(mask & (logits == mx), lane, big), axis=-1, keepdims=True)

    gidx = top(lane < N_GROUPS)
    lo = ROUTE_LANE0 + EXP_PER_GROUP * gidx
    emask = (lane >= lo) & (lane < lo + EXP_PER_GROUP)
    i1 = top(emask)
    i2 = top(emask & (lane != i1))
    p_lo = jnp.minimum(i1, i2) - lo
    p_hi = jnp.maximum(i1, i2) - lo
    pair = jnp.where(p_lo == 0, p_hi - 1, jnp.where(p_lo == 1, p_hi + 1, len(PAIR_LO) - 1))
    return gidx * len(PAIR_LO) + pair


def _pair_weights(logits, grp, e_lo, e_hi):
    lane = lax.broadcasted_iota(jnp.int32, logits.shape, 1)

    def pick(idx):
        return jnp.sum(jnp.where(lane == idx, logits, 0.0), axis=-1, keepdims=True)

    def softmax_stats(mask):
        mx = jnp.max(jnp.where(mask, logits, -jnp.inf), axis=-1, keepdims=True)
        return mx, jnp.sum(jnp.where(mask, jnp.exp(logits - mx), 0.0), axis=-1, keepdims=True)

    gmax, gsum = softmax_stats(lane < N_GROUPS)
    g_p = jnp.exp(pick(grp) - gmax) / gsum
    lo = ROUTE_LANE0 + EXP_PER_GROUP * grp
    emax, esum = softmax_stats((lane >= lo) & (lane < lo + EXP_PER_GROUP))
    p_lo = jnp.exp(pick(ROUTE_LANE0 + e_lo) - emax) / esum
    p_hi = jnp.exp(pick(ROUTE_LANE0 + e_hi) - emax) / esum
    return g_p * (p_lo / (p_lo + p_hi)), g_p * (p_hi / (p_lo + p_hi))


def _store_rows(rows_ref, x, tm, tok0=0):
    for c in range(ROW_SUB):
        rows_ref[pl.ds(tok0 * ROW_SUB + c, tm, stride=ROW_SUB), :] = x[:, c * LANES:(c + 1) * LANES]


def _load_rows(rows_ref, tm):
    return jnp.concatenate([rows_ref[pl.ds(c, tm, stride=ROW_SUB), :] for c in range(ROW_SUB)], axis=-1)


def _ffn_norm_and_route(h_parts, gn_ref, wr_ref, br_ref, xs_ref, route_ref, counts_ref, first_step):
    @pl.when(first_step)
    def _():
        counts_ref[...] = jnp.zeros_like(counts_ref)

    tp = h_parts[0].shape[0]
    pieces = range(len(h_parts))
    xns = [h * _rms_scale(h) * gn_ref[...] for h in h_parts]
    for p in pieces:
        _store_rows(xs_ref, xns[p], tp, p * tp)
    splits = [_split2(xn) for xn in xns]
    logits = [_dot_x3(xh, xl, wr_ref) + br_ref[...] for xh, xl in splits]
    bins = [_route_bins(lg) for lg in logits]
    lane = lax.broadcasted_iota(jnp.int32, (tp, LANES), 1)
    onehots = [jnp.where(lane == b, 1.0, 0.0) for b in bins]
    ti = lax.broadcasted_iota(jnp.int32, (tp, tp), 0)
    tj = lax.broadcasted_iota(jnp.int32, (tp, tp), 1)
    earlier = jnp.where(tj < ti, 1.0, 0.0).astype(BF16)
    befores = [_dot(earlier, oh.astype(BF16)) for oh in onehots]
    pick = jnp.where(lax.broadcasted_iota(jnp.int32, (8, LANES), 0) == lax.broadcasted_iota(jnp.int32, (8, LANES), 1),
                     1.0, 0.0).astype(BF16)
    counts = counts_ref[...]
    for p in pieces:
        rank = jnp.sum(onehots[p] * (befores[p] + counts), axis=-1, keepdims=True)
        counts = counts + jnp.sum(onehots[p], axis=0, keepdims=True)
        rank_hi = jnp.floor(rank * (1.0 / 256.0))
        cols = jnp.where(lane == 0, bins[p].astype(F32),
                         jnp.where(lane == 1, rank_hi, jnp.where(lane == 2, rank - 256.0 * rank_hi, 0.0)))
        route_ref[:, p * tp:(p + 1) * tp] = _dot_nt(pick, cols.astype(BF16))
    counts_ref[...] = counts


def _out_a_kernel(hn_ref, o_ref, x_ref, hg_ref, wout_ref, gn_ref, wr_ref, br_ref,
                  h1_ref, xs_ref, route_ref, counts_ref, *, tm):
    h_parts = []
    for p in range(tm // ROUTE_PIECE):
        rows = slice(p * ROUTE_PIECE, (p + 1) * ROUTE_PIECE)
        z = (hn_ref[rows, :].astype(F32) * hg_ref[...] * _sigmoid(o_ref[rows, :].astype(F32))).astype(BF16)
        h1 = x_ref[rows, :] + _dot(z, wout_ref[...])
        h1_ref[rows, :] = h1
        h_parts.append(h1)
    _ffn_norm_and_route(h_parts, gn_ref, wr_ref, br_ref, xs_ref, route_ref, counts_ref, pl.program_id(0) == 0)


def _route_out_specs(tm, row, const2):
    return [
        pl.BlockSpec((tm, D_MODEL), row),
        pl.BlockSpec((tm * ROW_SUB, LANES), row),
        pl.BlockSpec((8, tm), lambda *idx: (0, row(*idx)[0])),
        pl.BlockSpec((1, LANES), const2),
    ]


def _route_out_shapes(t):
    return [
        jax.ShapeDtypeStruct((t, D_MODEL), F32),
        jax.ShapeDtypeStruct((t * ROW_SUB, LANES), F32),
        jax.ShapeDtypeStruct((8, t), F32),
        jax.ShapeDtypeStruct((1, LANES), F32),
    ]


def _out_a(hn, o, x2d, hg, wout, gn, wr, br, tm):
    t = x2d.shape[0]
    row = lambda i: (i, 0)
    const2 = lambda i: (0, 0)
    const3 = lambda i: (0, 0, 0)
    return pl.pallas_call(
        functools.partial(_out_a_kernel, tm=tm),
        grid=(t // tm,),
        in_specs=[
            pl.BlockSpec((tm, D_MODEL), row),
            pl.BlockSpec((tm, D_MODEL), row),
            pl.BlockSpec((tm, D_MODEL), row),
            pl.BlockSpec((1, D_MODEL), const2),
            pl.BlockSpec((D_MODEL, D_MODEL), const2),
            pl.BlockSpec((1, D_MODEL), const2),
            pl.BlockSpec((2, D_MODEL, LANES), const3),
            pl.BlockSpec((1, LANES), const2),
        ],
        out_specs=_route_out_specs(tm, row, const2),
        out_shape=_route_out_shapes(t),
        compiler_params=_params(("arbitrary",)),
        name="out_a",
    )(hn, o, x2d, hg, wout, gn, wr, br)


def _dispatch_kernel(slot_ref, src_ref, zeros_ref, dst_ref, sem, *, td):
    del zeros_ref

    def body(j, carry):
        row = pl.multiple_of(slot_ref[0, 0, j] * ROW_SUB, ROW_SUB)
        pltpu.make_async_copy(src_ref.at[pl.ds(pl.multiple_of(j * ROW_SUB, ROW_SUB), ROW_SUB)],
                              dst_ref.at[pl.ds(row, ROW_SUB)], sem).start()
        return carry

    lax.fori_loop(0, td, body, 0, unroll=8)
    pltpu.make_async_copy(src_ref, dst_ref.at[pl.ds(0, td * ROW_SUB)], sem).wait()


def _dispatch(slots, src, n_dst_tokens):
    t = slots.shape[0]
    td = math.gcd(ROWS_PER_STEP, t)
    any_spec = pl.BlockSpec(memory_space=pl.ANY)
    return pl.pallas_call(
        functools.partial(_dispatch_kernel, td=td),
        grid=(t // td,),
        in_specs=[
            pl.BlockSpec((1, 1, td), lambda i: (i, 0, 0), memory_space=pltpu.SMEM),
            pl.BlockSpec((td * ROW_SUB, LANES), lambda i: (i, 0)),
            any_spec,
        ],
        out_specs=any_spec,
        scratch_shapes=[pltpu.SemaphoreType.DMA(())],
        out_shape=jax.ShapeDtypeStruct((n_dst_tokens * ROW_SUB, LANES), F32),
        input_output_aliases={2: 0},
        compiler_params=_params(("arbitrary",)),
        name="moe_dispatch",
    )(slots.reshape(t // td, 1, td), src, jnp.zeros((n_dst_tokens * ROW_SUB, LANES), F32))


def _moe_kernel(elo_ref, ehi_ref, grp_ref, valid_ref, xs_ref, wr_ref, br_ref,
                wg_lo_ref, wu_lo_ref, wd_lo_ref, wg_hi_ref, wu_hi_ref, wd_hi_ref, ys_ref):
    i = pl.program_id(0)

    @pl.when(valid_ref[i] == 0)
    def _():
        ys_ref[...] = jnp.zeros_like(ys_ref)

    @pl.when(valid_ref[i] != 0)
    def _():
        x = _load_rows(xs_ref, MOE_TILE).astype(BF16)
        w_lo, w_hi = _pair_weights(_dot(x, wr_ref[...]) + br_ref[...], grp_ref[i], elo_ref[i], ehi_ref[i])
        y = None
        for wg_ref, wu_ref, wd_ref, w in ((wg_lo_ref, wu_lo_ref, wd_lo_ref, w_lo),
                                           (wg_hi_ref, wu_hi_ref, wd_hi_ref, w_hi)):
            g = _dot(x, wg_ref[0])
            u = _dot(x, wu_ref[0])
            d = _dot((g * _sigmoid(g) * u * w).astype(BF16), wd_ref[0])
            y = d if y is None else y + d
        _store_rows(ys_ref, y, MOE_TILE)


def _moe_plan(route, counts, n_tiles):
    cnt = counts[0, :N_BINS].astype(jnp.int32)
    padded = (cnt + (MOE_TILE - 1)) // MOE_TILE * MOE_TILE
    ends = jnp.cumsum(padded)
    starts = ends - padded
    rank = (route[1] * 256.0 + route[2]).astype(jnp.int32)
    slots = starts[route[0].astype(jnp.int32)] + rank
    last_tile = ends[-1] // MOE_TILE - 1
    tile = jnp.arange(n_tiles, dtype=jnp.int32)
    first_row = jnp.minimum(tile, last_tile) * MOE_TILE
    tbin = jnp.sum((ends[None, :] <= first_row[:, None]).astype(jnp.int32), axis=1)
    grp = tbin // len(PAIR_LO)
    pair = tbin % len(PAIR_LO)
    elo = grp * EXP_PER_GROUP + jnp.asarray(PAIR_LO, jnp.int32)[pair]
    ehi = grp * EXP_PER_GROUP + jnp.asarray(PAIR_HI, jnp.int32)[pair]
    valid = (tile <= last_tile).astype(jnp.int32)
    return slots, elo, ehi, grp, valid


def _moe_experts(xs_sorted, elo, ehi, grp, valid, wr, br, wg, wu, wd):
    n_tiles = xs_sorted.shape[0] // (MOE_TILE * ROW_SUB)
    rows = pl.BlockSpec((MOE_TILE * ROW_SUB, LANES), lambda i, *_: (i, 0))
    lo3 = lambda i, elo, ehi, grp, valid: (elo[i], 0, 0)
    hi3 = lambda i, elo, ehi, grp, valid: (ehi[i], 0, 0)
    const2 = lambda i, *_: (0, 0)
    up_shape, down_shape = (1, D_MODEL, D_EXPERT), (1, D_EXPERT, D_MODEL)
    return pl.pallas_call(
        _moe_kernel,
        grid_spec=pltpu.PrefetchScalarGridSpec(
            num_scalar_prefetch=4,
            grid=(n_tiles,),
            in_specs=[
                rows,
                pl.BlockSpec((D_MODEL, LANES), const2),
                pl.BlockSpec((1, LANES), const2),
                pl.BlockSpec(up_shape, lo3), pl.BlockSpec(up_shape, lo3), pl.BlockSpec(down_shape, lo3),
                pl.BlockSpec(up_shape, hi3), pl.BlockSpec(up_shape, hi3), pl.BlockSpec(down_shape, hi3),
            ],
            out_specs=rows,
        ),
        out_shape=jax.ShapeDtypeStruct(xs_sorted.shape, F32),
        compiler_params=_params(("arbitrary",)),
        name="moe_experts",
    )(elo, ehi, grp, valid, xs_sorted, wr, br, wg, wu, wd, wg, wu, wd)


def _moe(xs, route, counts, wr, br, wg, wu, wd):
    t = route.shape[1]
    n_sorted = t + N_BINS * MOE_TILE
    slots, elo, ehi, grp, valid = _moe_plan(route, counts, n_sorted // MOE_TILE)
    xs_sorted = _dispatch(slots, xs, n_sorted)
    return _moe_experts(xs_sorted, elo, ehi, grp, valid, wr, br, wg, wu, wd), slots


def _gather_start(slot_ref, src_ref, buf_ref, sem, tm):
    def body(j, carry):
        row = pl.multiple_of(slot_ref[0, 0, j] * ROW_SUB, ROW_SUB)
        pltpu.make_async_copy(src_ref.at[pl.ds(row, ROW_SUB)],
                              buf_ref.at[pl.ds(pl.multiple_of(j * ROW_SUB, ROW_SUB), ROW_SUB)], sem).start()
        return carry

    lax.fori_loop(0, tm, body, 0, unroll=8)


def _gathered_rows(step, n_steps, slot_cur_ref, slot_next_ref, src_ref, buf_ref, sem, tm):
    cur = step & 1

    @pl.when(step == 0)
    def _():
        _gather_start(slot_cur_ref, src_ref, buf_ref.at[0], sem.at[0], tm)

    @pl.when(step + 1 < n_steps)
    def _():
        _gather_start(slot_next_ref, src_ref, buf_ref.at[1 - cur], sem.at[1 - cur], tm)

    pltpu.make_async_copy(src_ref.at[pl.ds(0, tm * ROW_SUB)], buf_ref.at[cur], sem.at[cur]).wait()
    return _load_rows(buf_ref.at[cur], tm)


def _gather_specs(tm, n_steps, step_of):
    cur = lambda *idx: (step_of(*idx), 0, 0)
    nxt = lambda *idx: (jnp.minimum(step_of(*idx) + 1, n_steps - 1), 0, 0)
    return [pl.BlockSpec((1, 1, tm), cur, memory_space=pltpu.SMEM),
            pl.BlockSpec((1, 1, tm), nxt, memory_space=pltpu.SMEM),
            pl.BlockSpec(memory_space=pl.ANY)]


def _gather_scratch(tm):
    return [pltpu.VMEM((2, tm * ROW_SUB, LANES), F32), pltpu.SemaphoreType.DMA((2,))]


def _final_kernel(h_ref, slot_cur_ref, slot_next_ref, ys_ref, g_ref, out_ref, ybuf, sem, *, tm):
    y = _gathered_rows(pl.program_id(0), pl.num_programs(0), slot_cur_ref, slot_next_ref, ys_ref, ybuf, sem, tm)
    h = h_ref[...] + y
    out_ref[...] = h * _rms_scale(h) * g_ref[...]


def _final(h3, ys_sorted, slots, g, tm):
    t = h3.shape[0]
    n_steps = t // tm
    row = lambda i: (i, 0)
    slots3 = slots.reshape(n_steps, 1, tm)
    return pl.pallas_call(
        functools.partial(_final_kernel, tm=tm),
        grid=(n_steps,),
        in_specs=[pl.BlockSpec((tm, D_MODEL), row)] + _gather_specs(tm, n_steps, lambda i: i)
        + [pl.BlockSpec((1, D_MODEL), lambda i: (0, 0))],
        out_specs=pl.BlockSpec((tm, D_MODEL), row),
        out_shape=jax.ShapeDtypeStruct((t, D_MODEL), F32),
        scratch_shapes=_gather_scratch(tm),
        compiler_params=_params(("arbitrary",)),
        name="final_norm",
    )(h3, slots3, slots3, ys_sorted, g)


def _head_rms(x, gsum_ref, gexp_ref):
    ms = _dot((x * x).astype(BF16), gsum_ref[...])
    r_hi, r_lo = _split2(lax.rsqrt(ms + EPS))
    return _dot(r_hi, gexp_ref[...]) + _dot(r_lo, gexp_ref[...])


def _proj_kvq_kernel(h_ref, slot_cur_ref, slot_next_ref, ys_ref, gkv_ref, gq_ref, wk_ref, wvt_ref, wf_ref, bf_ref,
                     wq_ref, wogt_ref, kgain_ref, qgain_ref, gsum_ref, gexp_ref, sel_ref,
                     h2_ref, kaug_ref, vt_ref, qaug_ref, sgt_ref, cstat_ref, carry_ref, ybuf, sem, *, tm):
    @pl.when(pl.program_id(1) == 0)
    def _():
        carry_ref[...] = jnp.zeros_like(carry_ref)

    step = pl.program_id(0) * pl.num_programs(1) + pl.program_id(1)
    n_steps = pl.num_programs(0) * pl.num_programs(1)
    ymoe = _gathered_rows(step, n_steps, slot_cur_ref, slot_next_ref, ys_ref, ybuf, sem, tm)
    hres = h_ref[...] + ymoe
    h2_ref[...] = hres
    y = hres * _rms_scale(hres)
    a = y * gkv_ref[...]
    ah, al = _split2(a)
    bh = (y * gq_ref[...]).astype(BF16)

    logf = _log_sigmoid(_dot_x3(ah, al, wf_ref) + bf_ref[...])
    ti = lax.broadcasted_iota(jnp.int32, (tm, tm), 0)
    tj = lax.broadcasted_iota(jnp.int32, (tm, tm), 1)
    tril = jnp.where(tj <= ti, 1.0, 0.0).astype(BF16)
    c = _dot_exact_lhs(tril, logf) + carry_ref[...]
    carry_ref[...] = c[tm - 1:tm, :]

    nb = c * (-LOG2E)
    for blk in range(tm // ATT_BLK):
        nb_blk = nb[blk * ATT_BLK:(blk + 1) * ATT_BLK, :]
        cstat_ref[0, blk, 0:1, :] = jnp.max(nb_blk, axis=0, keepdims=True)
        cstat_ref[0, blk, 1:2, :] = jnp.min(nb_blk, axis=0, keepdims=True)
    n0, n1, n2 = _split3(nb)
    lane = lax.broadcasted_iota(jnp.int32, (tm, LANES), 1)
    packed = jnp.where(lane < N_HEADS_B, n0.astype(F32),
                       jnp.where(lane < 2 * N_HEADS_B, pltpu.roll(n1.astype(F32), N_HEADS_B, 1),
                                 pltpu.roll(n2.astype(F32), 2 * N_HEADS_B, 1)))
    packed = jnp.where(lane < 3 * N_HEADS_B, packed, 0.0).astype(BF16)
    extras = _dot(packed, sel_ref[...])

    k = _dot(ah, wk_ref[...])
    kn = k * _head_rms(k, gsum_ref, gexp_ref) * kgain_ref[...]
    q = _dot(bh, wq_ref[...])
    qn = q * _head_rms(q, gsum_ref, gexp_ref) * qgain_ref[...]
    ones = jnp.where((lane >= BIAS_LANE) & (lane < BIAS_LANE + 3), 1.0, 0.0)
    for j in range(N_HEADS_B // 2):
        kj = kn[:, j * LANES:(j + 1) * LANES]
        qj = qn[:, j * LANES:(j + 1) * LANES]
        for half, (kk, qq) in enumerate(((kj, qj), (pltpu.roll(kj, DH_B, 1), pltpu.roll(qj, DH_B, 1)))):
            hd = 2 * j + half
            ex = extras[:, hd * LANES:(hd + 1) * LANES]
            kaug_ref[0, hd] = jnp.where(lane < DH_B, kk, ex).astype(BF16)
            qaug_ref[0, hd] = jnp.where(lane < DH_B, qq, ones).astype(BF16)

    vt = _dot_nt(wvt_ref[...], ah)
    sgt = _sigmoid(_dot_nt(wogt_ref[...], bh))
    for hd in range(N_HEADS_B):
        vt_ref[0, hd] = vt[hd * DH_B:(hd + 1) * DH_B, :].astype(BF16)
    sgt_ref[0] = sgt.astype(BF16)


def _proj_kvq(h1, ys_sorted, slots, gkv, gq, wk, wvt, wf, bf, wq, wogt, kgain, qgain, gsum, gexp, sel,
              bsz, seq, tm):
    ns = seq // tm
    row = lambda b, s: (b * ns + s, 0)
    const2 = lambda b, s: (0, 0)
    const3 = lambda b, s: (0, 0, 0)
    slots3 = slots.reshape(bsz * ns, 1, tm)
    return pl.pallas_call(
        functools.partial(_proj_kvq_kernel, tm=tm),
        grid=(bsz, ns),
        in_specs=[pl.BlockSpec((tm, D_MODEL), row)] + _gather_specs(tm, bsz * ns, lambda b, s: b * ns + s) + [
            pl.BlockSpec((1, D_MODEL), const2),
            pl.BlockSpec((1, D_MODEL), const2),
            pl.BlockSpec((D_MODEL, D_MODEL), const2),
            pl.BlockSpec((D_MODEL, D_MODEL), const2),
            pl.BlockSpec((2, D_MODEL, LANES), const3),
            pl.BlockSpec((1, LANES), const2),
            pl.BlockSpec((D_MODEL, D_MODEL), const2),
            pl.BlockSpec((D_MODEL, D_MODEL), const2),
            pl.BlockSpec((1, D_MODEL), const2),
            pl.BlockSpec((1, D_MODEL), const2),
            pl.BlockSpec((D_MODEL, LANES), const2),
            pl.BlockSpec((LANES, D_MODEL), const2),
            pl.BlockSpec((LANES, N_HEADS_B * LANES), const2),
        ],
        out_specs=[
            pl.BlockSpec((tm, D_MODEL), row),
            pl.BlockSpec((1, N_HEADS_B, tm, LANES), lambda b, s: (b, 0, s, 0)),
            pl.BlockSpec((1, N_HEADS_B, DH_B, tm), lambda b, s: (b, 0, 0, s)),
            pl.BlockSpec((1, N_HEADS_B, tm, LANES), lambda b, s: (b, 0, s, 0)),
            pl.BlockSpec((1, D_MODEL, tm), lambda b, s: (b, 0, s)),
            pl.BlockSpec((1, tm // ATT_BLK, 2, LANES), lambda b, s: (b, s, 0, 0)),
        ],
        out_shape=[
            jax.ShapeDtypeStruct((bsz * seq, D_MODEL), F32),
            jax.ShapeDtypeStruct((bsz, N_HEADS_B, seq, LANES), BF16),
            jax.ShapeDtypeStruct((bsz, N_HEADS_B, DH_B, seq), BF16),
            jax.ShapeDtypeStruct((bsz, N_HEADS_B, seq, LANES), BF16),
            jax.ShapeDtypeStruct((bsz, D_MODEL, seq), BF16),
            jax.ShapeDtypeStruct((bsz, seq // ATT_BLK, 2, LANES), F32),
        ],
        scratch_shapes=[pltpu.VMEM((1, LANES), F32)] + _gather_scratch(tm),
        compiler_params=_params(("arbitrary", "arbitrary")),
        name="proj_kvq",
    )(h1, slots3, slots3, ys_sorted, gkv, gq, wk, wvt, wf, bf, wq, wogt, kgain, qgain, gsum, gexp, sel)


def _attn_kernel(trips_ref, q_ref, k_ref, vt_ref, o_ref, m_scr, acc_scr, sa_scr, sb_scr, xa_scr, xb_scr,
                 p_scr, *, chains):
    b, h, i = pl.program_id(0), pl.program_id(1), pl.program_id(2)
    trips = trips_ref[(b * pl.num_programs(1) + h) * pl.num_programs(2) + i]
    blk = ATT_BLK
    lane = lax.broadcasted_iota(jnp.int32, (blk, LANES), 1)
    k_null = jnp.where(lane == BIAS_LANE, NEG_BIG, 0.0).astype(BF16)

    def key_offset(g, t):
        kb = i * chains + g - t
        return kb, pl.multiple_of(jnp.maximum(kb, 0) * blk, blk)

    all_chains = range(chains)
    ones_rows = jnp.ones((DEN_ROWS, blk), BF16)

    def scores(t, s_ref, x_ref, first, gs=all_chains):
        for g in gs:
            q = q_ref[0, 0, g * blk:(g + 1) * blk, :]
            kb, off = key_offset(g, t)
            kblk = k_ref[0, 0, pl.ds(off, blk), :]
            if not first:
                kblk = jnp.where(kb >= 0, kblk, k_null)
            s = _dot_nt(kblk, q)
            if first:
                ki = lax.broadcasted_iota(jnp.int32, (blk, blk), 0)
                qj = lax.broadcasted_iota(jnp.int32, (blk, blk), 1)
                s = jnp.where(ki <= qj, s, NEG_BIG)
            s_ref[g] = s
            x_ref[g] = jnp.max(s, axis=0, keepdims=True)

    def update(t, s_ref, x_ref, gs=all_chains):
        for g in gs:
            _, off = key_offset(g, t)
            v_den = jnp.concatenate([vt_ref[0, 0, :, pl.ds(off, blk)], ones_rows], axis=0)
            m_old = m_scr[g]
            m_new = jnp.maximum(m_old, x_ref[g])
            p_scr[g] = jnp.exp2(s_ref[g] - m_new).astype(BF16)
            acc_scr[g] = jnp.exp2(m_old - m_new) * acc_scr[g] + _dot(v_den, p_scr[g])
            m_scr[g] = m_new

    m_scr[...] = jnp.full(m_scr.shape, NEG_BIG, F32)
    acc_scr[...] = jnp.zeros(acc_scr.shape, F32)
    scores(0, sa_scr, xa_scr, True)
    steps = trips - 1

    def body(pair, carry):
        t = 2 * pair
        for g in all_chains:
            scores(t + 1, sb_scr, xb_scr, False, [g])
            update(t, sa_scr, xa_scr, [g])
        for g in all_chains:
            scores(t + 2, sa_scr, xa_scr, False, [g])
            update(t + 1, sb_scr, xb_scr, [g])
        return carry

    lax.fori_loop(0, steps >> 1, body, 0)

    @pl.when((steps & 1) == 1)
    def _():
        for g in all_chains:
            scores(steps, sb_scr, xb_scr, False, [g])
            update(steps - 1, sa_scr, xa_scr, [g])
        update(steps, sb_scr, xb_scr)

    @pl.when((steps & 1) == 0)
    def _():
        update(steps, sa_scr, xa_scr)

    for g in all_chains:
        o_ref[0, 0, :, g * blk:(g + 1) * blk] = (acc_scr[g, :DH_B, :] / acc_scr[g, DH_B:DH_B + 1, :]).astype(BF16)


def _attn_trips(cstat, kgain, qgain, chains):
    nbmax = jnp.swapaxes(cstat[:, :, 0, :N_HEADS_B], 1, 2)
    nbmin = jnp.swapaxes(cstat[:, :, 1, :N_HEADS_B], 1, 2)
    nblk = nbmax.shape[-1]
    xb = DH_B * jnp.max(jnp.abs(kgain)) * jnp.max(jnp.abs(qgain))
    slack = 2.0 * xb * 1.02 + 2.0 + SKIP_LOG2
    j = jnp.arange(nblk)
    keep = (nbmax[:, :, None, :] - nbmin[:, :, :, None] + slack >= 0.0) & (j[None, :] <= j[:, None])
    jmin = jnp.min(jnp.where(keep, j[None, :], nblk), axis=-1)
    need = j - jnp.minimum(jmin, j) + 1
    return jnp.max(need.reshape(need.shape[0], need.shape[1], nblk // chains, chains), axis=-1).astype(jnp.int32)


def _attn(qaug, kaug, vt, trips, chains):
    bsz, nh, seq, _ = qaug.shape
    tq = chains * ATT_BLK
    grid_spec = pltpu.PrefetchScalarGridSpec(
        num_scalar_prefetch=1,
        grid=(bsz, nh, seq // tq),
        in_specs=[
            pl.BlockSpec((1, 1, tq, LANES), lambda b, h, i, tr: (b, h, i, 0)),
            pl.BlockSpec((1, 1, seq, LANES), lambda b, h, i, tr: (b, h, 0, 0)),
            pl.BlockSpec((1, 1, DH_B, seq), lambda b, h, i, tr: (b, h, 0, 0)),
        ],
        out_specs=pl.BlockSpec((1, 1, DH_B, tq), lambda b, h, i, tr: (b, h, 0, i)),
        scratch_shapes=[
            pltpu.VMEM((chains, 1, ATT_BLK), F32),
            pltpu.VMEM((chains, DH_B + DEN_ROWS, ATT_BLK), F32),
            pltpu.VMEM((chains, ATT_BLK, ATT_BLK), F32),
            pltpu.VMEM((chains, ATT_BLK, ATT_BLK), F32),
            pltpu.VMEM((chains, 1, ATT_BLK), F32),
            pltpu.VMEM((chains, 1, ATT_BLK), F32),
            pltpu.VMEM((chains, ATT_BLK, ATT_BLK), BF16),
        ],
    )
    return pl.pallas_call(
        functools.partial(_attn_kernel, chains=chains),
        grid_spec=grid_spec,
        out_shape=jax.ShapeDtypeStruct((bsz, nh, DH_B, seq), BF16),
        compiler_params=_params(("parallel", "parallel", "arbitrary")),
        name="attn",
    )(trips.reshape(-1), qaug, kaug, vt)


def _out_b_kernel(ot_ref, sgt_ref, h_ref, wout_ref, gn_ref, wr_ref, br_ref,
                  h3_ref, xs_ref, route_ref, counts_ref, *, tm):
    h_parts = []
    for p in range(tm // ROUTE_PIECE):
        rows = slice(p * ROUTE_PIECE, (p + 1) * ROUTE_PIECE)
        zt = (ot_ref[0, :, rows].astype(F32) * sgt_ref[0, :, rows].astype(F32)).astype(BF16)
        h3 = h_ref[rows, :] + _dot_tn(zt, wout_ref[...])
        h3_ref[rows, :] = h3
        h_parts.append(h3)
    first_step = (pl.program_id(0) == 0) & (pl.program_id(1) == 0)
    _ffn_norm_and_route(h_parts, gn_ref, wr_ref, br_ref, xs_ref, route_ref, counts_ref, first_step)


def _out_b(ot, sgt, h2, wout, gn, wr, br, bsz, seq, tm):
    ns = seq // tm
    row = lambda b, s: (b * ns + s, 0)
    const2 = lambda b, s: (0, 0)
    const3 = lambda b, s: (0, 0, 0)
    t = bsz * seq
    return pl.pallas_call(
        functools.partial(_out_b_kernel, tm=tm),
        grid=(bsz, ns),
        in_specs=[
            pl.BlockSpec((1, D_MODEL, tm), lambda b, s: (b, 0, s)),
            pl.BlockSpec((1, D_MODEL, tm), lambda b, s: (b, 0, s)),
            pl.BlockSpec((tm, D_MODEL), row),
            pl.BlockSpec((D_MODEL, D_MODEL), const2),
            pl.BlockSpec((1, D_MODEL), const2),
            pl.BlockSpec((2, D_MODEL, LANES), const3),
            pl.BlockSpec((1, LANES), const2),
        ],
        out_specs=_route_out_specs(tm, row, const2),
        out_shape=_route_out_shapes(t),
        compiler_params=_params(("arbitrary", "arbitrary")),
        name="out_b",
    )(ot, sgt, h2, wout, gn, wr, br)


def _hi_lo(w):
    hi = w.astype(BF16)
    lo = (w - hi.astype(F32)).astype(BF16)
    return jnp.stack([hi, lo])


def _pad_lanes(w, width=LANES):
    return jnp.pad(w, ((0, 0),) * (w.ndim - 1) + ((0, width - w.shape[-1]),))


def _router_params(w_grp, b_grp, w_exp, b_exp):
    w = _pad_lanes(jnp.concatenate([w_grp, w_exp], axis=-1))
    b = _pad_lanes(jnp.concatenate([b_grp, b_exp], axis=-1)[None, :])
    return _hi_lo(w), b


def _tile(seq, pref):
    t = pref
    while seq % t:
        t //= 2
    return t


def kernel(x, norm_mix, norm_ffn, a_w_in, a_b_gate, a_head_gain, a_w_out, kv_norm, kv_w, kv_b_f, kv_k_gain,
           b_w_in, b_q_gain, b_w_out, moe_w_grp, moe_b_grp, moe_w_exp, moe_b_exp, moe_w_gate, moe_w_up,
           moe_w_down, norm_final):
    bsz, seq, _ = x.shape
    t = bsz * seq
    assert seq % CHUNK == 0 and seq % ATT_BLK == 0
    tm = _tile(seq, 512)
    assert tm % ATT_BLK == 0
    x2d = x.reshape(t, D_MODEL)

    w_in = a_w_in[0]
    wq = w_in[:, :QK_A].astype(BF16)
    wk = w_in[:, QK_A:2 * QK_A].astype(BF16)
    wv = w_in[:, 2 * QK_A:2 * QK_A + D_MODEL].astype(BF16)
    wo = w_in[:, 2 * QK_A + D_MODEL:2 * QK_A + 2 * D_MODEL].astype(BF16)
    wgate = w_in[:, 2 * QK_A + 2 * D_MODEL:]
    wgc = _hi_lo(_pad_lanes(wgate))
    wgr = _hi_lo(wgate.T)
    bgc = _pad_lanes(a_b_gate[0][None, :])
    bgr = a_b_gate[0][:, None]
    q, k, v, o, gc, gr = _proj_a(x2d, norm_mix[0][None, :], wq, wk, wv, wo, wgc, wgr, bgc, bgr, tm)
    hn = _mlstm(q, k, v, gc, gr, bsz, seq)
    wr0, br0 = _router_params(moe_w_grp[0], moe_b_grp[0], moe_w_exp[0], moe_b_exp[0])
    h1, xs1, route1, counts1 = _out_a(hn, o, x2d, a_head_gain[0][None, :], a_w_out[0].astype(BF16),
                                      norm_ffn[0][None, :], wr0, br0, tm)
    ys1, slots1 = _moe(xs1, route1, counts1, wr0[0], br0, moe_w_gate[0].astype(BF16), moe_w_up[0].astype(BF16),
              moe_w_down[0].astype(BF16))

    wkk = kv_w[:, :D_MODEL].astype(BF16)
    wvt = kv_w[:, D_MODEL:2 * D_MODEL].T.astype(BF16)
    wf = _hi_lo(_pad_lanes(kv_w[:, 2 * D_MODEL:]))
    bf = _pad_lanes(kv_b_f[None, :])
    wq1 = b_w_in[0][:, :D_MODEL].astype(BF16)
    wogt = b_w_in[0][:, D_MODEL:].T.astype(BF16)
    kgain = jnp.tile(kv_k_gain, N_HEADS_B)[None, :]
    qgain = jnp.tile(b_q_gain[0], N_HEADS_B)[None, :] * (DH_B ** -0.5 * LOG2E)
    head_of = jnp.arange(D_MODEL) // DH_B
    gsum = (head_of[:, None] == jnp.arange(LANES)[None, :]).astype(BF16) * (1.0 / DH_B)
    gexp = (jnp.arange(LANES)[:, None] == head_of[None, :]).astype(BF16)
    src = jnp.arange(LANES)[:, None]
    dst = jnp.arange(N_HEADS_B * LANES)[None, :]
    sel = ((src < 3 * N_HEADS_B) & (dst == (src % N_HEADS_B) * LANES + BIAS_LANE + src // N_HEADS_B)).astype(BF16)
    h2, kaug, vt, qaug, sgt, cstat = _proj_kvq(h1, ys1, slots1, kv_norm[None, :], norm_mix[1][None, :], wkk, wvt, wf, bf,
                                               wq1, wogt, kgain, qgain, gsum, gexp, sel, bsz, seq, tm)
    chains = math.gcd(ATT_CHAINS, seq // ATT_BLK)
    ot = _attn(qaug, kaug, vt, _attn_trips(cstat, kgain, qgain, chains), chains)
    wr1, br1 = _router_params(moe_w_grp[1], moe_b_grp[1], moe_w_exp[1], moe_b_exp[1])
    h3, xs3, route3, counts3 = _out_b(ot.reshape(bsz, D_MODEL, seq), sgt, h2, b_w_out[0].astype(BF16),
                                      norm_ffn[1][None, :], wr1, br1, bsz, seq, tm)
    ys3, slots3 = _moe(xs3, route3, counts3, wr1[0], br1, moe_w_gate[1].astype(BF16), moe_w_up[1].astype(BF16),
              moe_w_down[1].astype(BF16))
    out = _final(h3, ys3, slots3, norm_final[None, :], tm)
    return out.reshape(bsz, seq, D_MODEL)
```

```python
import functools
import math

import jax
import jax.numpy as jnp
from jax import lax
from jax.experimental import pallas as pl
from jax.experimental.pallas import tpu as pltpu

F32 = jnp.float32
BF16 = jnp.bfloat16

D_MODEL = 1024
EPS = 1e-6

N_HEADS_A = 4
DV_A = D_MODEL // N_HEADS_A
DQK_A = DV_A // 2
QK_A = N_HEADS_A * DQK_A
GATE_CAP = 15.0
CHUNK = 128
CHUNK_SHIFT = CHUNK.bit_length() - 1

DH_B = 64
N_HEADS_B = D_MODEL // DH_B
LOG2E = 1.4426950408889634
BIAS_LANE = DH_B
NEG_BIG = -1e30
ATT_BLK = 256
ATT_CHAINS = 8
SKIP_LOG2 = 160.0
SKIP_REL_MARGIN = 1.02
SKIP_ABS_MARGIN = 2.0
DEN_ROWS = 16

N_GROUPS = 4
EXP_PER_GROUP = 4
N_EXPERTS = N_GROUPS * EXP_PER_GROUP
D_EXPERT = D_MODEL // 2
ROUTE_LANE0 = N_GROUPS
PAIR_LO = (0, 0, 0, 1, 1, 2)
PAIR_HI = (1, 2, 3, 2, 3, 3)
N_BINS = N_GROUPS * len(PAIR_LO)
MOE_TILE = 256
ROW_SUB = 8
ROWS_PER_STEP = 1024
ROUTE_PIECE = 256
RANK_BASE = 256.0

LANES = 128
VMEM_LIMIT = 56 * 1024 * 1024


def _params(sem, vmem=VMEM_LIMIT):
    return pltpu.CompilerParams(dimension_semantics=sem, vmem_limit_bytes=vmem)


def _dot(a, b):
    return jnp.dot(a, b, preferred_element_type=F32)


def _dot_nt(a, b):
    return lax.dot_general(a, b, (((1,), (1,)), ((), ())), preferred_element_type=F32)


def _dot_tn(a, b):
    return lax.dot_general(a, b, (((0,), (0,)), ((), ())), preferred_element_type=F32)


def _split2(x):
    hi = x.astype(BF16)
    lo = (x - hi.astype(F32)).astype(BF16)
    return hi, lo


def _split3(x):
    hi = x.astype(BF16)
    r = x - hi.astype(F32)
    mid = r.astype(BF16)
    lo = (r - mid.astype(F32)).astype(BF16)
    return hi, mid, lo


def _dot_x3(x_hi, x_lo, w_ref):
    return _dot(x_hi, w_ref[0]) + _dot(x_lo, w_ref[0]) + _dot(x_hi, w_ref[1])


def _dot_nt_x3(w_ref, x_hi, x_lo):
    return _dot_nt(w_ref[0], x_hi) + _dot_nt(w_ref[0], x_lo) + _dot_nt(w_ref[1], x_hi)


def _dot_exact_rhs(a, b_exact):
    a0, a1, a2 = _split3(a)
    return _dot(a0, b_exact) + _dot(a1, b_exact) + _dot(a2, b_exact)


def _dot_exact_lhs(a_exact, b):
    b0, b1, b2 = _split3(b)
    return _dot(a_exact, b0) + _dot(a_exact, b1) + _dot(a_exact, b2)


def _rms_scale(x):
    return lax.rsqrt(jnp.mean(x * x, axis=-1, keepdims=True) + EPS)


def _log_sigmoid(z):
    return jnp.minimum(z, 0.0) - jnp.log(1.0 + jnp.exp(-jnp.abs(z)))


def _sigmoid(z):
    return 1.0 / (1.0 + jnp.exp(-z))


def _softcap(z):
    return GATE_CAP * jnp.tanh(z / GATE_CAP)


def _proj_a_kernel(x_ref, g_ref, wq_ref, wk_ref, wv_ref, wo_ref, wgc_ref, wgr_ref, bgc_ref, bgr_ref,
                   q_ref, k_ref, v_ref, o_ref, gc_ref, gr_ref, *, tm):
    x = x_ref[...]
    xn = x * _rms_scale(x) * g_ref[...]
    xh, xl = _split2(xn)
    q_ref[...] = _dot(xh, wq_ref[...]).astype(BF16)
    k_ref[...] = (_dot(xh, wk_ref[...]) * (DQK_A ** -0.5)).astype(BF16)
    v_ref[...] = _dot(xh, wv_ref[...]).astype(BF16)
    o_ref[...] = _dot(xh, wo_ref[...]).astype(BF16)

    zc = _softcap(_dot_x3(xh, xl, wgc_ref) + bgc_ref[...])
    zr = _softcap(_dot_nt_x3(wgr_ref, xh, xl) + bgr_ref[...])
    lane = lax.broadcasted_iota(jnp.int32, zc.shape, 1)
    sub = lax.broadcasted_iota(jnp.int32, zr.shape, 0)
    vc = jnp.where(lane < N_HEADS_A, zc, _log_sigmoid(zc))
    vr = jnp.where(sub < N_HEADS_A, zr, _log_sigmoid(zr))
    ti = lax.broadcasted_iota(jnp.int32, (tm, tm), 0)
    tj = lax.broadcasted_iota(jnp.int32, (tm, tm), 1)
    same = (ti >> CHUNK_SHIFT) == (tj >> CHUNK_SHIFT)
    tril = jnp.where(same & (tj <= ti), 1.0, 0.0).astype(BF16)
    triu = jnp.where(same & (ti <= tj), 1.0, 0.0).astype(BF16)
    cc = _dot_exact_lhs(tril, vc)
    cr = _dot_exact_rhs(vr, triu)
    gc = jnp.where(lane < N_HEADS_A, vc, cc)
    gr = jnp.where(sub < N_HEADS_A, vr, cr)
    a_rows = gr[:N_HEADS_A, :] - gr[N_HEADS_A:, :]
    ci = lax.broadcasted_iota(jnp.int32, (CHUNK, CHUNK), 0)
    cj = lax.broadcasted_iota(jnp.int32, (CHUNK, CHUNK), 1)
    for h in range(N_HEADS_A):
        col = jnp.concatenate(
            [jnp.max(jnp.where(cj <= ci, a_rows[h:h + 1, c * CHUNK:(c + 1) * CHUNK], -jnp.inf), axis=-1, keepdims=True)
             for c in range(tm // CHUNK)], axis=0)
        gc = jnp.where(lane == 2 * N_HEADS_A + h, col, gc)
    gc_ref[...] = gc
    gr_ref[...] = gr


def _proj_a(x2d, g, w_in, wgc, wgr, bgc, bgr, tm):
    assert 2 * QK_A == D_MODEL
    t = x2d.shape[0]
    row = lambda i: (i, 0)
    const2 = lambda i: (0, 0)
    const3 = lambda i: (0, 0, 0)
    return pl.pallas_call(
        functools.partial(_proj_a_kernel, tm=tm),
        grid=(t // tm,),
        in_specs=[
            pl.BlockSpec((tm, D_MODEL), row),
            pl.BlockSpec((1, D_MODEL), const2),
            pl.BlockSpec((D_MODEL, QK_A), lambda i: (0, 0)),
            pl.BlockSpec((D_MODEL, QK_A), lambda i: (0, 1)),
            pl.BlockSpec((D_MODEL, D_MODEL), lambda i: (0, 1)),
            pl.BlockSpec((D_MODEL, D_MODEL), lambda i: (0, 2)),
            pl.BlockSpec((2, D_MODEL, LANES), const3),
            pl.BlockSpec((2, 8, D_MODEL), const3),
            pl.BlockSpec((1, LANES), const2),
            pl.BlockSpec((8, 1), const2),
        ],
        out_specs=[
            pl.BlockSpec((tm, QK_A), row),
            pl.BlockSpec((tm, QK_A), row),
            pl.BlockSpec((tm, D_MODEL), row),
            pl.BlockSpec((tm, D_MODEL), row),
            pl.BlockSpec((tm, LANES), row),
            pl.BlockSpec((8, tm), lambda i: (0, i)),
        ],
        out_shape=[
            jax.ShapeDtypeStruct((t, QK_A), BF16),
            jax.ShapeDtypeStruct((t, QK_A), BF16),
            jax.ShapeDtypeStruct((t, D_MODEL), BF16),
            jax.ShapeDtypeStruct((t, D_MODEL), BF16),
            jax.ShapeDtypeStruct((t, LANES), F32),
            jax.ShapeDtypeStruct((8, t), F32),
        ],
        compiler_params=_params(("parallel",)),
        name="proj_a",
    )(x2d, g, w_in, w_in, w_in, w_in, wgc, wgr, bgc, bgr)


def _mlstm_kernel(q_ref, k_ref, v_ref, gc_ref, gr_ref, h_ref, c_scr, m_scr):
    @pl.when(pl.program_id(1) == 0)
    def _():
        c_scr[...] = jnp.zeros_like(c_scr)
        m_scr[...] = jnp.zeros_like(m_scr)

    ti = lax.broadcasted_iota(jnp.int32, (CHUNK, CHUNK), 0)
    si = lax.broadcasted_iota(jnp.int32, (CHUNK, CHUNK), 1)
    causal = si <= ti
    gc = gc_ref[...]
    gr = gr_ref[...]
    heads = range(N_HEADS_A)
    ones_blk = jnp.ones((CHUNK, LANES), BF16)
    ones_sq = jnp.ones((DV_A, LANES), BF16)
    qs = [q_ref[:, h * DQK_A:(h + 1) * DQK_A] for h in heads]
    ks = [k_ref[:, h * DQK_A:(h + 1) * DQK_A] for h in heads]
    vs = [jnp.concatenate([v_ref[:, h * DV_A:(h + 1) * DV_A], ones_blk], axis=1) for h in heads]

    gates = []
    for h in heads:
        b_c = gc[:, N_HEADS_A + h:N_HEADS_A + h + 1]
        a_c = gc[:, h:h + 1] - b_c
        amax_c = gc[:, 2 * N_HEADS_A + h:2 * N_HEADS_A + h + 1]
        a_r = gr[h:h + 1, :] - gr[N_HEADS_A + h:N_HEADS_A + h + 1, :]
        b_end = b_c[CHUNK - 1:CHUNK, :]
        m_run = m_scr[h][:, 0:1]
        mx_c = jnp.maximum(m_run, amax_c)
        mx_end = mx_c[CHUNK - 1:CHUNK, :]
        d_mat = jnp.where(causal, jnp.exp(a_r - mx_c), 0.0)
        w_inter = jnp.broadcast_to(jnp.exp(m_run - mx_c), (CHUNK, LANES))
        floor = jnp.broadcast_to(jnp.exp(-(b_c + mx_c)), (CHUNK, LANES))
        w_c = jnp.exp(a_c - mx_end)
        decay = jnp.exp(m_run - mx_end)
        gates.append((d_mat, w_inter, floor, b_end + mx_end, w_c, decay))

    kws = [(ks[h].astype(F32) * gates[h][4]).astype(BF16) for h in heads]
    s_raw = [_dot_nt(qs[h], ks[h]) for h in heads]
    inter = [_dot(qs[h], c_scr[h].astype(BF16)) for h in heads]
    c_upd = [_dot_tn(kws[h], vs[h]) for h in heads]
    for h in heads:
        d_mat, w_inter, floor, m_new, w_c, decay = gates[h]
        tot = _dot((s_raw[h] * d_mat).astype(BF16), vs[h])
        den = tot[:, DV_A:] + w_inter * inter[h][:, DV_A:]
        rden = 1.0 / jnp.maximum(jnp.abs(den), floor)
        halves = [(tot[:, j * LANES:(j + 1) * LANES] + w_inter * inter[h][:, j * LANES:(j + 1) * LANES]) * rden
                  for j in range(DV_A // LANES)]
        hh = jnp.concatenate(halves, axis=1)
        ms = _dot((hh * hh).astype(BF16), ones_sq) * (1.0 / DV_A)
        rs = lax.rsqrt(ms + EPS)
        for j in range(DV_A // LANES):
            h_ref[:, h * DV_A + j * LANES:h * DV_A + (j + 1) * LANES] = (halves[j] * rs).astype(BF16)
        c_scr[h] = decay * c_scr[h] + c_upd[h]
        m_scr[h] = jnp.broadcast_to(m_new, (1, LANES))


def _mlstm(q, k, v, gc, gr, bsz, seq):
    nc = seq // CHUNK
    row = lambda b, c: (b * nc + c, 0)
    return pl.pallas_call(
        _mlstm_kernel,
        grid=(bsz, nc),
        in_specs=[
            pl.BlockSpec((CHUNK, QK_A), row),
            pl.BlockSpec((CHUNK, QK_A), row),
            pl.BlockSpec((CHUNK, D_MODEL), row),
            pl.BlockSpec((CHUNK, LANES), row),
            pl.BlockSpec((8, CHUNK), lambda b, c: (0, b * nc + c)),
        ],
        out_specs=pl.BlockSpec((CHUNK, D_MODEL), row),
        out_shape=jax.ShapeDtypeStruct((bsz * seq, D_MODEL), BF16),
        scratch_shapes=[
            pltpu.VMEM((N_HEADS_A, DQK_A, DV_A + LANES), F32),
            pltpu.VMEM((N_HEADS_A, 1, LANES), F32),
        ],
        compiler_params=_params(("arbitrary", "arbitrary")),
        name="mlstm",
    )(q, k, v, gc, gr)


def _route_bins(logits):
    lane = lax.broadcasted_iota(jnp.int32, logits.shape, 1)
    big = jnp.int32(10 ** 6)

    def top(mask):
        mx = jnp.max(jnp.where(mask, logits, -jnp.inf), axis=-1, keepdims=True)
        return jnp.min(jnp.where(mask & (logits == mx), lane, big), axis=-1, keepdims=True)

    gidx = top(lane < N_GROUPS)
    lo = ROUTE_LANE0 + EXP_PER_GROUP * gidx
    emask = (lane >= lo) & (lane < lo + EXP_PER_GROUP)
    i1 = top(emask)
    i2 = top(emask & (lane != i1))
    p_lo = jnp.minimum(i1, i2) - lo
    p_hi = jnp.maximum(i1, i2) - lo
    pair = jnp.where(p_lo == 0, p_hi - 1, jnp.where(p_lo == 1, p_hi + 1, len(PAIR_LO) - 1))
    return gidx * len(PAIR_LO) + pair


def _pair_weights(logits, grp, e_lo, e_hi):
    lane = lax.broadcasted_iota(jnp.int32, logits.shape, 1)

    def pick(idx):
        return jnp.sum(jnp.where(lane == idx, logits, 0.0), axis=-1, keepdims=True)

    def softmax_stats(mask):
        mx = jnp.max(jnp.where(mask, logits, -jnp.inf), axis=-1, keepdims=True)
        return mx, jnp.sum(jnp.where(mask, jnp.exp(logits - mx), 0.0), axis=-1, keepdims=True)

    gmax, gsum = softmax_stats(lane < N_GROUPS)
    g_p = jnp.exp(pick(grp) - gmax) / gsum
    lo = ROUTE_LANE0 + EXP_PER_GROUP * grp
    emax, esum = softmax_stats((lane >= lo) & (lane < lo + EXP_PER_GROUP))
    p_lo = jnp.exp(pick(ROUTE_LANE0 + e_lo) - emax) / esum
    p_hi = jnp.exp(pick(ROUTE_LANE0 + e_hi) - emax) / esum
    return g_p * (p_lo / (p_lo + p_hi)), g_p * (p_hi / (p_lo + p_hi))


def _store_rows(rows_ref, x, tm, tok0=0):
    for c in range(ROW_SUB):
        rows_ref[pl.ds(tok0 * ROW_SUB + c, tm, stride=ROW_SUB), :] = x[:, c * LANES:(c + 1) * LANES]


def _load_rows(rows_ref, tm):
    return jnp.concatenate([rows_ref[pl.ds(c, tm, stride=ROW_SUB), :] for c in range(ROW_SUB)], axis=-1)


def _ffn_norm_and_route(h_parts, gn_ref, wr_ref, br_ref, xs_ref, route_ref, counts_ref, first_step):
    @pl.when(first_step)
    def _():
        counts_ref[...] = jnp.zeros_like(counts_ref)

    tp = h_parts[0].shape[0]
    pieces = range(len(h_parts))
    xns = [h * _rms_scale(h) * gn_ref[...] for h in h_parts]
    for p in pieces:
        _store_rows(xs_ref, xns[p], tp, p * tp)
    splits = [_split2(xn) for xn in xns]
    logits = [_dot_x3(xh, xl, wr_ref) + br_ref[...] for xh, xl in splits]
    bins = [_route_bins(lg) for lg in logits]
    lane = lax.broadcasted_iota(jnp.int32, (tp, LANES), 1)
    onehots = [jnp.where(lane == b, 1.0, 0.0) for b in bins]
    ti = lax.broadcasted_iota(jnp.int32, (tp, tp), 0)
    tj = lax.broadcasted_iota(jnp.int32, (tp, tp), 1)
    earlier = jnp.where(tj < ti, 1.0, 0.0).astype(BF16)
    befores = [_dot(earlier, oh.astype(BF16)) for oh in onehots]
    pick = jnp.where(lax.broadcasted_iota(jnp.int32, (8, LANES), 0) == lax.broadcasted_iota(jnp.int32, (8, LANES), 1),
                     1.0, 0.0).astype(BF16)
    counts = counts_ref[...]
    for p in pieces:
        rank = jnp.sum(onehots[p] * (befores[p] + counts), axis=-1, keepdims=True)
        counts = counts + jnp.sum(onehots[p], axis=0, keepdims=True)
        rank_hi = jnp.floor(rank * (1.0 / RANK_BASE))
        cols = jnp.where(lane == 0, bins[p].astype(F32),
                         jnp.where(lane == 1, rank_hi, jnp.where(lane == 2, rank - RANK_BASE * rank_hi, 0.0)))
        route_ref[:, p * tp:(p + 1) * tp] = _dot_nt(pick, cols.astype(BF16))
    counts_ref[...] = counts


def _out_a_kernel(hn_ref, o_ref, x_ref, hg_ref, wout_ref, gn_ref, wr_ref, br_ref,
                  h1_ref, xs_ref, route_ref, counts_ref, *, tm):
    h_parts = []
    for p in range(tm // ROUTE_PIECE):
        rows = slice(p * ROUTE_PIECE, (p + 1) * ROUTE_PIECE)
        z = (hn_ref[rows, :].astype(F32) * hg_ref[...] * _sigmoid(o_ref[rows, :].astype(F32))).astype(BF16)
        h1 = x_ref[rows, :] + _dot(z, wout_ref[...])
        h1_ref[rows, :] = h1
        h_parts.append(h1)
    _ffn_norm_and_route(h_parts, gn_ref, wr_ref, br_ref, xs_ref, route_ref, counts_ref, pl.program_id(0) == 0)


def _route_out_specs(tm, row, const2):
    return [
        pl.BlockSpec((tm, D_MODEL), row),
        pl.BlockSpec((tm * ROW_SUB, LANES), row),
        pl.BlockSpec((8, tm), lambda *idx: (0, row(*idx)[0])),
        pl.BlockSpec((1, LANES), const2),
    ]


def _route_out_shapes(t):
    return [
        jax.ShapeDtypeStruct((t, D_MODEL), F32),
        jax.ShapeDtypeStruct((t * ROW_SUB, LANES), F32),
        jax.ShapeDtypeStruct((8, t), F32),
        jax.ShapeDtypeStruct((1, LANES), F32),
    ]


def _out_a(hn, o, x2d, hg, wout, gn, wr, br, tm):
    t = x2d.shape[0]
    row = lambda i: (i, 0)
    const2 = lambda i: (0, 0)
    const3 = lambda i: (0, 0, 0)
    return pl.pallas_call(
        functools.partial(_out_a_kernel, tm=tm),
        grid=(t // tm,),
        in_specs=[
            pl.BlockSpec((tm, D_MODEL), row),
            pl.BlockSpec((tm, D_MODEL), row),
            pl.BlockSpec((tm, D_MODEL), row),
            pl.BlockSpec((1, D_MODEL), const2),
            pl.BlockSpec((D_MODEL, D_MODEL), const2),
            pl.BlockSpec((1, D_MODEL), const2),
            pl.BlockSpec((2, D_MODEL, LANES), const3),
            pl.BlockSpec((1, LANES), const2),
        ],
        out_specs=_route_out_specs(tm, row, const2),
        out_shape=_route_out_shapes(t),
        compiler_params=_params(("arbitrary",)),
        name="out_a",
    )(hn, o, x2d, hg, wout, gn, wr, br)


def _dispatch_kernel(slot_ref, src_ref, zeros_ref, dst_ref, sem, *, td):
    del zeros_ref

    def body(j, carry):
        row = pl.multiple_of(slot_ref[0, 0, j] * ROW_SUB, ROW_SUB)
        pltpu.make_async_copy(src_ref.at[pl.ds(pl.multiple_of(j * ROW_SUB, ROW_SUB), ROW_SUB)],
                              dst_ref.at[pl.ds(row, ROW_SUB)], sem).start()
        return carry

    lax.fori_loop(0, td, body, 0, unroll=8)
    pltpu.make_async_copy(src_ref, dst_ref.at[pl.ds(0, td * ROW_SUB)], sem).wait()


def _dispatch(slots, src, n_dst_tokens):
    t = slots.shape[0]
    td = math.gcd(ROWS_PER_STEP, t)
    any_spec = pl.BlockSpec(memory_space=pl.ANY)
    return pl.pallas_call(
        functools.partial(_dispatch_kernel, td=td),
        grid=(t // td,),
        in_specs=[
            pl.BlockSpec((1, 1, td), lambda i: (i, 0, 0), memory_space=pltpu.SMEM),
            pl.BlockSpec((td * ROW_SUB, LANES), lambda i: (i, 0)),
            any_spec,
        ],
        out_specs=any_spec,
        scratch_shapes=[pltpu.SemaphoreType.DMA(())],
        out_shape=jax.ShapeDtypeStruct((n_dst_tokens * ROW_SUB, LANES), F32),
        input_output_aliases={2: 0},
        compiler_params=_params(("arbitrary",)),
        name="moe_dispatch",
    )(slots.reshape(t // td, 1, td), src, jnp.zeros((n_dst_tokens * ROW_SUB, LANES), F32))


def _moe_kernel(elo_ref, ehi_ref, grp_ref, valid_ref, xs_ref, wr_ref, br_ref,
                wg_lo_ref, wu_lo_ref, wd_lo_ref, wg_hi_ref, wu_hi_ref, wd_hi_ref, ys_ref):
    i = pl.program_id(0)

    @pl.when(valid_ref[i] == 0)
    def _():
        ys_ref[...] = jnp.zeros_like(ys_ref)

    @pl.when(valid_ref[i] != 0)
    def _():
        x = _load_rows(xs_ref, MOE_TILE).astype(BF16)
        w_lo, w_hi = _pair_weights(_dot(x, wr_ref[...]) + br_ref[...], grp_ref[i], elo_ref[i], ehi_ref[i])
        y = None
        for wg_ref, wu_ref, wd_ref, w in ((wg_lo_ref, wu_lo_ref, wd_lo_ref, w_lo),
                                           (wg_hi_ref, wu_hi_ref, wd_hi_ref, w_hi)):
            g = _dot(x, wg_ref[0])
            u = _dot(x, wu_ref[0])
            d = _dot((g * _sigmoid(g) * u * w).astype(BF16), wd_ref[0])
            y = d if y is None else y + d
        _store_rows(ys_ref, y, MOE_TILE)


def _moe_plan(route, counts, n_tiles):
    cnt = counts[0, :N_BINS].astype(jnp.int32)
    padded = (cnt + (MOE_TILE - 1)) // MOE_TILE * MOE_TILE
    ends = jnp.cumsum(padded)
    starts = ends - padded
    rank = (route[1] * RANK_BASE + route[2]).astype(jnp.int32)
    slots = starts[route[0].astype(jnp.int32)] + rank
    last_tile = ends[-1] // MOE_TILE - 1
    tile = jnp.arange(n_tiles, dtype=jnp.int32)
    first_row = jnp.minimum(tile, last_tile) * MOE_TILE
    tbin = jnp.sum((ends[None, :] <= first_row[:, None]).astype(jnp.int32), axis=1)
    grp = tbin // len(PAIR_LO)
    pair = tbin % len(PAIR_LO)
    elo = grp * EXP_PER_GROUP + jnp.asarray(PAIR_LO, jnp.int32)[pair]
    ehi = grp * EXP_PER_GROUP + jnp.asarray(PAIR_HI, jnp.int32)[pair]
    valid = (tile <= last_tile).astype(jnp.int32)
    return slots, elo, ehi, grp, valid


def _moe_experts(xs_sorted, elo, ehi, grp, valid, wr, br, wg, wu, wd):
    n_tiles = xs_sorted.shape[0] // (MOE_TILE * ROW_SUB)
    rows = pl.BlockSpec((MOE_TILE * ROW_SUB, LANES), lambda i, *_: (i, 0))
    lo3 = lambda i, elo, ehi, grp, valid: (elo[i], 0, 0)
    hi3 = lambda i, elo, ehi, grp, valid: (ehi[i], 0, 0)
    const2 = lambda i, *_: (0, 0)
    up_shape, down_shape = (1, D_MODEL, D_EXPERT), (1, D_EXPERT, D_MODEL)
    return pl.pallas_call(
        _moe_kernel,
        grid_spec=pltpu.PrefetchScalarGridSpec(
            num_scalar_prefetch=4,
            grid=(n_tiles,),
            in_specs=[
                rows,
                pl.BlockSpec((D_MODEL, LANES), const2),
                pl.BlockSpec((1, LANES), const2),
                pl.BlockSpec(up_shape, lo3), pl.BlockSpec(up_shape, lo3), pl.BlockSpec(down_shape, lo3),
                pl.BlockSpec(up_shape, hi3), pl.BlockSpec(up_shape, hi3), pl.BlockSpec(down_shape, hi3),
            ],
            out_specs=rows,
        ),
        out_shape=jax.ShapeDtypeStruct(xs_sorted.shape, F32),
        compiler_params=_params(("arbitrary",)),
        name="moe_experts",
    )(elo, ehi, grp, valid, xs_sorted, wr, br, wg, wu, wd, wg, wu, wd)


def _moe(xs, route, counts, wr, br, wg, wu, wd):
    t = route.shape[1]
    n_sorted = t + N_BINS * MOE_TILE
    slots, elo, ehi, grp, valid = _moe_plan(route, counts, n_sorted // MOE_TILE)
    xs_sorted = _dispatch(slots, xs, n_sorted)
    return _moe_experts(xs_sorted, elo, ehi, grp, valid, wr, br, wg, wu, wd), slots


def _gather_start(slot_ref, src_ref, buf_ref, sem, tm):
    def body(j, carry):
        row = pl.multiple_of(slot_ref[0, 0, j] * ROW_SUB, ROW_SUB)
        pltpu.make_async_copy(src_ref.at[pl.ds(row, ROW_SUB)],
                              buf_ref.at[pl.ds(pl.multiple_of(j * ROW_SUB, ROW_SUB), ROW_SUB)], sem).start()
        return carry

    lax.fori_loop(0, tm, body, 0, unroll=8)


def _gathered_rows(step, n_steps, slot_cur_ref, slot_next_ref, src_ref, buf_ref, sem, tm):
    cur = step & 1

    @pl.when(step == 0)
    def _():
        _gather_start(slot_cur_ref, src_ref, buf_ref.at[0], sem.at[0], tm)

    @pl.when(step + 1 < n_steps)
    def _():
        _gather_start(slot_next_ref, src_ref, buf_ref.at[1 - cur], sem.at[1 - cur], tm)

    pltpu.make_async_copy(src_ref.at[pl.ds(0, tm * ROW_SUB)], buf_ref.at[cur], sem.at[cur]).wait()
    return _load_rows(buf_ref.at[cur], tm)


def _gather_specs(tm, n_steps, step_of):
    cur = lambda *idx: (step_of(*idx), 0, 0)
    nxt = lambda *idx: (jnp.minimum(step_of(*idx) + 1, n_steps - 1), 0, 0)
    return [pl.BlockSpec((1, 1, tm), cur, memory_space=pltpu.SMEM),
            pl.BlockSpec((1, 1, tm), nxt, memory_space=pltpu.SMEM),
            pl.BlockSpec(memory_space=pl.ANY)]


def _gather_scratch(tm):
    return [pltpu.VMEM((2, tm * ROW_SUB, LANES), F32), pltpu.SemaphoreType.DMA((2,))]


def _final_kernel(h_ref, slot_cur_ref, slot_next_ref, ys_ref, g_ref, out_ref, ybuf, sem, *, tm):
    y = _gathered_rows(pl.program_id(0), pl.num_programs(0), slot_cur_ref, slot_next_ref, ys_ref, ybuf, sem, tm)
    h = h_ref[...] + y
    out_ref[...] = h * _rms_scale(h) * g_ref[...]


def _final(h3, ys_sorted, slots, g, tm):
    t = h3.shape[0]
    n_steps = t // tm
    row = lambda i: (i, 0)
    slots3 = slots.reshape(n_steps, 1, tm)
    return pl.pallas_call(
        functools.partial(_final_kernel, tm=tm),
        grid=(n_steps,),
        in_specs=[pl.BlockSpec((tm, D_MODEL), row)] + _gather_specs(tm, n_steps, lambda i: i)
        + [pl.BlockSpec((1, D_MODEL), lambda i: (0, 0))],
        out_specs=pl.BlockSpec((tm, D_MODEL), row),
        out_shape=jax.ShapeDtypeStruct((t, D_MODEL), F32),
        scratch_shapes=_gather_scratch(tm),
        compiler_params=_params(("arbitrary",)),
        name="final_norm",
    )(h3, slots3, slots3, ys_sorted, g)


def _head_rms(x, gsum_ref, gexp_ref):
    ms = _dot((x * x).astype(BF16), gsum_ref[...])
    r_hi, r_lo = _split2(lax.rsqrt(ms + EPS))
    return _dot(r_hi, gexp_ref[...]) + _dot(r_lo, gexp_ref[...])


def _proj_kvq_kernel(h_ref, slot_cur_ref, slot_next_ref, ys_ref, gkv_ref, gq_ref, wk_ref, wvt_ref, wf_ref, bf_ref,
                     wq_ref, wogt_ref, kgain_ref, qgain_ref, gsum_ref, gexp_ref, sel_ref,
                     h2_ref, kaug_ref, vt_ref, qaug_ref, sgt_ref, cstat_ref, carry_ref, ybuf, sem, *, tm):
    @pl.when(pl.program_id(1) == 0)
    def _():
        carry_ref[...] = jnp.zeros_like(carry_ref)

    step = pl.program_id(0) * pl.num_programs(1) + pl.program_id(1)
    n_steps = pl.num_programs(0) * pl.num_programs(1)
    ymoe = _gathered_rows(step, n_steps, slot_cur_ref, slot_next_ref, ys_ref, ybuf, sem, tm)
    hres = h_ref[...] + ymoe
    h2_ref[...] = hres
    y = hres * _rms_scale(hres)
    a = y * gkv_ref[...]
    ah, al = _split2(a)
    bh = (y * gq_ref[...]).astype(BF16)

    logf = _log_sigmoid(_dot_x3(ah, al, wf_ref) + bf_ref[...])
    ti = lax.broadcasted_iota(jnp.int32, (tm, tm), 0)
    tj = lax.broadcasted_iota(jnp.int32, (tm, tm), 1)
    tril = jnp.where(tj <= ti, 1.0, 0.0).astype(BF16)
    c = _dot_exact_lhs(tril, logf) + carry_ref[...]
    carry_ref[...] = c[tm - 1:tm, :]

    nb = c * (-LOG2E)
    for blk in range(tm // ATT_BLK):
        nb_blk = nb[blk * ATT_BLK:(blk + 1) * ATT_BLK, :]
        cstat_ref[0, blk, 0:1, :] = jnp.max(nb_blk, axis=0, keepdims=True)
        cstat_ref[0, blk, 1:2, :] = jnp.min(nb_blk, axis=0, keepdims=True)
    n0, n1, n2 = _split3(nb)
    lane = lax.broadcasted_iota(jnp.int32, (tm, LANES), 1)
    packed = jnp.where(lane < N_HEADS_B, n0.astype(F32),
                       jnp.where(lane < 2 * N_HEADS_B, pltpu.roll(n1.astype(F32), N_HEADS_B, 1),
                                 pltpu.roll(n2.astype(F32), 2 * N_HEADS_B, 1)))
    packed = jnp.where(lane < 3 * N_HEADS_B, packed, 0.0).astype(BF16)
    extras = _dot(packed, sel_ref[...])

    k = _dot(ah, wk_ref[...])
    kn = k * _head_rms(k, gsum_ref, gexp_ref) * kgain_ref[...]
    q = _dot(bh, wq_ref[...])
    qn = q * _head_rms(q, gsum_ref, gexp_ref) * qgain_ref[...]
    ones = jnp.where((lane >= BIAS_LANE) & (lane < BIAS_LANE + 3), 1.0, 0.0)
    for j in range(N_HEADS_B // 2):
        kj = kn[:, j * LANES:(j + 1) * LANES]
        qj = qn[:, j * LANES:(j + 1) * LANES]
        for half, (kk, qq) in enumerate(((kj, qj), (pltpu.roll(kj, DH_B, 1), pltpu.roll(qj, DH_B, 1)))):
            hd = 2 * j + half
            ex = extras[:, hd * LANES:(hd + 1) * LANES]
            kaug_ref[0, hd] = jnp.where(lane < DH_B, kk, ex).astype(BF16)
            qaug_ref[0, hd] = jnp.where(lane < DH_B, qq, ones).astype(BF16)

    vt = _dot_nt(wvt_ref[...], ah)
    sgt = _sigmoid(_dot_nt(wogt_ref[...], bh))
    for hd in range(N_HEADS_B):
        vt_ref[0, hd] = vt[hd * DH_B:(hd + 1) * DH_B, :].astype(BF16)
    sgt_ref[0] = sgt.astype(BF16)


def _proj_kvq(h1, ys_sorted, slots, gkv, gq, wk, wvt, wf, bf, wq, wogt, kgain, qgain, gsum, gexp, sel,
              bsz, seq, tm):
    ns = seq // tm
    row = lambda b, s: (b * ns + s, 0)
    const2 = lambda b, s: (0, 0)
    const3 = lambda b, s: (0, 0, 0)
    slots3 = slots.reshape(bsz * ns, 1, tm)
    return pl.pallas_call(
        functools.partial(_proj_kvq_kernel, tm=tm),
        grid=(bsz, ns),
        in_specs=[pl.BlockSpec((tm, D_MODEL), row)] + _gather_specs(tm, bsz * ns, lambda b, s: b * ns + s) + [
            pl.BlockSpec((1, D_MODEL), const2),
            pl.BlockSpec((1, D_MODEL), const2),
            pl.BlockSpec((D_MODEL, D_MODEL), const2),
            pl.BlockSpec((D_MODEL, D_MODEL), const2),
            pl.BlockSpec((2, D_MODEL, LANES), const3),
            pl.BlockSpec((1, LANES), const2),
            pl.BlockSpec((D_MODEL, D_MODEL), const2),
            pl.BlockSpec((D_MODEL, D_MODEL), const2),
            pl.BlockSpec((1, D_MODEL), const2),
            pl.BlockSpec((1, D_MODEL), const2),
            pl.BlockSpec((D_MODEL, LANES), const2),
            pl.BlockSpec((LANES, D_MODEL), const2),
            pl.BlockSpec((LANES, N_HEADS_B * LANES), const2),
        ],
        out_specs=[
            pl.BlockSpec((tm, D_MODEL), row),
            pl.BlockSpec((1, N_HEADS_B, tm, LANES), lambda b, s: (b, 0, s, 0)),
            pl.BlockSpec((1, N_HEADS_B, DH_B, tm), lambda b, s: (b, 0, 0, s)),
            pl.BlockSpec((1, N_HEADS_B, tm, LANES), lambda b, s: (b, 0, s, 0)),
            pl.BlockSpec((1, D_MODEL, tm), lambda b, s: (b, 0, s)),
            pl.BlockSpec((1, tm // ATT_BLK, 2, LANES), lambda b, s: (b, s, 0, 0)),
        ],
        out_shape=[
            jax.ShapeDtypeStruct((bsz * seq, D_MODEL), F32),
            jax.ShapeDtypeStruct((bsz, N_HEADS_B, seq, LANES), BF16),
            jax.ShapeDtypeStruct((bsz, N_HEADS_B, DH_B, seq), BF16),
            jax.ShapeDtypeStruct((bsz, N_HEADS_B, seq, LANES), BF16),
            jax.ShapeDtypeStruct((bsz, D_MODEL, seq), BF16),
            jax.ShapeDtypeStruct((bsz, seq // ATT_BLK, 2, LANES), F32),
        ],
        scratch_shapes=[pltpu.VMEM((1, LANES), F32)] + _gather_scratch(tm),
        compiler_params=_params(("arbitrary", "arbitrary")),
        name="proj_kvq",
    )(h1, slots3, slots3, ys_sorted, gkv, gq, wk, wvt, wf, bf, wq, wogt, kgain, qgain, gsum, gexp, sel)


def _attn_kernel(trips_ref, q_ref, k_ref, vt_ref, o_ref, m_scr, acc_scr, sa_scr, sb_scr, xa_scr, xb_scr,
                 p_scr, *, chains):
    b, h, i = pl.program_id(0), pl.program_id(1), pl.program_id(2)
    trips = trips_ref[(b * pl.num_programs(1) + h) * pl.num_programs(2) + i]
    blk = ATT_BLK
    lane = lax.broadcasted_iota(jnp.int32, (blk, LANES), 1)
    k_null = jnp.where(lane == BIAS_LANE, NEG_BIG, 0.0).astype(BF16)

    def key_offset(g, t):
        kb = i * chains + g - t
        return kb, pl.multiple_of(jnp.maximum(kb, 0) * blk, blk)

    all_chains = range(chains)
    ones_rows = jnp.ones((DEN_ROWS, blk), BF16)

    def scores(t, s_ref, x_ref, first, gs=all_chains):
        for g in gs:
            q = q_ref[0, 0, g * blk:(g + 1) * blk, :]
            kb, off = key_offset(g, t)
            kblk = k_ref[0, 0, pl.ds(off, blk), :]
            if not first:
                kblk = jnp.where(kb >= 0, kblk, k_null)
            s = _dot_nt(kblk, q)
            if first:
                ki = lax.broadcasted_iota(jnp.int32, (blk, blk), 0)
                qj = lax.broadcasted_iota(jnp.int32, (blk, blk), 1)
                s = jnp.where(ki <= qj, s, NEG_BIG)
            s_ref[g] = s
            x_ref[g] = jnp.max(s, axis=0, keepdims=True)

    def update(t, s_ref, x_ref, gs=all_chains):
        for g in gs:
            _, off = key_offset(g, t)
            v_den = jnp.concatenate([vt_ref[0, 0, :, pl.ds(off, blk)], ones_rows], axis=0)
            m_old = m_scr[g]
            m_new = jnp.maximum(m_old, x_ref[g])
            p_scr[g] = jnp.exp2(s_ref[g] - m_new).astype(BF16)
            acc_scr[g] = jnp.exp2(m_old - m_new) * acc_scr[g] + _dot(v_den, p_scr[g])
            m_scr[g] = m_new

    m_scr[...] = jnp.full(m_scr.shape, NEG_BIG, F32)
    acc_scr[...] = jnp.zeros(acc_scr.shape, F32)
    scores(0, sa_scr, xa_scr, True)
    steps = trips - 1

    def body(pair, carry):
        t = 2 * pair
        for g in all_chains:
            scores(t + 1, sb_scr, xb_scr, False, [g])
            update(t, sa_scr, xa_scr, [g])
        for g in all_chains:
            scores(t + 2, sa_scr, xa_scr, False, [g])
            update(t + 1, sb_scr, xb_scr, [g])
        return carry

    lax.fori_loop(0, steps >> 1, body, 0)

    @pl.when((steps & 1) == 1)
    def _():
        for g in all_chains:
            scores(steps, sb_scr, xb_scr, False, [g])
            update(steps - 1, sa_scr, xa_scr, [g])
        update(steps, sb_scr, xb_scr)

    @pl.when((steps & 1) == 0)
    def _():
        update(steps, sa_scr, xa_scr)

    for g in all_chains:
        o_ref[0, 0, :, g * blk:(g + 1) * blk] = (acc_scr[g, :DH_B, :] / acc_scr[g, DH_B:DH_B + 1, :]).astype(BF16)


def _attn_trips(cstat, kgain, qgain, chains):
    nbmax = jnp.swapaxes(cstat[:, :, 0, :N_HEADS_B], 1, 2)
    nbmin = jnp.swapaxes(cstat[:, :, 1, :N_HEADS_B], 1, 2)
    nblk = nbmax.shape[-1]
    xb = DH_B * jnp.max(jnp.abs(kgain)) * jnp.max(jnp.abs(qgain))
    slack = 2.0 * xb * SKIP_REL_MARGIN + SKIP_ABS_MARGIN + SKIP_LOG2
    j = jnp.arange(nblk)
    keep = (nbmax[:, :, None, :] - nbmin[:, :, :, None] + slack >= 0.0) & (j[None, :] <= j[:, None])
    jmin = jnp.min(jnp.where(keep, j[None, :], nblk), axis=-1)
    need = j - jnp.minimum(jmin, j) + 1
    return jnp.max(need.reshape(need.shape[0], need.shape[1], nblk // chains, chains), axis=-1).astype(jnp.int32)


def _attn(qaug, kaug, vt, trips, chains):
    bsz, nh, seq, _ = qaug.shape
    tq = chains * ATT_BLK
    grid_spec = pltpu.PrefetchScalarGridSpec(
        num_scalar_prefetch=1,
        grid=(bsz, nh, seq // tq),
        in_specs=[
            pl.BlockSpec((1, 1, tq, LANES), lambda b, h, i, tr: (b, h, i, 0)),
            pl.BlockSpec((1, 1, seq, LANES), lambda b, h, i, tr: (b, h, 0, 0)),
            pl.BlockSpec((1, 1, DH_B, seq), lambda b, h, i, tr: (b, h, 0, 0)),
        ],
        out_specs=pl.BlockSpec((1, 1, DH_B, tq), lambda b, h, i, tr: (b, h, 0, i)),
        scratch_shapes=[
            pltpu.VMEM((chains, 1, ATT_BLK), F32),
            pltpu.VMEM((chains, DH_B + DEN_ROWS, ATT_BLK), F32),
            pltpu.VMEM((chains, ATT_BLK, ATT_BLK), F32),
            pltpu.VMEM((chains, ATT_BLK, ATT_BLK), F32),
            pltpu.VMEM((chains, 1, ATT_BLK), F32),
            pltpu.VMEM((chains, 1, ATT_BLK), F32),
            pltpu.VMEM((chains, ATT_BLK, ATT_BLK), BF16),
        ],
    )
    return pl.pallas_call(
        functools.partial(_attn_kernel, chains=chains),
        grid_spec=grid_spec,
        out_shape=jax.ShapeDtypeStruct((bsz, nh, DH_B, seq), BF16),
        compiler_params=_params(("parallel", "parallel", "arbitrary")),
        name="attn",
    )(trips.reshape(-1), qaug, kaug, vt)


def _out_b_kernel(ot_ref, sgt_ref, h_ref, wout_ref, gn_ref, wr_ref, br_ref,
                  h3_ref, xs_ref, route_ref, counts_ref, *, tm):
    h_parts = []
    for p in range(tm // ROUTE_PIECE):
        rows = slice(p * ROUTE_PIECE, (p + 1) * ROUTE_PIECE)
        zt = (ot_ref[0, :, rows].astype(F32) * sgt_ref[0, :, rows].astype(F32)).astype(BF16)
        h3 = h_ref[rows, :] + _dot_tn(zt, wout_ref[...])
        h3_ref[rows, :] = h3
        h_parts.append(h3)
    first_step = (pl.program_id(0) == 0) & (pl.program_id(1) == 0)
    _ffn_norm_and_route(h_parts, gn_ref, wr_ref, br_ref, xs_ref, route_ref, counts_ref, first_step)


def _out_b(ot, sgt, h2, wout, gn, wr, br, bsz, seq, tm):
    ns = seq // tm
    row = lambda b, s: (b * ns + s, 0)
    const2 = lambda b, s: (0, 0)
    const3 = lambda b, s: (0, 0, 0)
    t = bsz * seq
    return pl.pallas_call(
        functools.partial(_out_b_kernel, tm=tm),
        grid=(bsz, ns),
        in_specs=[
            pl.BlockSpec((1, D_MODEL, tm), lambda b, s: (b, 0, s)),
            pl.BlockSpec((1, D_MODEL, tm), lambda b, s: (b, 0, s)),
            pl.BlockSpec((tm, D_MODEL), row),
            pl.BlockSpec((D_MODEL, D_MODEL), const2),
            pl.BlockSpec((1, D_MODEL), const2),
            pl.BlockSpec((2, D_MODEL, LANES), const3),
            pl.BlockSpec((1, LANES), const2),
        ],
        out_specs=_route_out_specs(tm, row, const2),
        out_shape=_route_out_shapes(t),
        compiler_params=_params(("arbitrary", "arbitrary")),
        name="out_b",
    )(ot, sgt, h2, wout, gn, wr, br)


def _hi_lo(w):
    hi = w.astype(BF16)
    lo = (w - hi.astype(F32)).astype(BF16)
    return jnp.stack([hi, lo])


def _pad_lanes(w, width=LANES):
    return jnp.pad(w, ((0, 0),) * (w.ndim - 1) + ((0, width - w.shape[-1]),))


def _router_params(w_grp, b_grp, w_exp, b_exp):
    w = _pad_lanes(jnp.concatenate([w_grp, w_exp], axis=-1))
    b = _pad_lanes(jnp.concatenate([b_grp, b_exp], axis=-1)[None, :])
    return _hi_lo(w), b


def _tile(seq, pref):
    t = pref
    while seq % t:
        t //= 2
    return t


def kernel(x, norm_mix, norm_ffn, a_w_in, a_b_gate, a_head_gain, a_w_out, kv_norm, kv_w, kv_b_f, kv_k_gain,
           b_w_in, b_q_gain, b_w_out, moe_w_grp, moe_b_grp, moe_w_exp, moe_b_exp, moe_w_gate, moe_w_up,
           moe_w_down, norm_final):
    bsz, seq, _ = x.shape
    t = bsz * seq
    assert seq % CHUNK == 0 and seq % ATT_BLK == 0
    tm = _tile(seq, 512)
    assert tm % ATT_BLK == 0
    x2d = x.reshape(t, D_MODEL)

    w_in = a_w_in[0]
    wgate = w_in[:, 2 * QK_A + 2 * D_MODEL:]
    wgc = _hi_lo(_pad_lanes(wgate))
    wgr = _hi_lo(wgate.T)
    bgc = _pad_lanes(a_b_gate[0][None, :])
    bgr = a_b_gate[0][:, None]
    q, k, v, o, gc, gr = _proj_a(x2d, norm_mix[0][None, :], w_in.astype(BF16), wgc, wgr, bgc, bgr, tm)
    hn = _mlstm(q, k, v, gc, gr, bsz, seq)
    wr0, br0 = _router_params(moe_w_grp[0], moe_b_grp[0], moe_w_exp[0], moe_b_exp[0])
    h1, xs1, route1, counts1 = _out_a(hn, o, x2d, a_head_gain[0][None, :], a_w_out[0].astype(BF16),
                                      norm_ffn[0][None, :], wr0, br0, tm)
    ys1, slots1 = _moe(xs1, route1, counts1, wr0[0], br0, moe_w_gate[0].astype(BF16), moe_w_up[0].astype(BF16),
              moe_w_down[0].astype(BF16))

    wkk = kv_w[:, :D_MODEL].astype(BF16)
    wvt = kv_w[:, D_MODEL:2 * D_MODEL].T.astype(BF16)
    wf = _hi_lo(_pad_lanes(kv_w[:, 2 * D_MODEL:]))
    bf = _pad_lanes(kv_b_f[None, :])
    wq1 = b_w_in[0][:, :D_MODEL].astype(BF16)
    wogt = b_w_in[0][:, D_MODEL:].T.astype(BF16)
    kgain = jnp.tile(kv_k_gain, N_HEADS_B)[None, :]
    qgain = jnp.tile(b_q_gain[0], N_HEADS_B)[None, :] * (DH_B ** -0.5 * LOG2E)
    head_of = jnp.arange(D_MODEL) // DH_B
    gsum = (head_of[:, None] == jnp.arange(LANES)[None, :]).astype(BF16) * (1.0 / DH_B)
    gexp = (jnp.arange(LANES)[:, None] == head_of[None, :]).astype(BF16)
    src = jnp.arange(LANES)[:, None]
    dst = jnp.arange(N_HEADS_B * LANES)[None, :]
    sel = ((src < 3 * N_HEADS_B) & (dst == (src % N_HEADS_B) * LANES + BIAS_LANE + src // N_HEADS_B)).astype(BF16)
    h2, kaug, vt, qaug, sgt, cstat = _proj_kvq(h1, ys1, slots1, kv_norm[None, :], norm_mix[1][None, :], wkk, wvt, wf, bf,
                                               wq1, wogt, kgain, qgain, gsum, gexp, sel, bsz, seq, tm)
    chains = math.gcd(ATT_CHAINS, seq // ATT_BLK)
    ot = _attn(qaug, kaug, vt, _attn_trips(cstat, kgain, qgain, chains), chains)
    wr1, br1 = _router_params(moe_w_grp[1], moe_b_grp[1], moe_w_exp[1], moe_b_exp[1])
    h3, xs3, route3, counts3 = _out_b(ot.reshape(bsz, D_MODEL, seq), sgt, h2, b_w_out[0].astype(BF16),
                                      norm_ffn[1][None, :], wr1, br1, bsz, seq, tm)
    ys3, slots3 = _moe(xs3, route3, counts3, wr1[0], br1, moe_w_gate[1].astype(BF16), moe_w_up[1].astype(BF16),
              moe_w_down[1].astype(BF16))
    out = _final(h3, ys3, slots3, norm_final[None, :], tm)
    return out.reshape(bsz, seq, D_MODEL)
```

```python
import functools
import math

import jax
import jax.numpy as jnp
from jax import lax
from jax.experimental import pallas as pl
from jax.experimental.pallas import tpu as pltpu

F32 = jnp.float32
BF16 = jnp.bfloat16

D_MODEL = 1024
EPS = 1e-6

N_HEADS_A = 4
DV_A = D_MODEL // N_HEADS_A
DQK_A = DV_A // 2
QK_A = N_HEADS_A * DQK_A
GATE_CAP = 15.0
CHUNK = 128
CHUNK_SHIFT = CHUNK.bit_length() - 1

DH_B = 64
N_HEADS_B = D_MODEL // DH_B
LOG2E = 1.4426950408889634
BIAS_LANE = DH_B
NEG_BIG = -1e30
ATT_BLK = 256
ATT_CHAINS = 8
SKIP_LOG2 = 160.0
SKIP_REL_MARGIN = 1.02
SKIP_ABS_MARGIN = 2.0
DEN_ROWS = 16

N_GROUPS = 4
EXP_PER_GROUP = 4
N_EXPERTS = N_GROUPS * EXP_PER_GROUP
D_EXPERT = D_MODEL // 2
ROUTE_LANE0 = N_GROUPS
PAIR_LO = (0, 0, 0, 1, 1, 2)
PAIR_HI = (1, 2, 3, 2, 3, 3)
N_BINS = N_GROUPS * len(PAIR_LO)
MOE_TILE = 256
ROW_SUB = 8
ROWS_PER_STEP = 1024
ROUTE_PIECE = 256
RANK_BASE = 256.0
DMA_PRIORITIES = 2

LANES = 128
VMEM_LIMIT = 56 * 1024 * 1024


def _params(sem, vmem=VMEM_LIMIT):
    return pltpu.CompilerParams(dimension_semantics=sem, vmem_limit_bytes=vmem)


def _dot(a, b):
    return jnp.dot(a, b, preferred_element_type=F32)


def _dot_nt(a, b):
    return lax.dot_general(a, b, (((1,), (1,)), ((), ())), preferred_element_type=F32)


def _dot_tn(a, b):
    return lax.dot_general(a, b, (((0,), (0,)), ((), ())), preferred_element_type=F32)


def _split2(x):
    hi = x.astype(BF16)
    lo = (x - hi.astype(F32)).astype(BF16)
    return hi, lo


def _split3(x):
    hi = x.astype(BF16)
    r = x - hi.astype(F32)
    mid = r.astype(BF16)
    lo = (r - mid.astype(F32)).astype(BF16)
    return hi, mid, lo


def _dot_x3(x_hi, x_lo, w_ref):
    return _dot(x_hi, w_ref[0]) + _dot(x_lo, w_ref[0]) + _dot(x_hi, w_ref[1])


def _dot_nt_x3(w_ref, x_hi, x_lo):
    return _dot_nt(w_ref[0], x_hi) + _dot_nt(w_ref[0], x_lo) + _dot_nt(w_ref[1], x_hi)


def _dot_exact_rhs(a, b_exact):
    a0, a1, a2 = _split3(a)
    return _dot(a0, b_exact) + _dot(a1, b_exact) + _dot(a2, b_exact)


def _dot_exact_lhs(a_exact, b):
    b0, b1, b2 = _split3(b)
    return _dot(a_exact, b0) + _dot(a_exact, b1) + _dot(a_exact, b2)


def _rms_scale(x):
    return lax.rsqrt(jnp.mean(x * x, axis=-1, keepdims=True) + EPS)


def _log_sigmoid(z):
    return jnp.minimum(z, 0.0) - jnp.log(1.0 + jnp.exp(-jnp.abs(z)))


def _sigmoid(z):
    return 1.0 / (1.0 + jnp.exp(-z))


def _softcap(z):
    return GATE_CAP * jnp.tanh(z / GATE_CAP)


def _proj_a_kernel(x_ref, g_ref, wq_ref, wk_ref, wv_ref, wo_ref, wgc_ref, wgr_ref, bgc_ref, bgr_ref,
                   q_ref, k_ref, v_ref, o_ref, gc_ref, gr_ref, *, tm):
    x = x_ref[...]
    xn = x * _rms_scale(x) * g_ref[...]
    xh, xl = _split2(xn)
    q_ref[...] = _dot(xh, wq_ref[...]).astype(BF16)
    k_ref[...] = (_dot(xh, wk_ref[...]) * (DQK_A ** -0.5)).astype(BF16)
    v_ref[...] = _dot(xh, wv_ref[...]).astype(BF16)
    o_ref[...] = _dot(xh, wo_ref[...]).astype(BF16)

    zc = _softcap(_dot_x3(xh, xl, wgc_ref) + bgc_ref[...])
    zr = _softcap(_dot_nt_x3(wgr_ref, xh, xl) + bgr_ref[...])
    lane = lax.broadcasted_iota(jnp.int32, zc.shape, 1)
    sub = lax.broadcasted_iota(jnp.int32, zr.shape, 0)
    vc = jnp.where(lane < N_HEADS_A, zc, _log_sigmoid(zc))
    vr = jnp.where(sub < N_HEADS_A, zr, _log_sigmoid(zr))
    ti = lax.broadcasted_iota(jnp.int32, (tm, tm), 0)
    tj = lax.broadcasted_iota(jnp.int32, (tm, tm), 1)
    same = (ti >> CHUNK_SHIFT) == (tj >> CHUNK_SHIFT)
    tril = jnp.where(same & (tj <= ti), 1.0, 0.0).astype(BF16)
    triu = jnp.where(same & (ti <= tj), 1.0, 0.0).astype(BF16)
    cc = _dot_exact_lhs(tril, vc)
    cr = _dot_exact_rhs(vr, triu)
    gc = jnp.where(lane < N_HEADS_A, vc, cc)
    gr = jnp.where(sub < N_HEADS_A, vr, cr)
    a_rows = gr[:N_HEADS_A, :] - gr[N_HEADS_A:, :]
    ci = lax.broadcasted_iota(jnp.int32, (CHUNK, CHUNK), 0)
    cj = lax.broadcasted_iota(jnp.int32, (CHUNK, CHUNK), 1)
    for h in range(N_HEADS_A):
        col = jnp.concatenate(
            [jnp.max(jnp.where(cj <= ci, a_rows[h:h + 1, c * CHUNK:(c + 1) * CHUNK], -jnp.inf), axis=-1, keepdims=True)
             for c in range(tm // CHUNK)], axis=0)
        gc = jnp.where(lane == 2 * N_HEADS_A + h, col, gc)
    gc_ref[...] = gc
    gr_ref[...] = gr


def _proj_a(x2d, g, w_in, wgc, wgr, bgc, bgr, tm):
    assert 2 * QK_A == D_MODEL
    t = x2d.shape[0]
    row = lambda i: (i, 0)
    const2 = lambda i: (0, 0)
    const3 = lambda i: (0, 0, 0)
    return pl.pallas_call(
        functools.partial(_proj_a_kernel, tm=tm),
        grid=(t // tm,),
        in_specs=[
            pl.BlockSpec((tm, D_MODEL), row),
            pl.BlockSpec((1, D_MODEL), const2),
            pl.BlockSpec((D_MODEL, QK_A), lambda i: (0, 0)),
            pl.BlockSpec((D_MODEL, QK_A), lambda i: (0, 1)),
            pl.BlockSpec((D_MODEL, D_MODEL), lambda i: (0, 1)),
            pl.BlockSpec((D_MODEL, D_MODEL), lambda i: (0, 2)),
            pl.BlockSpec((2, D_MODEL, LANES), const3),
            pl.BlockSpec((2, 8, D_MODEL), const3),
            pl.BlockSpec((1, LANES), const2),
            pl.BlockSpec((8, 1), const2),
        ],
        out_specs=[
            pl.BlockSpec((tm, QK_A), row),
            pl.BlockSpec((tm, QK_A), row),
            pl.BlockSpec((tm, D_MODEL), row),
            pl.BlockSpec((tm, D_MODEL), row),
            pl.BlockSpec((tm, LANES), row),
            pl.BlockSpec((8, tm), lambda i: (0, i)),
        ],
        out_shape=[
            jax.ShapeDtypeStruct((t, QK_A), BF16),
            jax.ShapeDtypeStruct((t, QK_A), BF16),
            jax.ShapeDtypeStruct((t, D_MODEL), BF16),
            jax.ShapeDtypeStruct((t, D_MODEL), BF16),
            jax.ShapeDtypeStruct((t, LANES), F32),
            jax.ShapeDtypeStruct((8, t), F32),
        ],
        compiler_params=_params(("parallel",)),
        name="proj_a",
    )(x2d, g, w_in, w_in, w_in, w_in, wgc, wgr, bgc, bgr)


def _mlstm_kernel(q_ref, k_ref, v_ref, gc_ref, gr_ref, h_ref, c_scr, m_scr):
    @pl.when(pl.program_id(1) == 0)
    def _():
        c_scr[...] = jnp.zeros_like(c_scr)
        m_scr[...] = jnp.zeros_like(m_scr)

    ti = lax.broadcasted_iota(jnp.int32, (CHUNK, CHUNK), 0)
    si = lax.broadcasted_iota(jnp.int32, (CHUNK, CHUNK), 1)
    causal = si <= ti
    gc = gc_ref[...]
    gr = gr_ref[...]
    heads = range(N_HEADS_A)
    ones_blk = jnp.ones((CHUNK, LANES), BF16)
    ones_sq = jnp.ones((DV_A, LANES), BF16)
    qs = [q_ref[:, h * DQK_A:(h + 1) * DQK_A] for h in heads]
    ks = [k_ref[:, h * DQK_A:(h + 1) * DQK_A] for h in heads]
    vs = [jnp.concatenate([v_ref[:, h * DV_A:(h + 1) * DV_A], ones_blk], axis=1) for h in heads]

    gates = []
    for h in heads:
        b_c = gc[:, N_HEADS_A + h:N_HEADS_A + h + 1]
        a_c = gc[:, h:h + 1] - b_c
        amax_c = gc[:, 2 * N_HEADS_A + h:2 * N_HEADS_A + h + 1]
        a_r = gr[h:h + 1, :] - gr[N_HEADS_A + h:N_HEADS_A + h + 1, :]
        b_end = b_c[CHUNK - 1:CHUNK, :]
        m_run = m_scr[h][:, 0:1]
        mx_c = jnp.maximum(m_run, amax_c)
        mx_end = mx_c[CHUNK - 1:CHUNK, :]
        d_mat = jnp.where(causal, jnp.exp(a_r - mx_c), 0.0)
        w_inter = jnp.broadcast_to(jnp.exp(m_run - mx_c), (CHUNK, LANES))
        floor = jnp.broadcast_to(jnp.exp(-(b_c + mx_c)), (CHUNK, LANES))
        w_c = jnp.exp(a_c - mx_end)
        decay = jnp.exp(m_run - mx_end)
        gates.append((d_mat, w_inter, floor, b_end + mx_end, w_c, decay))

    kws = [(ks[h].astype(F32) * gates[h][4]).astype(BF16) for h in heads]
    s_raw = [_dot_nt(qs[h], ks[h]) for h in heads]
    inter = [_dot(qs[h], c_scr[h].astype(BF16)) for h in heads]
    c_upd = [_dot_tn(kws[h], vs[h]) for h in heads]
    for h in heads:
        d_mat, w_inter, floor, m_new, w_c, decay = gates[h]
        tot = _dot((s_raw[h] * d_mat).astype(BF16), vs[h])
        den = tot[:, DV_A:] + w_inter * inter[h][:, DV_A:]
        rden = 1.0 / jnp.maximum(jnp.abs(den), floor)
        halves = [(tot[:, j * LANES:(j + 1) * LANES] + w_inter * inter[h][:, j * LANES:(j + 1) * LANES]) * rden
                  for j in range(DV_A // LANES)]
        hh = jnp.concatenate(halves, axis=1)
        ms = _dot((hh * hh).astype(BF16), ones_sq) * (1.0 / DV_A)
        rs = lax.rsqrt(ms + EPS)
        for j in range(DV_A // LANES):
            h_ref[:, h * DV_A + j * LANES:h * DV_A + (j + 1) * LANES] = (halves[j] * rs).astype(BF16)
        c_scr[h] = decay * c_scr[h] + c_upd[h]
        m_scr[h] = jnp.broadcast_to(m_new, (1, LANES))


def _mlstm(q, k, v, gc, gr, bsz, seq):
    nc = seq // CHUNK
    row = lambda b, c: (b * nc + c, 0)
    return pl.pallas_call(
        _mlstm_kernel,
        grid=(bsz, nc),
        in_specs=[
            pl.BlockSpec((CHUNK, QK_A), row),
            pl.BlockSpec((CHUNK, QK_A), row),
            pl.BlockSpec((CHUNK, D_MODEL), row),
            pl.BlockSpec((CHUNK, LANES), row),
            pl.BlockSpec((8, CHUNK), lambda b, c: (0, b * nc + c)),
        ],
        out_specs=pl.BlockSpec((CHUNK, D_MODEL), row),
        out_shape=jax.ShapeDtypeStruct((bsz * seq, D_MODEL), BF16),
        scratch_shapes=[
            pltpu.VMEM((N_HEADS_A, DQK_A, DV_A + LANES), F32),
            pltpu.VMEM((N_HEADS_A, 1, LANES), F32),
        ],
        compiler_params=_params(("arbitrary", "arbitrary")),
        name="mlstm",
    )(q, k, v, gc, gr)


def _route_bins(logits):
    lane = lax.broadcasted_iota(jnp.int32, logits.shape, 1)
    big = jnp.int32(10 ** 6)

    def top(mask):
        mx = jnp.max(jnp.where(mask, logits, -jnp.inf), axis=-1, keepdims=True)
        return jnp.min(jnp.where(mask & (logits == mx), lane, big), axis=-1, keepdims=True)

    gidx = top(lane < N_GROUPS)
    lo = ROUTE_LANE0 + EXP_PER_GROUP * gidx
    emask = (lane >= lo) & (lane < lo + EXP_PER_GROUP)
    i1 = top(emask)
    i2 = top(emask & (lane != i1))
    p_lo = jnp.minimum(i1, i2) - lo
    p_hi = jnp.maximum(i1, i2) - lo
    pair = jnp.where(p_lo == 0, p_hi - 1, jnp.where(p_lo == 1, p_hi + 1, len(PAIR_LO) - 1))
    return gidx * len(PAIR_LO) + pair


def _pair_weights(logits, grp, e_lo, e_hi):
    lane = lax.broadcasted_iota(jnp.int32, logits.shape, 1)

    def pick(idx):
        return jnp.sum(jnp.where(lane == idx, logits, 0.0), axis=-1, keepdims=True)

    def softmax_stats(mask):
        mx = jnp.max(jnp.where(mask, logits, -jnp.inf), axis=-1, keepdims=True)
        return mx, jnp.sum(jnp.where(mask, jnp.exp(logits - mx), 0.0), axis=-1, keepdims=True)

    gmax, gsum = softmax_stats(lane < N_GROUPS)
    g_p = jnp.exp(pick(grp) - gmax) / gsum
    lo = ROUTE_LANE0 + EXP_PER_GROUP * grp
    emax, esum = softmax_stats((lane >= lo) & (lane < lo + EXP_PER_GROUP))
    p_lo = jnp.exp(pick(ROUTE_LANE0 + e_lo) - emax) / esum
    p_hi = jnp.exp(pick(ROUTE_LANE0 + e_hi) - emax) / esum
    return g_p * (p_lo / (p_lo + p_hi)), g_p * (p_hi / (p_lo + p_hi))


def _store_rows(rows_ref, x, tm, tok0=0):
    for c in range(ROW_SUB):
        rows_ref[pl.ds(tok0 * ROW_SUB + c, tm, stride=ROW_SUB), :] = x[:, c * LANES:(c + 1) * LANES]


def _load_rows(rows_ref, tm):
    return jnp.concatenate([rows_ref[pl.ds(c, tm, stride=ROW_SUB), :] for c in range(ROW_SUB)], axis=-1)


def _ffn_norm_and_route(h_parts, gn_ref, wr_ref, br_ref, xs_ref, route_ref, counts_ref, first_step):
    @pl.when(first_step)
    def _():
        counts_ref[...] = jnp.zeros_like(counts_ref)

    tp = h_parts[0].shape[0]
    pieces = range(len(h_parts))
    xns = [h * _rms_scale(h) * gn_ref[...] for h in h_parts]
    for p in pieces:
        _store_rows(xs_ref, xns[p], tp, p * tp)
    splits = [_split2(xn) for xn in xns]
    logits = [_dot_x3(xh, xl, wr_ref) + br_ref[...] for xh, xl in splits]
    bins = [_route_bins(lg) for lg in logits]
    lane = lax.broadcasted_iota(jnp.int32, (tp, LANES), 1)
    onehots = [jnp.where(lane == b, 1.0, 0.0) for b in bins]
    ti = lax.broadcasted_iota(jnp.int32, (tp, tp), 0)
    tj = lax.broadcasted_iota(jnp.int32, (tp, tp), 1)
    earlier = jnp.where(tj < ti, 1.0, 0.0).astype(BF16)
    befores = [_dot(earlier, oh.astype(BF16)) for oh in onehots]
    pick = jnp.where(lax.broadcasted_iota(jnp.int32, (8, LANES), 0) == lax.broadcasted_iota(jnp.int32, (8, LANES), 1),
                     1.0, 0.0).astype(BF16)
    counts = counts_ref[...]
    for p in pieces:
        rank = jnp.sum(onehots[p] * (befores[p] + counts), axis=-1, keepdims=True)
        counts = counts + jnp.sum(onehots[p], axis=0, keepdims=True)
        rank_hi = jnp.floor(rank * (1.0 / RANK_BASE))
        cols = jnp.where(lane == 0, bins[p].astype(F32),
                         jnp.where(lane == 1, rank_hi, jnp.where(lane == 2, rank - RANK_BASE * rank_hi, 0.0)))
        route_ref[:, p * tp:(p + 1) * tp] = _dot_nt(pick, cols.astype(BF16))
    counts_ref[...] = counts


def _out_a_kernel(hn_ref, o_ref, x_ref, hg_ref, wout_ref, gn_ref, wr_ref, br_ref,
                  h1_ref, xs_ref, route_ref, counts_ref, *, tm):
    h_parts = []
    for p in range(tm // ROUTE_PIECE):
        rows = slice(p * ROUTE_PIECE, (p + 1) * ROUTE_PIECE)
        z = (hn_ref[rows, :].astype(F32) * hg_ref[...] * _sigmoid(o_ref[rows, :].astype(F32))).astype(BF16)
        h1 = x_ref[rows, :] + _dot(z, wout_ref[...])
        h1_ref[rows, :] = h1
        h_parts.append(h1)
    _ffn_norm_and_route(h_parts, gn_ref, wr_ref, br_ref, xs_ref, route_ref, counts_ref, pl.program_id(0) == 0)


def _route_out_specs(tm, row, const2):
    return [
        pl.BlockSpec((tm, D_MODEL), row),
        pl.BlockSpec((tm * ROW_SUB, LANES), row),
        pl.BlockSpec((8, tm), lambda *idx: (0, row(*idx)[0])),
        pl.BlockSpec((1, LANES), const2),
    ]


def _route_out_shapes(t):
    return [
        jax.ShapeDtypeStruct((t, D_MODEL), F32),
        jax.ShapeDtypeStruct((t * ROW_SUB, LANES), F32),
        jax.ShapeDtypeStruct((8, t), F32),
        jax.ShapeDtypeStruct((1, LANES), F32),
    ]


def _out_a(hn, o, x2d, hg, wout, gn, wr, br, tm):
    t = x2d.shape[0]
    row = lambda i: (i, 0)
    const2 = lambda i: (0, 0)
    const3 = lambda i: (0, 0, 0)
    return pl.pallas_call(
        functools.partial(_out_a_kernel, tm=tm),
        grid=(t // tm,),
        in_specs=[
            pl.BlockSpec((tm, D_MODEL), row),
            pl.BlockSpec((tm, D_MODEL), row),
            pl.BlockSpec((tm, D_MODEL), row),
            pl.BlockSpec((1, D_MODEL), const2),
            pl.BlockSpec((D_MODEL, D_MODEL), const2),
            pl.BlockSpec((1, D_MODEL), const2),
            pl.BlockSpec((2, D_MODEL, LANES), const3),
            pl.BlockSpec((1, LANES), const2),
        ],
        out_specs=_route_out_specs(tm, row, const2),
        out_shape=_route_out_shapes(t),
        compiler_params=_params(("arbitrary",)),
        name="out_a",
    )(hn, o, x2d, hg, wout, gn, wr, br)


def _dispatch_kernel(slot_ref, src_ref, zeros_ref, dst_ref, sem, *, td):
    del zeros_ref

    def body(pair, carry):
        for prio in range(DMA_PRIORITIES):
            j = pair * DMA_PRIORITIES + prio
            row = pl.multiple_of(slot_ref[0, 0, j] * ROW_SUB, ROW_SUB)
            pltpu.make_async_copy(src_ref.at[pl.ds(pl.multiple_of(j * ROW_SUB, ROW_SUB), ROW_SUB)],
                                  dst_ref.at[pl.ds(row, ROW_SUB)], sem).start(priority=prio)
        return carry

    lax.fori_loop(0, td // DMA_PRIORITIES, body, 0, unroll=4)
    pltpu.make_async_copy(src_ref, dst_ref.at[pl.ds(0, td * ROW_SUB)], sem).wait()


def _dispatch(slots, src, n_dst_tokens):
    t = slots.shape[0]
    td = math.gcd(ROWS_PER_STEP, t)
    any_spec = pl.BlockSpec(memory_space=pl.ANY)
    return pl.pallas_call(
        functools.partial(_dispatch_kernel, td=td),
        grid=(t // td,),
        in_specs=[
            pl.BlockSpec((1, 1, td), lambda i: (i, 0, 0), memory_space=pltpu.SMEM),
            pl.BlockSpec((td * ROW_SUB, LANES), lambda i: (i, 0)),
            any_spec,
        ],
        out_specs=any_spec,
        scratch_shapes=[pltpu.SemaphoreType.DMA(())],
        out_shape=jax.ShapeDtypeStruct((n_dst_tokens * ROW_SUB, LANES), F32),
        input_output_aliases={2: 0},
        compiler_params=_params(("arbitrary",)),
        name="moe_dispatch",
    )(slots.reshape(t // td, 1, td), src, jnp.zeros((n_dst_tokens * ROW_SUB, LANES), F32))


def _moe_kernel(elo_ref, ehi_ref, grp_ref, valid_ref, xs_ref, wr_ref, br_ref,
                wg_lo_ref, wu_lo_ref, wd_lo_ref, wg_hi_ref, wu_hi_ref, wd_hi_ref, ys_ref):
    i = pl.program_id(0)

    @pl.when(valid_ref[i] == 0)
    def _():
        ys_ref[...] = jnp.zeros_like(ys_ref)

    @pl.when(valid_ref[i] != 0)
    def _():
        x = _load_rows(xs_ref, MOE_TILE).astype(BF16)
        w_lo, w_hi = _pair_weights(_dot(x, wr_ref[...]) + br_ref[...], grp_ref[i], elo_ref[i], ehi_ref[i])
        y = None
        for wg_ref, wu_ref, wd_ref, w in ((wg_lo_ref, wu_lo_ref, wd_lo_ref, w_lo),
                                           (wg_hi_ref, wu_hi_ref, wd_hi_ref, w_hi)):
            g = _dot(x, wg_ref[0])
            u = _dot(x, wu_ref[0])
            d = _dot((g * _sigmoid(g) * u * w).astype(BF16), wd_ref[0])
            y = d if y is None else y + d
        _store_rows(ys_ref, y, MOE_TILE)


def _moe_plan(route, counts, n_tiles):
    cnt = counts[0, :N_BINS].astype(jnp.int32)
    padded = (cnt + (MOE_TILE - 1)) // MOE_TILE * MOE_TILE
    ends = jnp.cumsum(padded)
    starts = ends - padded
    rank = (route[1] * RANK_BASE + route[2]).astype(jnp.int32)
    slots = starts[route[0].astype(jnp.int32)] + rank
    last_tile = ends[-1] // MOE_TILE - 1
    tile = jnp.arange(n_tiles, dtype=jnp.int32)
    first_row = jnp.minimum(tile, last_tile) * MOE_TILE
    tbin = jnp.sum((ends[None, :] <= first_row[:, None]).astype(jnp.int32), axis=1)
    grp = tbin // len(PAIR_LO)
    pair = tbin % len(PAIR_LO)
    elo = grp * EXP_PER_GROUP + jnp.asarray(PAIR_LO, jnp.int32)[pair]
    ehi = grp * EXP_PER_GROUP + jnp.asarray(PAIR_HI, jnp.int32)[pair]
    valid = (tile <= last_tile).astype(jnp.int32)
    return slots, elo, ehi, grp, valid


def _moe_experts(xs_sorted, elo, ehi, grp, valid, wr, br, wg, wu, wd):
    n_tiles = xs_sorted.shape[0] // (MOE_TILE * ROW_SUB)
    rows = pl.BlockSpec((MOE_TILE * ROW_SUB, LANES), lambda i, *_: (i, 0))
    lo3 = lambda i, elo, ehi, grp, valid: (elo[i], 0, 0)
    hi3 = lambda i, elo, ehi, grp, valid: (ehi[i], 0, 0)
    const2 = lambda i, *_: (0, 0)
    up_shape, down_shape = (1, D_MODEL, D_EXPERT), (1, D_EXPERT, D_MODEL)
    return pl.pallas_call(
        _moe_kernel,
        grid_spec=pltpu.PrefetchScalarGridSpec(
            num_scalar_prefetch=4,
            grid=(n_tiles,),
            in_specs=[
                rows,
                pl.BlockSpec((D_MODEL, LANES), const2),
                pl.BlockSpec((1, LANES), const2),
                pl.BlockSpec(up_shape, lo3), pl.BlockSpec(up_shape, lo3), pl.BlockSpec(down_shape, lo3),
                pl.BlockSpec(up_shape, hi3), pl.BlockSpec(up_shape, hi3), pl.BlockSpec(down_shape, hi3),
            ],
            out_specs=rows,
        ),
        out_shape=jax.ShapeDtypeStruct(xs_sorted.shape, F32),
        compiler_params=_params(("arbitrary",)),
        name="moe_experts",
    )(elo, ehi, grp, valid, xs_sorted, wr, br, wg, wu, wd, wg, wu, wd)


def _moe(xs, route, counts, wr, br, wg, wu, wd):
    t = route.shape[1]
    n_sorted = t + N_BINS * MOE_TILE
    slots, elo, ehi, grp, valid = _moe_plan(route, counts, n_sorted // MOE_TILE)
    xs_sorted = _dispatch(slots, xs, n_sorted)
    return _moe_experts(xs_sorted, elo, ehi, grp, valid, wr, br, wg, wu, wd), slots


def _gather_start(slot_ref, src_ref, buf_ref, sem, tm):
    def body(pair, carry):
        for prio in range(DMA_PRIORITIES):
            j = pair * DMA_PRIORITIES + prio
            row = pl.multiple_of(slot_ref[0, 0, j] * ROW_SUB, ROW_SUB)
            pltpu.make_async_copy(src_ref.at[pl.ds(row, ROW_SUB)],
                                  buf_ref.at[pl.ds(pl.multiple_of(j * ROW_SUB, ROW_SUB), ROW_SUB)],
                                  sem).start(priority=prio)
        return carry

    lax.fori_loop(0, tm // DMA_PRIORITIES, body, 0, unroll=4)


def _gathered_rows(step, n_steps, slot_cur_ref, slot_next_ref, src_ref, buf_ref, sem, tm):
    cur = step & 1

    @pl.when(step == 0)
    def _():
        _gather_start(slot_cur_ref, src_ref, buf_ref.at[0], sem.at[0], tm)

    @pl.when(step + 1 < n_steps)
    def _():
        _gather_start(slot_next_ref, src_ref, buf_ref.at[1 - cur], sem.at[1 - cur], tm)

    pltpu.make_async_copy(src_ref.at[pl.ds(0, tm * ROW_SUB)], buf_ref.at[cur], sem.at[cur]).wait()
    return _load_rows(buf_ref.at[cur], tm)


def _gather_specs(tm, n_steps, step_of):
    cur = lambda *idx: (step_of(*idx), 0, 0)
    nxt = lambda *idx: (jnp.minimum(step_of(*idx) + 1, n_steps - 1), 0, 0)
    return [pl.BlockSpec((1, 1, tm), cur, memory_space=pltpu.SMEM),
            pl.BlockSpec((1, 1, tm), nxt, memory_space=pltpu.SMEM),
            pl.BlockSpec(memory_space=pl.ANY)]


def _gather_scratch(tm):
    return [pltpu.VMEM((2, tm * ROW_SUB, LANES), F32), pltpu.SemaphoreType.DMA((2,))]


def _final_kernel(h_ref, slot_cur_ref, slot_next_ref, ys_ref, g_ref, out_ref, ybuf, sem, *, tm):
    y = _gathered_rows(pl.program_id(0), pl.num_programs(0), slot_cur_ref, slot_next_ref, ys_ref, ybuf, sem, tm)
    h = h_ref[...] + y
    out_ref[...] = h * _rms_scale(h) * g_ref[...]


def _final(h3, ys_sorted, slots, g, tm):
    t = h3.shape[0]
    n_steps = t // tm
    row = lambda i: (i, 0)
    slots3 = slots.reshape(n_steps, 1, tm)
    return pl.pallas_call(
        functools.partial(_final_kernel, tm=tm),
        grid=(n_steps,),
        in_specs=[pl.BlockSpec((tm, D_MODEL), row)] + _gather_specs(tm, n_steps, lambda i: i)
        + [pl.BlockSpec((1, D_MODEL), lambda i: (0, 0))],
        out_specs=pl.BlockSpec((tm, D_MODEL), row),
        out_shape=jax.ShapeDtypeStruct((t, D_MODEL), F32),
        scratch_shapes=_gather_scratch(tm),
        compiler_params=_params(("arbitrary",)),
        name="final_norm",
    )(h3, slots3, slots3, ys_sorted, g)


def _head_rms(x, gsum_ref, gexp_ref):
    ms = _dot((x * x).astype(BF16), gsum_ref[...])
    r_hi, r_lo = _split2(lax.rsqrt(ms + EPS))
    return _dot(r_hi, gexp_ref[...]) + _dot(r_lo, gexp_ref[...])


def _proj_kvq_kernel(h_ref, slot_cur_ref, slot_next_ref, ys_ref, gkv_ref, gq_ref, wk_ref, wvt_ref, wf_ref, bf_ref,
                     wq_ref, wogt_ref, kgain_ref, qgain_ref, gsum_ref, gexp_ref, sel_ref,
                     h2_ref, kaug_ref, vt_ref, qaug_ref, sgt_ref, cstat_ref, carry_ref, ybuf, sem, *, tm):
    @pl.when(pl.program_id(1) == 0)
    def _():
        carry_ref[...] = jnp.zeros_like(carry_ref)

    step = pl.program_id(0) * pl.num_programs(1) + pl.program_id(1)
    n_steps = pl.num_programs(0) * pl.num_programs(1)
    ymoe = _gathered_rows(step, n_steps, slot_cur_ref, slot_next_ref, ys_ref, ybuf, sem, tm)
    hres = h_ref[...] + ymoe
    h2_ref[...] = hres
    y = hres * _rms_scale(hres)
    a = y * gkv_ref[...]
    ah, al = _split2(a)
    bh = (y * gq_ref[...]).astype(BF16)

    logf = _log_sigmoid(_dot_x3(ah, al, wf_ref) + bf_ref[...])
    ti = lax.broadcasted_iota(jnp.int32, (tm, tm), 0)
    tj = lax.broadcasted_iota(jnp.int32, (tm, tm), 1)
    tril = jnp.where(tj <= ti, 1.0, 0.0).astype(BF16)
    c = _dot_exact_lhs(tril, logf) + carry_ref[...]
    carry_ref[...] = c[tm - 1:tm, :]

    nb = c * (-LOG2E)
    for blk in range(tm // ATT_BLK):
        nb_blk = nb[blk * ATT_BLK:(blk + 1) * ATT_BLK, :]
        cstat_ref[0, blk, 0:1, :] = jnp.max(nb_blk, axis=0, keepdims=True)
        cstat_ref[0, blk, 1:2, :] = jnp.min(nb_blk, axis=0, keepdims=True)
    n0, n1, n2 = _split3(nb)
    lane = lax.broadcasted_iota(jnp.int32, (tm, LANES), 1)
    packed = jnp.where(lane < N_HEADS_B, n0.astype(F32),
                       jnp.where(lane < 2 * N_HEADS_B, pltpu.roll(n1.astype(F32), N_HEADS_B, 1),
                                 pltpu.roll(n2.astype(F32), 2 * N_HEADS_B, 1)))
    packed = jnp.where(lane < 3 * N_HEADS_B, packed, 0.0).astype(BF16)
    extras = _dot(packed, sel_ref[...])

    k = _dot(ah, wk_ref[...])
    kn = k * _head_rms(k, gsum_ref, gexp_ref) * kgain_ref[...]
    q = _dot(bh, wq_ref[...])
    qn = q * _head_rms(q, gsum_ref, gexp_ref) * qgain_ref[...]
    ones = jnp.where((lane >= BIAS_LANE) & (lane < BIAS_LANE + 3), 1.0, 0.0)
    for j in range(N_HEADS_B // 2):
        kj = kn[:, j * LANES:(j + 1) * LANES]
        qj = qn[:, j * LANES:(j + 1) * LANES]
        for half, (kk, qq) in enumerate(((kj, qj), (pltpu.roll(kj, DH_B, 1), pltpu.roll(qj, DH_B, 1)))):
            hd = 2 * j + half
            ex = extras[:, hd * LANES:(hd + 1) * LANES]
            kaug_ref[0, hd] = jnp.where(lane < DH_B, kk, ex).astype(BF16)
            qaug_ref[0, hd] = jnp.where(lane < DH_B, qq, ones).astype(BF16)

    vt = _dot_nt(wvt_ref[...], ah)
    sgt = _sigmoid(_dot_nt(wogt_ref[...], bh))
    for hd in range(N_HEADS_B):
        vt_ref[0, hd] = vt[hd * DH_B:(hd + 1) * DH_B, :].astype(BF16)
    sgt_ref[0] = sgt.astype(BF16)


def _proj_kvq(h1, ys_sorted, slots, gkv, gq, wk, wvt, wf, bf, wq, wogt, kgain, qgain, gsum, gexp, sel,
              bsz, seq, tm):
    ns = seq // tm
    row = lambda b, s: (b * ns + s, 0)
    const2 = lambda b, s: (0, 0)
    const3 = lambda b, s: (0, 0, 0)
    slots3 = slots.reshape(bsz * ns, 1, tm)
    return pl.pallas_call(
        functools.partial(_proj_kvq_kernel, tm=tm),
        grid=(bsz, ns),
        in_specs=[pl.BlockSpec((tm, D_MODEL), row)] + _gather_specs(tm, bsz * ns, lambda b, s: b * ns + s) + [
            pl.BlockSpec((1, D_MODEL), const2),
            pl.BlockSpec((1, D_MODEL), const2),
            pl.BlockSpec((D_MODEL, D_MODEL), const2),
            pl.BlockSpec((D_MODEL, D_MODEL), const2),
            pl.BlockSpec((2, D_MODEL, LANES), const3),
            pl.BlockSpec((1, LANES), const2),
            pl.BlockSpec((D_MODEL, D_MODEL), const2),
            pl.BlockSpec((D_MODEL, D_MODEL), const2),
            pl.BlockSpec((1, D_MODEL), const2),
            pl.BlockSpec((1, D_MODEL), const2),
            pl.BlockSpec((D_MODEL, LANES), const2),
            pl.BlockSpec((LANES, D_MODEL), const2),
            pl.BlockSpec((LANES, N_HEADS_B * LANES), const2),
        ],
        out_specs=[
            pl.BlockSpec((tm, D_MODEL), row),
            pl.BlockSpec((1, N_HEADS_B, tm, LANES), lambda b, s: (b, 0, s, 0)),
            pl.BlockSpec((1, N_HEADS_B, DH_B, tm), lambda b, s: (b, 0, 0, s)),
            pl.BlockSpec((1, N_HEADS_B, tm, LANES), lambda b, s: (b, 0, s, 0)),
            pl.BlockSpec((1, D_MODEL, tm), lambda b, s: (b, 0, s)),
            pl.BlockSpec((1, tm // ATT_BLK, 2, LANES), lambda b, s: (b, s, 0, 0)),
        ],
        out_shape=[
            jax.ShapeDtypeStruct((bsz * seq, D_MODEL), F32),
            jax.ShapeDtypeStruct((bsz, N_HEADS_B, seq, LANES), BF16),
            jax.ShapeDtypeStruct((bsz, N_HEADS_B, DH_B, seq), BF16),
            jax.ShapeDtypeStruct((bsz, N_HEADS_B, seq, LANES), BF16),
            jax.ShapeDtypeStruct((bsz, D_MODEL, seq), BF16),
            jax.ShapeDtypeStruct((bsz, seq // ATT_BLK, 2, LANES), F32),
        ],
        scratch_shapes=[pltpu.VMEM((1, LANES), F32)] + _gather_scratch(tm),
        compiler_params=_params(("arbitrary", "arbitrary")),
        name="proj_kvq",
    )(h1, slots3, slots3, ys_sorted, gkv, gq, wk, wvt, wf, bf, wq, wogt, kgain, qgain, gsum, gexp, sel)


def _attn_kernel(trips_ref, q_ref, k_ref, vt_ref, o_ref, m_scr, acc_scr, sa_scr, sb_scr, xa_scr, xb_scr,
                 p_scr, *, chains):
    b, h, i = pl.program_id(0), pl.program_id(1), pl.program_id(2)
    trips = trips_ref[(b * pl.num_programs(1) + h) * pl.num_programs(2) + i]
    blk = ATT_BLK
    lane = lax.broadcasted_iota(jnp.int32, (blk, LANES), 1)
    k_null = jnp.where(lane == BIAS_LANE, NEG_BIG, 0.0).astype(BF16)

    def key_offset(g, t):
        kb = i * chains + g - t
        return kb, pl.multiple_of(jnp.maximum(kb, 0) * blk, blk)

    all_chains = range(chains)
    ones_rows = jnp.ones((DEN_ROWS, blk), BF16)

    def scores(t, s_ref, x_ref, first, gs=all_chains):
        for g in gs:
            q = q_ref[0, 0, g * blk:(g + 1) * blk, :]
            kb, off = key_offset(g, t)
            kblk = k_ref[0, 0, pl.ds(off, blk), :]
            if not first:
                kblk = jnp.where(kb >= 0, kblk, k_null)
            s = _dot_nt(kblk, q)
            if first:
                ki = lax.broadcasted_iota(jnp.int32, (blk, blk), 0)
                qj = lax.broadcasted_iota(jnp.int32, (blk, blk), 1)
                s = jnp.where(ki <= qj, s, NEG_BIG)
            s_ref[g] = s
            x_ref[g] = jnp.max(s, axis=0, keepdims=True)

    def update(t, s_ref, x_ref, gs=all_chains):
        for g in gs:
            _, off = key_offset(g, t)
            v_den = jnp.concatenate([vt_ref[0, 0, :, pl.ds(off, blk)], ones_rows], axis=0)
            m_old = m_scr[g]
            m_new = jnp.maximum(m_old, x_ref[g])
            p_scr[g] = jnp.exp2(s_ref[g] - m_new).astype(BF16)
            acc_scr[g] = jnp.exp2(m_old - m_new) * acc_scr[g] + _dot(v_den, p_scr[g])
            m_scr[g] = m_new

    m_scr[...] = jnp.full(m_scr.shape, NEG_BIG, F32)
    acc_scr[...] = jnp.zeros(acc_scr.shape, F32)
    scores(0, sa_scr, xa_scr, True)
    steps = trips - 1

    def body(pair, carry):
        t = 2 * pair
        for g in all_chains:
            scores(t + 1, sb_scr, xb_scr, False, [g])
            update(t, sa_scr, xa_scr, [g])
        for g in all_chains:
            scores(t + 2, sa_scr, xa_scr, False, [g])
            update(t + 1, sb_scr, xb_scr, [g])
        return carry

    lax.fori_loop(0, steps >> 1, body, 0)

    @pl.when((steps & 1) == 1)
    def _():
        for g in all_chains:
            scores(steps, sb_scr, xb_scr, False, [g])
            update(steps - 1, sa_scr, xa_scr, [g])
        update(steps, sb_scr, xb_scr)

    @pl.when((steps & 1) == 0)
    def _():
        update(steps, sa_scr, xa_scr)

    for g in all_chains:
        o_ref[0, 0, :, g * blk:(g + 1) * blk] = (acc_scr[g, :DH_B, :] / acc_scr[g, DH_B:DH_B + 1, :]).astype(BF16)


def _attn_trips(cstat, kgain, qgain, chains):
    nbmax = jnp.swapaxes(cstat[:, :, 0, :N_HEADS_B], 1, 2)
    nbmin = jnp.swapaxes(cstat[:, :, 1, :N_HEADS_B], 1, 2)
    nblk = nbmax.shape[-1]
    xb = DH_B * jnp.max(jnp.abs(kgain)) * jnp.max(jnp.abs(qgain))
    slack = 2.0 * xb * SKIP_REL_MARGIN + SKIP_ABS_MARGIN + SKIP_LOG2
    j = jnp.arange(nblk)
    keep = (nbmax[:, :, None, :] - nbmin[:, :, :, None] + slack >= 0.0) & (j[None, :] <= j[:, None])
    jmin = jnp.min(jnp.where(keep, j[None, :], nblk), axis=-1)
    need = j - jnp.minimum(jmin, j) + 1
    return jnp.max(need.reshape(need.shape[0], need.shape[1], nblk // chains, chains), axis=-1).astype(jnp.int32)


def _attn(qaug, kaug, vt, trips, chains):
    bsz, nh, seq, _ = qaug.shape
    tq = chains * ATT_BLK
    grid_spec = pltpu.PrefetchScalarGridSpec(
        num_scalar_prefetch=1,
        grid=(bsz, nh, seq // tq),
        in_specs=[
            pl.BlockSpec((1, 1, tq, LANES), lambda b, h, i, tr: (b, h, i, 0)),
            pl.BlockSpec((1, 1, seq, LANES), lambda b, h, i, tr: (b, h, 0, 0)),
            pl.BlockSpec((1, 1, DH_B, seq), lambda b, h, i, tr: (b, h, 0, 0)),
        ],
        out_specs=pl.BlockSpec((1, 1, DH_B, tq), lambda b, h, i, tr: (b, h, 0, i)),
        scratch_shapes=[
            pltpu.VMEM((chains, 1, ATT_BLK), F32),
            pltpu.VMEM((chains, DH_B + DEN_ROWS, ATT_BLK), F32),
            pltpu.VMEM((chains, ATT_BLK, ATT_BLK), F32),
            pltpu.VMEM((chains, ATT_BLK, ATT_BLK), F32),
            pltpu.VMEM((chains, 1, ATT_BLK), F32),
            pltpu.VMEM((chains, 1, ATT_BLK), F32),
            pltpu.VMEM((chains, ATT_BLK, ATT_BLK), BF16),
        ],
    )
    return pl.pallas_call(
        functools.partial(_attn_kernel, chains=chains),
        grid_spec=grid_spec,
        out_shape=jax.ShapeDtypeStruct((bsz, nh, DH_B, seq), BF16),
        compiler_params=_params(("parallel", "parallel", "arbitrary")),
        name="attn",
    )(trips.reshape(-1), qaug, kaug, vt)


def _out_b_kernel(ot_ref, sgt_ref, h_ref, wout_ref, gn_ref, wr_ref, br_ref,
                  h3_ref, xs_ref, route_ref, counts_ref, *, tm):
    h_parts = []
    for p in range(tm // ROUTE_PIECE):
        rows = slice(p * ROUTE_PIECE, (p + 1) * ROUTE_PIECE)
        zt = (ot_ref[0, :, rows].astype(F32) * sgt_ref[0, :, rows].astype(F32)).astype(BF16)
        h3 = h_ref[rows, :] + _dot_tn(zt, wout_ref[...])
        h3_ref[rows, :] = h3
        h_parts.append(h3)
    first_step = (pl.program_id(0) == 0) & (pl.program_id(1) == 0)
    _ffn_norm_and_route(h_parts, gn_ref, wr_ref, br_ref, xs_ref, route_ref, counts_ref, first_step)


def _out_b(ot, sgt, h2, wout, gn, wr, br, bsz, seq, tm):
    ns = seq // tm
    row = lambda b, s: (b * ns + s, 0)
    const2 = lambda b, s: (0, 0)
    const3 = lambda b, s: (0, 0, 0)
    t = bsz * seq
    return pl.pallas_call(
        functools.partial(_out_b_kernel, tm=tm),
        grid=(bsz, ns),
        in_specs=[
            pl.BlockSpec((1, D_MODEL, tm), lambda b, s: (b, 0, s)),
            pl.BlockSpec((1, D_MODEL, tm), lambda b, s: (b, 0, s)),
            pl.BlockSpec((tm, D_MODEL), row),
            pl.BlockSpec((D_MODEL, D_MODEL), const2),
            pl.BlockSpec((1, D_MODEL), const2),
            pl.BlockSpec((2, D_MODEL, LANES), const3),
            pl.BlockSpec((1, LANES), const2),
        ],
        out_specs=_route_out_specs(tm, row, const2),
        out_shape=_route_out_shapes(t),
        compiler_params=_params(("arbitrary", "arbitrary")),
        name="out_b",
    )(ot, sgt, h2, wout, gn, wr, br)


def _hi_lo(w):
    hi = w.astype(BF16)
    lo = (w - hi.astype(F32)).astype(BF16)
    return jnp.stack([hi, lo])


def _pad_lanes(w, width=LANES):
    return jnp.pad(w, ((0, 0),) * (w.ndim - 1) + ((0, width - w.shape[-1]),))


def _router_params(w_grp, b_grp, w_exp, b_exp):
    w = _pad_lanes(jnp.concatenate([w_grp, w_exp], axis=-1))
    b = _pad_lanes(jnp.concatenate([b_grp, b_exp], axis=-1)[None, :])
    return _hi_lo(w), b


def _tile(seq, pref):
    t = pref
    while seq % t:
        t //= 2
    return t


def kernel(x, norm_mix, norm_ffn, a_w_in, a_b_gate, a_head_gain, a_w_out, kv_norm, kv_w, kv_b_f, kv_k_gain,
           b_w_in, b_q_gain, b_w_out, moe_w_grp, moe_b_grp, moe_w_exp, moe_b_exp, moe_w_gate, moe_w_up,
           moe_w_down, norm_final):
    bsz, seq, _ = x.shape
    t = bsz * seq
    assert seq % CHUNK == 0 and seq % ATT_BLK == 0
    tm = _tile(seq, 512)
    assert tm % ATT_BLK == 0
    x2d = x.reshape(t, D_MODEL)

    w_in = a_w_in[0]
    wgate = w_in[:, 2 * QK_A + 2 * D_MODEL:]
    wgc = _hi_lo(_pad_lanes(wgate))
    wgr = _hi_lo(wgate.T)
    bgc = _pad_lanes(a_b_gate[0][None, :])
    bgr = a_b_gate[0][:, None]
    q, k, v, o, gc, gr = _proj_a(x2d, norm_mix[0][None, :], w_in.astype(BF16), wgc, wgr, bgc, bgr, tm)
    hn = _mlstm(q, k, v, gc, gr, bsz, seq)
    wr0, br0 = _router_params(moe_w_grp[0], moe_b_grp[0], moe_w_exp[0], moe_b_exp[0])
    h1, xs1, route1, counts1 = _out_a(hn, o, x2d, a_head_gain[0][None, :], a_w_out[0].astype(BF16),
                                      norm_ffn[0][None, :], wr0, br0, tm)
    ys1, slots1 = _moe(xs1, route1, counts1, wr0[0], br0, moe_w_gate[0].astype(BF16), moe_w_up[0].astype(BF16),
              moe_w_down[0].astype(BF16))

    wkk = kv_w[:, :D_MODEL].astype(BF16)
    wvt = kv_w[:, D_MODEL:2 * D_MODEL].T.astype(BF16)
    wf = _hi_lo(_pad_lanes(kv_w[:, 2 * D_MODEL:]))
    bf = _pad_lanes(kv_b_f[None, :])
    wq1 = b_w_in[0][:, :D_MODEL].astype(BF16)
    wogt = b_w_in[0][:, D_MODEL:].T.astype(BF16)
    kgain = jnp.tile(kv_k_gain, N_HEADS_B)[None, :]
    qgain = jnp.tile(b_q_gain[0], N_HEADS_B)[None, :] * (DH_B ** -0.5 * LOG2E)
    head_of = jnp.arange(D_MODEL) // DH_B
    gsum = (head_of[:, None] == jnp.arange(LANES)[None, :]).astype(BF16) * (1.0 / DH_B)
    gexp = (jnp.arange(LANES)[:, None] == head_of[None, :]).astype(BF16)
    src = jnp.arange(LANES)[:, None]
    dst = jnp.arange(N_HEADS_B * LANES)[None, :]
    sel = ((src < 3 * N_HEADS_B) & (dst == (src % N_HEADS_B) * LANES + BIAS_LANE + src // N_HEADS_B)).astype(BF16)
    h2, kaug, vt, qaug, sgt, cstat = _proj_kvq(h1, ys1, slots1, kv_norm[None, :], norm_mix[1][None, :], wkk, wvt, wf, bf,
                                               wq1, wogt, kgain, qgain, gsum, gexp, sel, bsz, seq, tm)
    chains = math.gcd(ATT_CHAINS, seq // ATT_BLK)
    ot = _attn(qaug, kaug, vt, _attn_trips(cstat, kgain, qgain, chains), chains)
    wr1, br1 = _router_params(moe_w_grp[1], moe_b_grp[1], moe_w_exp[1], moe_b_exp[1])
    h3, xs3, route3, counts3 = _out_b(ot.reshape(bsz, D_MODEL, seq), sgt, h2, b_w_out[0].astype(BF16),
                                      norm_ffn[1][None, :], wr1, br1, bsz, seq, tm)
    ys3, slots3 = _moe(xs3, route3, counts3, wr1[0], br1, moe_w_gate[1].astype(BF16), moe_w_up[1].astype(BF16),
              moe_w_down[1].astype(BF16))
    out = _final(h3, ys3, slots3, norm_final[None, :], tm)
    return out.reshape(bsz, seq, D_MODEL)
```
